```python
import math
import jax
import jax.numpy as jnp
from jax import lax
import numpy as np

D_MODEL = 1024
BATCH = 4
SEQ = 4096
DEPTH = 2

GRID_W = 64
CTX_LEN = 256
N_BRANCH = 4
W_BR = 256
N_MOD = 6
NORM_EPS = 1e-6
F32 = jnp.float32

HY_SHORT = 3
HY_BANDS = 8
HY_EMB = 2 * HY_BANDS + 1
HY_HID = 64
HY_TARGET = 1e-2
HY_FAST_PCT = 0.3
HY_SLOW_PCT = 1.5

S5_GROUP = 16
S5_GROUPS = W_BR // S5_GROUP
S5_STATE = 64

RW_HEAD = 64
RW_HEADS = W_BR // RW_HEAD
RW_W_RANK = 32
RW_A_RANK = 32
RW_G_RANK = 64
RW_GN_EPS = 64e-5

M2_HEADDIM = 64
M2_HEADS = W_BR // M2_HEADDIM
M2_GROUPS = 2
M2_STATE = 128
M2_CONV = 3

MOE_GROUPS = 4
MOE_PER_GROUP = 4
MOE_EXPERTS = MOE_GROUPS * MOE_PER_GROUP
MOE_TOPK = 2
MOE_FF = 512

HY_COLS = 3 * W_BR
S5_COLS = W_BR
RW_COLS = 3 * W_BR + 2 * RW_W_RANK + 2 * RW_A_RANK + RW_G_RANK
M2_XBC = W_BR + 2 * M2_GROUPS * M2_STATE
M2_COLS = W_BR + M2_XBC + 2 * M2_HEADS
GATE_COLS = N_BRANCH * D_MODEL
IN_COLS = HY_COLS + S5_COLS + RW_COLS + M2_COLS + GATE_COLS

kernel_name = 'hybrid_parallel_mixers_hmoe_dit'


def _split(x, sizes):
    idx = [int(i) for i in np.cumsum(sizes)[:-1]]
    return jnp.split(x, idx, axis=-1)


def rmsnorm(x, g):
    xf = x.astype(F32)
    xf = xf * lax.rsqrt(jnp.mean(xf * xf, axis=-1, keepdims=True) + NORM_EPS)
    return xf.astype(x.dtype) * g


def modulate(x, g, shift, scale):
    return rmsnorm(x, g) * (1 + scale) + shift


def centred_conv(x, w, b):
    k = w.shape[0]
    pad = k // 2
    L = x.shape[1]
    xp = jnp.pad(x, ((0, 0), (pad, pad), (0, 0)))
    y = b
    for j in range(k):
        y = y + xp[:, j:j + L] * w[j]
    return y


def centred_shift(x):
    xp = jnp.pad(x, ((0, 0), (1, 1), (0, 0)))
    return 0.5 * (xp[:, :-2] + xp[:, 2:])


def hyena_filters(L, w1, b1, w2, b2, w3, freq):
    t = jnp.linspace(0.0, 1.0, L, dtype=F32)[:, None]
    w = 2.0 * math.pi * jnp.arange(L, dtype=F32)[:, None] / L
    f = jnp.linspace(1e-4, HY_BANDS - 1, HY_BANDS, dtype=F32)[None, :]
    feats = jnp.concatenate([t, jnp.cos(f * w), -jnp.sin(f * w)], axis=-1)
    fr = freq.astype(F32)
    h = jnp.sin(fr * (feats @ w1.astype(F32) + b1.astype(F32)))
    h = jnp.sin(fr * (h @ w2.astype(F32) + b2.astype(F32)))
    h = h @ w3.astype(F32)
    deltas = jnp.abs(jnp.linspace(math.log(HY_TARGET) / HY_SLOW_PCT,
                                  math.log(HY_TARGET) / HY_FAST_PCT, W_BR, dtype=F32))
    decay = jnp.exp(-t * deltas)
    return h * jnp.concatenate([decay, decay], axis=-1)


def hyena_mixer(p, conv_w, conv_b, f_w1, f_b1, f_w2, f_b2, f_w3, f_freq, h_bias):
    L = p.shape[1]
    x0, x1, v = jnp.split(centred_conv(p, conv_w, conv_b), 3, axis=-1)
    filt = hyena_filters(L, f_w1, f_b1, f_w2, f_b2, f_w3, f_freq)
    h_f, h_b = filt[:, :W_BR], filt[:, W_BR:]
    filt_full = jnp.concatenate([h_f, jnp.zeros((1, W_BR), F32), h_b[:0:-1]], axis=0)
    u = (x1 * v).astype(F32)
    n = 2 * L
    y = jnp.fft.irfft(jnp.fft.rfft(u, n=n, axis=1) * jnp.fft.rfft(filt_full, axis=0)[None],
                      n=n, axis=1)[:, :L]
    y = y + h_bias.astype(F32) * u
    return x0 * y.astype(p.dtype)


def _lin_rec(e1, e2):
    a1, b1 = e1
    a2, b2 = e2
    return a1 * a2, a2 * b1 + b2


def s5_mixer(u, h0, lam_re, lam_im, log_step, b_re, b_im, c_re, c_im, d_skip, w_glu):
    b_, L, _ = u.shape
    uc = u.astype(F32).reshape(b_, L, S5_GROUPS, S5_GROUP).astype(jnp.complex64)
    y = d_skip * u
    finals = []
    for d in range(2):
        lam = lax.complex(lam_re[d].astype(F32), lam_im[d].astype(F32))
        lam_bar = jnp.exp(lam * jnp.exp(log_step[d].astype(F32))[:, None])
        b_bar = ((lam_bar - 1.0) / lam)[:, :, None] * lax.complex(b_re[d].astype(F32), b_im[d].astype(F32))
        seq = uc if d == 0 else jnp.flip(uc, 1)
        bu = jnp.einsum('blgi,gni->blgn', seq, b_bar)
        bu = bu.at[:, 0].add(lam_bar * h0[d])
        _, xs = lax.associative_scan(_lin_rec, (jnp.broadcast_to(lam_bar, bu.shape), bu), axis=1)
        finals.append(xs[:, -1])
        yd = jnp.einsum('blgn,gin->blgi', xs,
                        lax.complex(c_re[d].astype(F32), c_im[d].astype(F32))).real
        if d == 1:
            yd = jnp.flip(yd, 1)
        y = y + yd.reshape(b_, L, W_BR).astype(u.dtype)
    y = jax.nn.gelu(y)
    return y * jax.nn.sigmoid(y @ w_glu), jnp.stack(finals)


def wkv7(r, w, k, v, kk, a, s0, reverse):
    def step(s, inp):
        r_t, w_t, k_t, v_t, kk_t, a_t = inp
        sa = jnp.einsum('bhvk,bhk->bhv', s, kk_t)
        s = (s * w_t[:, :, None, :] - sa[..., None] * (kk_t * a_t)[:, :, None, :]
             + v_t[..., None] * k_t[:, :, None, :])
        return s, jnp.einsum('bhvk,bhk->bhv', s, r_t)
    xs = tuple(jnp.moveaxis(t, 1, 0) for t in (r, w, k, v, kk, a))
    s_final, y = lax.scan(step, s0, xs, reverse=reverse)
    return jnp.moveaxis(y, 0, 1), s_final


def rwkv_mixer(p, s0, mu, w0, w2, a0, a2, g2, k_k, k_a, r_k, ln_w, ln_b):
    out_dtype = p.dtype
    b_, L, _ = p.shape
    p = (p + (centred_shift(p) - p) * mu).astype(F32)
    r, k, v, wd, ad, gd = _split(p, [W_BR, W_BR, W_BR, 2 * RW_W_RANK, 2 * RW_A_RANK, RW_G_RANK])
    heads = lambda t: t.reshape(b_, L, RW_HEADS, RW_HEAD)
    g = jax.nn.sigmoid(gd) @ g2.astype(F32)
    kk = heads(k * k_k)
    kk = kk * lax.rsqrt(jnp.maximum(jnp.sum(kk * kk, -1, keepdims=True), 1e-24))
    rh, kh, vh = heads(r), heads(k), heads(v)
    y = None
    finals = []
    for d in range(2):
        w = w0[d] + jnp.tanh(wd[..., d * RW_W_RANK:(d + 1) * RW_W_RANK]) @ w2[d].astype(F32)
        decay = jnp.exp(-jnp.exp(-jax.nn.softplus(-w) - 0.5))
        a = jax.nn.sigmoid(a0[d] + ad[..., d * RW_A_RANK:(d + 1) * RW_A_RANK] @ a2[d].astype(F32))
        kd = heads(k * (1 + (a - 1) * k_a))
        yd, sf = wkv7(rh, heads(decay), kd, vh, kk, heads(a), s0[d], d == 1)
        y = yd if d == 0 else y + yd
        finals.append(sf)
    mean = jnp.mean(y, -1, keepdims=True)
    var = jnp.mean(jnp.square(y - mean), -1, keepdims=True)
    y = ((y - mean) * lax.rsqrt(var + RW_GN_EPS)).reshape(b_, L, W_BR) * ln_w + ln_b
    bonus = jnp.sum(rh * kh * r_k, -1, keepdims=True) * vh
    y = (y + bonus.reshape(b_, L, W_BR)) * g
    return y.astype(out_dtype), jnp.stack(finals)


def ssd_scan(xh, dt, a, bm, cm, h0, n_chunks):
    b_, L, H, P = xh.shape
    q = L // n_chunks
    rep = H // M2_GROUPS
    bh = jnp.repeat(bm, rep, axis=2).reshape(b_, n_chunks, q, H, M2_STATE)
    ch = jnp.repeat(cm, rep, axis=2).reshape(b_, n_chunks, q, H, M2_STATE)
    xdt = (xh * dt[..., None]).reshape(b_, n_chunks, q, H, P)
    acs = jnp.cumsum((dt * a).reshape(b_, n_chunks, q, H), axis=2)
    lower = jnp.tril(jnp.ones((q, q), bool))[None, None, :, :, None]
    seg = acs[:, :, :, None, :] - acs[:, :, None, :, :]
    decay_in = jnp.exp(jnp.where(lower, seg, -jnp.inf))
    scores = jnp.einsum('bcihn,bcjhn->bcijh', ch, bh) * decay_in
    y_diag = jnp.einsum('bcijh,bcjhp->bcihp', scores, xdt)
    decay_out = jnp.exp(acs[:, :, -1:, :] - acs)
    states = jnp.einsum('bcjhn,bcjh,bcjhp->bchpn', bh, decay_out, xdt)
    chunk_decay = jnp.exp(acs[:, :, -1, :])

    def step(h, inp):
        st, dec = inp
        return h * dec[:, :, None, None] + st, h

    h_final, h_enter = lax.scan(step, h0, (jnp.moveaxis(states, 1, 0), jnp.moveaxis(chunk_decay, 1, 0)))
    h_enter = jnp.moveaxis(h_enter, 0, 1)
    y_off = jnp.einsum('bcihn,bchpn,bcih->bcihp', ch, h_enter, jnp.exp(acs))
    return (y_diag + y_off).reshape(b_, L, H, P), h_final


def mamba_mixer(p, h0, n_chunks, conv_w, conv_b, a_log, dt_bias, d_skip, norm_w):
    b_, L, _ = p.shape
    z, xbc, dt = _split(p, [W_BR, M2_XBC, 2 * M2_HEADS])
    xbc = jax.nn.silu(centred_conv(xbc, conv_w, conv_b)).astype(F32)
    xs, bm, cm = _split(xbc, [W_BR, M2_GROUPS * M2_STATE, M2_GROUPS * M2_STATE])
    xh = xs.reshape(b_, L, M2_HEADS, M2_HEADDIM)
    bm = bm.reshape(b_, L, M2_GROUPS, M2_STATE)
    cm = cm.reshape(b_, L, M2_GROUPS, M2_STATE)
    y = d_skip.astype(F32)[:, None] * xh
    finals = []
    for d in range(2):
        dtd = jax.nn.softplus(dt[..., d * M2_HEADS:(d + 1) * M2_HEADS].astype(F32) + dt_bias[d].astype(F32))
        a = -jnp.exp(a_log[d].astype(F32))
        seqs = (xh, dtd, bm, cm) if d == 0 else tuple(jnp.flip(t, 1) for t in (xh, dtd, bm, cm))
        yd, hf = ssd_scan(seqs[0], seqs[1], a, seqs[2], seqs[3], h0[d], n_chunks)
        y = y + (yd if d == 0 else jnp.flip(yd, 1))
        finals.append(hf)
    y = y.reshape(b_, L, W_BR).astype(p.dtype) * jax.nn.silu(z)
    return rmsnorm(y, norm_w), jnp.stack(finals)


def merge(ys, gate_cols, w_branch, w_out):
    gates = jax.nn.sigmoid(gate_cols)
    out = sum(gates[..., i * D_MODEL:(i + 1) * D_MODEL] * (ys[i] @ w_branch[i]) for i in range(N_BRANCH))
    return out @ w_out


def moe_ffn(u, w_group, w_expert, w_gate, w_up, w_down):
    uf = u.astype(F32)
    grp_prob = jax.nn.softmax(uf @ w_group.astype(F32), axis=-1)
    grp_p, grp_idx = lax.top_k(grp_prob, 1)
    exp_logits = jnp.einsum('bld,gde->blge', uf, w_expert.astype(F32))
    sel = jnp.einsum('blge,blg->ble', exp_logits, jax.nn.one_hot(grp_idx[..., 0], MOE_GROUPS, dtype=F32))
    top_v, top_i = lax.top_k(sel, MOE_TOPK)
    top_w = jax.nn.softmax(top_v, axis=-1) * grp_p
    e_idx = grp_idx * MOE_PER_GROUP + top_i
    comb = jnp.sum(jax.nn.one_hot(e_idx, MOE_EXPERTS, dtype=F32) * top_w[..., None], axis=-2).astype(u.dtype)
    y = jnp.zeros_like(u)
    for e in range(MOE_EXPERTS):
        h = jax.nn.silu(u @ w_gate[e]) * (u @ w_up[e])
        y = y + comb[..., e:e + 1] * (h @ w_down[e])
    return y


def hybrid_layer(x, cx, c, c_ctx, mod_w, mod_b, norm_mix, norm_ffn, w_in, hy, s5, rw, m2,
                 w_branch, w_out, moe, rows, ctx_rows, ctx_out):
    mod_x = jax.nn.silu(c) @ mod_w + mod_b
    mod_c = jax.nn.silu(c_ctx) @ mod_w + mod_b
    sh1, sc1, g1, sh2, sc2, g2 = jnp.split(mod_x[:, None, :], N_MOD, axis=-1)
    csh1, csc1, cg1, csh2, csc2, cg2 = jnp.split(mod_c, N_MOD, axis=-1)

    px = modulate(x, norm_mix, sh1, sc1) @ w_in
    pc = modulate(cx, norm_mix, csh1, csc1) @ w_in
    sizes = [HY_COLS, S5_COLS, RW_COLS, M2_COLS, GATE_COLS]
    hy_x, s5_x, rw_x, m2_x, gate_x = _split(px, sizes)
    hy_c, s5_c, rw_c, m2_c, gate_c = _split(pc, sizes)

    b_ = x.shape[0]
    s5_0 = jnp.zeros((2, b_, S5_GROUPS, S5_STATE), jnp.complex64)
    rw_0 = jnp.zeros((2, b_, RW_HEADS, RW_HEAD, RW_HEAD), F32)
    m2_0 = jnp.zeros((2, b_, M2_HEADS, M2_HEADDIM, M2_STATE), F32)

    y_s5c, s5_h = s5_mixer(s5_c, s5_0, *s5)
    y_rwc, rw_h = rwkv_mixer(rw_c, rw_0, *rw)
    y_m2c, m2_h = mamba_mixer(m2_c, m2_0, ctx_rows, *m2)

    ys_x = [hyena_mixer(hy_x, *hy),
            s5_mixer(s5_x, s5_h, *s5)[0],
            rwkv_mixer(rw_x, rw_h, *rw)[0],
            mamba_mixer(m2_x, m2_h, rows, *m2)[0]]
    x = x + g1 * merge(ys_x, gate_x, w_branch, w_out)
    x = x + g2 * moe_ffn(modulate(x, norm_ffn, sh2, sc2), *moe)

    if ctx_out:
        ys_c = [hyena_mixer(hy_c, *hy), y_s5c, y_rwc, y_m2c]
        cx = cx + cg1 * merge(ys_c, gate_c, w_branch, w_out)
        cx = cx + cg2 * moe_ffn(modulate(cx, norm_ffn, csh2, csc2), *moe)
    return x, cx


def setup_inputs(seed: int = 0) -> dict:
    key = jax.random.key(seed)
    ks = iter(jax.random.split(key, 64))
    nrm = lambda shape, s=1.0: s * jax.random.normal(next(ks), shape, F32)
    uni = lambda shape, lo, hi: jax.random.uniform(next(ks), shape, F32, lo, hi)
    Ld = DEPTH
    inp = {}
    inp['x'] = nrm((BATCH, SEQ, D_MODEL))
    inp['c'] = nrm((BATCH, D_MODEL))
    inp['ctx'] = nrm((BATCH, CTX_LEN, D_MODEL))
    inp['c_ctx'] = nrm((D_MODEL,))
    inp['mod_w'] = nrm((Ld, D_MODEL, N_MOD * D_MODEL), 0.02)
    inp['mod_b'] = nrm((Ld, N_MOD * D_MODEL), 0.02)
    inp['norm_mix'] = 1.0 + nrm((Ld, D_MODEL), 0.02)
    inp['norm_ffn'] = 1.0 + nrm((Ld, D_MODEL), 0.02)
    inp['w_in'] = nrm((Ld, D_MODEL, IN_COLS), D_MODEL ** -0.5)
    inp['hy_conv_w'] = nrm((Ld, HY_SHORT, HY_COLS), 0.5)
    inp['hy_conv_b'] = nrm((Ld, HY_COLS), 0.02)
    inp['hy_f_w1'] = nrm((Ld, HY_EMB, HY_HID), HY_EMB ** -0.5)
    inp['hy_f_b1'] = nrm((Ld, HY_HID), 0.1)
    inp['hy_f_w2'] = nrm((Ld, HY_HID, HY_HID), HY_HID ** -0.5)
    inp['hy_f_b2'] = nrm((Ld, HY_HID), 0.1)
    inp['hy_f_w3'] = nrm((Ld, HY_HID, 2 * W_BR), 0.02)
    inp['hy_f_freq'] = 1.0 + nrm((Ld, HY_HID), 0.1)
    inp['hy_bias'] = nrm((Ld, W_BR), 0.5)
    inp['s5_lam_re'] = -0.5 + nrm((Ld, 2, S5_GROUPS, S5_STATE), 0.01)
    inp['s5_lam_im'] = math.pi * jnp.arange(S5_STATE, dtype=F32) + nrm((Ld, 2, S5_GROUPS, S5_STATE), 0.01)
    inp['s5_log_step'] = uni((Ld, 2, S5_GROUPS), math.log(1e-3), math.log(1e-1))
    inp['s5_b_re'] = nrm((Ld, 2, S5_GROUPS, S5_STATE, S5_GROUP), (2 * S5_GROUP) ** -0.5)
    inp['s5_b_im'] = nrm((Ld, 2, S5_GROUPS, S5_STATE, S5_GROUP), (2 * S5_GROUP) ** -0.5)
    inp['s5_c_re'] = nrm((Ld, 2, S5_GROUPS, S5_GROUP, S5_STATE), S5_STATE ** -0.5)
    inp['s5_c_im'] = nrm((Ld, 2, S5_GROUPS, S5_GROUP, S5_STATE), S5_STATE ** -0.5)
    inp['s5_d'] = nrm((Ld, W_BR))
    inp['s5_w_glu'] = nrm((Ld, W_BR, W_BR), W_BR ** -0.5)
    ramp = jnp.arange(W_BR, dtype=F32) / (W_BR - 1)
    inp['rw_mu'] = uni((Ld, RW_COLS), 0.0, 1.0)
    inp['rw_w0'] = -6.0 + 5.0 * ramp ** 0.7 + nrm((Ld, 2, W_BR), 0.1)
    inp['rw_w2'] = nrm((Ld, 2, RW_W_RANK, W_BR), 0.1)
    inp['rw_a0'] = nrm((Ld, 2, W_BR), 0.1)
    inp['rw_a2'] = nrm((Ld, 2, RW_A_RANK, W_BR), 0.1)
    inp['rw_g2'] = nrm((Ld, RW_G_RANK, W_BR), RW_G_RANK ** -0.5)
    inp['rw_k_k'] = 0.85 + nrm((Ld, W_BR), 0.02)
    inp['rw_k_a'] = 1.0 + nrm((Ld, W_BR), 0.02)
    inp['rw_r_k'] = nrm((Ld, RW_HEADS, RW_HEAD), 0.1)
    inp['rw_ln_w'] = 1.0 + nrm((Ld, W_BR), 0.02)
    inp['rw_ln_b'] = nrm((Ld, W_BR), 0.02)
    inp['m2_conv_w'] = nrm((Ld, M2_CONV, M2_XBC), 0.5)
    inp['m2_conv_b'] = nrm((Ld, M2_XBC), 0.02)
    inp['m2_a_log'] = jnp.log(uni((Ld, 2, M2_HEADS), 1.0, 16.0))
    dt0 = jnp.exp(uni((Ld, 2, M2_HEADS), math.log(1e-3), math.log(1e-1)))
    inp['m2_dt_bias'] = dt0 + jnp.log(-jnp.expm1(-dt0))
    inp['m2_d'] = 1.0 + nrm((Ld, M2_HEADS), 0.1)
    inp['m2_norm_w'] = 1.0 + nrm((Ld, W_BR), 0.02)
    inp['w_branch'] = nrm((Ld, N_BRANCH, W_BR, D_MODEL), W_BR ** -0.5)
    inp['w_out'] = nrm((Ld, D_MODEL, D_MODEL), D_MODEL ** -0.5)
    inp['moe_w_group'] = nrm((Ld, D_MODEL, MOE_GROUPS), D_MODEL ** -0.5)
    inp['moe_w_expert'] = nrm((Ld, MOE_GROUPS, D_MODEL, MOE_PER_GROUP), D_MODEL ** -0.5)
    inp['moe_w_gate'] = nrm((Ld, MOE_EXPERTS, D_MODEL, MOE_FF), D_MODEL ** -0.5)
    inp['moe_w_up'] = nrm((Ld, MOE_EXPERTS, D_MODEL, MOE_FF), D_MODEL ** -0.5)
    inp['moe_w_down'] = nrm((Ld, MOE_EXPERTS, MOE_FF, D_MODEL), MOE_FF ** -0.5)
    inp['norm_final'] = 1.0 + nrm((D_MODEL,), 0.02)
    return inp


def reference(x, c, ctx, c_ctx, mod_w, mod_b, norm_mix, norm_ffn, w_in,
              hy_conv_w, hy_conv_b, hy_f_w1, hy_f_b1, hy_f_w2, hy_f_b2, hy_f_w3, hy_f_freq, hy_bias,
              s5_lam_re, s5_lam_im, s5_log_step, s5_b_re, s5_b_im, s5_c_re, s5_c_im, s5_d, s5_w_glu,
              rw_mu, rw_w0, rw_w2, rw_a0, rw_a2, rw_g2, rw_k_k, rw_k_a, rw_r_k, rw_ln_w, rw_ln_b,
              m2_conv_w, m2_conv_b, m2_a_log, m2_dt_bias, m2_d, m2_norm_w,
              w_branch, w_out,
              moe_w_group, moe_w_expert, moe_w_gate, moe_w_up, moe_w_down,
              norm_final):
    rows = x.shape[1] // GRID_W
    ctx_rows = ctx.shape[1] // GRID_W
    cx = ctx
    for l in range(DEPTH):
        hy = (hy_conv_w[l], hy_conv_b[l], hy_f_w1[l], hy_f_b1[l], hy_f_w2[l], hy_f_b2[l],
              hy_f_w3[l], hy_f_freq[l], hy_bias[l])
        s5 = (s5_lam_re[l], s5_lam_im[l], s5_log_step[l], s5_b_re[l], s5_b_im[l],
              s5_c_re[l], s5_c_im[l], s5_d[l], s5_w_glu[l])
        rw = (rw_mu[l], rw_w0[l], rw_w2[l], rw_a0[l], rw_a2[l], rw_g2[l], rw_k_k[l], rw_k_a[l],
              rw_r_k[l], rw_ln_w[l], rw_ln_b[l])
        m2 = (m2_conv_w[l], m2_conv_b[l], m2_a_log[l], m2_dt_bias[l], m2_d[l], m2_norm_w[l])
        moe = (moe_w_group[l], moe_w_expert[l], moe_w_gate[l], moe_w_up[l], moe_w_down[l])
        x, cx = hybrid_layer(x, cx, c, c_ctx, mod_w[l], mod_b[l], norm_mix[l], norm_ffn[l], w_in[l],
                             hy, s5, rw, m2, w_branch[l], w_out[l], moe, rows, ctx_rows, l < DEPTH - 1)
    return rmsnorm(x, norm_final)
```

```python
import functools
import math

import numpy as np
import jax
import jax.numpy as jnp
from jax import lax
from jax.experimental import pallas as pl
from jax.experimental.pallas import tpu as pltpu

F32 = jnp.float32
BF16 = jnp.bfloat16
HI = lax.Precision.HIGHEST

D_MODEL = 1024
W_BR = 256
N_BRANCH = 4
N_MOD = 6
NORM_EPS = 1e-6
GRID_W = 64

HY_BANDS = 8
HY_HID = 64
HY_TARGET = 1e-2
HY_FAST_PCT = 0.3
HY_SLOW_PCT = 1.5
HY_BLOCK = 512

S5_GROUP = 16
S5_GROUPS = 16
S5_STATE = 64
S5_LANES = S5_GROUPS * S5_STATE
S5_CHUNK = 256

RW_HEAD = 64
RW_HEADS = 4
RW_W_RANK = 32
RW_A_RANK = 32
RW_G_RANK = 64
RW_GN_EPS = 64e-5
RW_SUB = 64
RW_STEP = 256

M2_HEADDIM = 64
M2_HEADS = 4
M2_GROUPS = 2
M2_STATE = 128
M2_CHUNK = 256

MOE_GROUPS = 4
MOE_PER_GROUP = 4
MOE_EXPERTS = 16
MOE_FF = 512

HY_COLS = 3 * W_BR
S5_COLS = W_BR
RW_COLS = 3 * W_BR + 2 * RW_W_RANK + 2 * RW_A_RANK + RW_G_RANK
M2_XBC = W_BR + 2 * M2_GROUPS * M2_STATE
M2_COLS = W_BR + M2_XBC + 2 * M2_HEADS
RW_PAD = 1024
M2_PAD = 1152
LANE = 128
SUBLANE = 8

VMEM_LIMIT = 56 * 1024 * 1024


def _dot(a, b, prec=None):
    return jnp.dot(a, b, preferred_element_type=F32, precision=prec)


def _dot_nt(a, b, prec=None):
    return lax.dot_general(a, b, (((1,), (1,)), ((), ())), preferred_element_type=F32, precision=prec)


def _dot_tn(a, b, prec=None):
    return lax.dot_general(a, b, (((0,), (0,)), ((), ())), preferred_element_type=F32, precision=prec)


def _bdot(a, b):
    return jnp.dot(a.astype(BF16), b.astype(BF16), preferred_element_type=F32)


def _sigmoid(x):
    return 1.0 / (1.0 + jnp.exp(-x))


def _silu(x):
    return x * _sigmoid(x)


def _params(sem):
    return pltpu.CompilerParams(dimension_semantics=sem, vmem_limit_bytes=VMEM_LIMIT)


def _rms(x):
    return x * lax.rsqrt(jnp.mean(x * x, axis=-1, keepdims=True) + NORM_EPS)


def _modulate(x, g, shift, scale):
    return _rms(x) * g * (1.0 + scale) + shift


def _iota(shape, axis):
    return lax.broadcasted_iota(jnp.int32, shape, axis)


def _shift_rows(cur, prev_row, next_row):
    t = cur.shape[0]
    row = _iota(cur.shape, 0)
    xm = jnp.where(row == 0, prev_row, pltpu.roll(cur, 1, 0))
    xp = jnp.where(row == t - 1, next_row, pltpu.roll(cur, t - 1, 0))
    return xm, xp


def _halo_specs(t, n_blocks, width, col_block, chunk_of):
    per = t // SUBLANE
    last = n_blocks * per - 1

    def prev_map(b, s):
        return (b, jnp.maximum(chunk_of(s) * per - 1, 0), col_block)

    def next_map(b, s):
        return (b, jnp.minimum((chunk_of(s) + 1) * per, last), col_block)

    return (pl.BlockSpec((1, SUBLANE, width), prev_map), pl.BlockSpec((1, SUBLANE, width), next_map))


def _halo_rows(prev_ref, next_ref, chunk, n_chunks):
    prev_row = jnp.where(chunk == 0, 0.0, prev_ref[0, SUBLANE - 1:SUBLANE, :])
    next_row = jnp.where(chunk == n_chunks - 1, 0.0, next_ref[0, 0:1, :])
    return prev_row, next_row


def _mod_kernel(c_ref, w_ref, b_ref, o_ref):
    o_ref[0] = _dot(_silu(c_ref[...]), w_ref[0], HI) + b_ref[0]


def adaln_mod(cvec, mod_w, mod_b):
    depth = mod_w.shape[0]
    return pl.pallas_call(
        _mod_kernel,
        grid=(depth, N_MOD),
        in_specs=[pl.BlockSpec((SUBLANE, D_MODEL), lambda l, j: (0, 0)),
                  pl.BlockSpec((1, D_MODEL, D_MODEL), lambda l, j: (l, 0, j)),
                  pl.BlockSpec((1, 1, D_MODEL), lambda l, j: (l, 0, j))],
        out_specs=pl.BlockSpec((1, SUBLANE, D_MODEL), lambda l, j: (l, 0, j)),
        out_shape=jax.ShapeDtypeStruct((depth, SUBLANE, N_MOD * D_MODEL), F32),
        compiler_params=_params(("parallel", "parallel")),
    )(cvec, mod_w, mod_b.reshape(depth, 1, N_MOD * D_MODEL))


def _inproj_kernel(x_ref, g_ref, sh_ref, sc_ref, w_ref, hy_ref, s5_ref, rw_ref, m2_ref):
    h = _modulate(x_ref[...], g_ref[...], sh_ref[0], sc_ref[0]).astype(BF16)
    o = 0
    for ref in (hy_ref, s5_ref, rw_ref, m2_ref):
        n = ref.shape[-1]
        ref[...] = _dot(h, w_ref[:, o:o + n])
        o += n


def in_projection(xt, g, shift, scale, w_mix, rows_per_mod, tm):
    n = xt.shape[0]
    tiles_per_mod = rows_per_mod // tm
    widths = (HY_COLS, S5_COLS, RW_PAD, M2_PAD)
    mod_spec = pl.BlockSpec((1, 1, D_MODEL), lambda i: (i // tiles_per_mod, 0, 0))
    return pl.pallas_call(
        _inproj_kernel,
        grid=(n // tm,),
        in_specs=[pl.BlockSpec((tm, D_MODEL), lambda i: (i, 0)),
                  pl.BlockSpec((1, D_MODEL), lambda i: (0, 0)),
                  mod_spec, mod_spec,
                  pl.BlockSpec(w_mix.shape, lambda i: (0, 0))],
        out_specs=[pl.BlockSpec((tm, w), lambda i: (i, 0)) for w in widths],
        out_shape=[jax.ShapeDtypeStruct((n, w), F32) for w in widths],
        compiler_params=_params(("parallel",)),
    )(xt, g, shift, scale, w_mix)


def _dft_mats(t):
    k = np.arange(t, dtype=np.float64)[:, None]
    n = np.arange(2 * t, dtype=np.float64)[None, :]
    ang = np.pi * (2.0 * k + 1.0) * n / (2.0 * t)
    fwd = np.concatenate([np.cos(ang), -np.sin(ang)], axis=0)
    inv = fwd[:, :t].T / t
    return fwd.astype(np.float32), inv.astype(np.float32)


def _hyfilt_kernel(w1_ref, b1_ref, w2_ref, b2_ref, w3_ref, fr_ref, o_ref):
    L = o_ref.shape[0]
    shape = (L, LANE)
    pos = _iota(shape, 0).astype(F32)
    lane = _iota(shape, 1)
    t = pos / (L - 1)
    w = (2.0 * math.pi / L) * pos
    band = jnp.where(lane >= 1 + HY_BANDS, lane - 1 - HY_BANDS, lane - 1).astype(F32)
    f = 1e-4 + band * ((HY_BANDS - 1 - 1e-4) / (HY_BANDS - 1))
    arg = f * w
    feats = jnp.where(lane == 0, t,
                      jnp.where(lane <= HY_BANDS, jnp.cos(arg),
                                jnp.where(lane <= 2 * HY_BANDS, -jnp.sin(arg), 0.0)))
    fr = fr_ref[...]
    h = jnp.sin(fr * (_dot(feats, w1_ref[...], HI) + b1_ref[...]))
    h = jnp.sin(fr * (_dot(h, w2_ref[...], HI) + b2_ref[...]))
    h = _dot(h, w3_ref[...], HI)
    ch = _iota((L, W_BR), 1).astype(F32)
    lo = math.log(HY_TARGET) / HY_SLOW_PCT
    hi = math.log(HY_TARGET) / HY_FAST_PCT
    deltas = jnp.abs(lo + ch * ((hi - lo) / (W_BR - 1)))
    decay = jnp.exp(-(t[:, 0:1]) * deltas)
    o_ref[:, 0:W_BR] = h[:, 0:W_BR] * decay
    o_ref[:, W_BR:2 * W_BR] = h[:, W_BR:2 * W_BR] * decay


def hyena_filter(L, f_w1, f_b1, f_w2, f_b2, f_w3, f_freq):
    pad = lambda a, r, c: jnp.zeros((r, c), F32).at[:a.shape[0], :a.shape[1]].set(a)
    args = (pad(f_w1, LANE, LANE), pad(f_b1[None], 1, LANE), pad(f_w2, LANE, LANE), pad(f_b2[None], 1, LANE),
            pad(f_w3, LANE, 2 * W_BR), pad(f_freq[None], 1, LANE))
    return pl.pallas_call(
        _hyfilt_kernel,
        out_shape=jax.ShapeDtypeStruct((L, 2 * W_BR), F32),
        compiler_params=_params(None),
    )(*args)


def _ghat_kernel(f_ref, g_ref, o_ref):
    o_ref[0] = _dot(f_ref[...], g_ref[0], HI)


def hyena_filter_spectra(filt, L, t):
    nb = L // t
    h_f, h_b = filt[:, :W_BR], filt[:, W_BR:]
    kext = jnp.concatenate([jnp.zeros((1, W_BR), F32), h_b[:0:-1], h_f], axis=0)
    segs = []
    for d in range(-(nb - 1), nb):
        c = L + d * t
        segs.append(jnp.concatenate([kext[c:c + t], jnp.zeros((1, W_BR), F32), -kext[c - t + 1:c]], axis=0))
    g = jnp.stack(segs)
    fwd, _ = _dft_mats(t)
    return pl.pallas_call(
        _ghat_kernel,
        grid=(2 * nb - 1,),
        in_specs=[pl.BlockSpec((2 * t, 2 * t), lambda d: (0, 0)),
                  pl.BlockSpec((1, 2 * t, W_BR), lambda d: (d, 0, 0))],
        out_specs=pl.BlockSpec((1, 2 * t, W_BR), lambda d: (d, 0, 0)),
        out_shape=jax.ShapeDtypeStruct((2 * nb - 1, 2 * t, W_BR), F32),
        compiler_params=_params(("parallel",)),
    )(jnp.asarray(fwd), g)


def _hypre_kernel(nb, cur_ref, prev_ref, next_ref, cw_ref, cb_ref, f_ref, x0_ref, u_ref, uh_ref):
    j = pl.program_id(1)
    cur = cur_ref[0]
    prev_row, next_row = _halo_rows(prev_ref, next_ref, j, nb)
    xm, xp = _shift_rows(cur, prev_row, next_row)
    pc = cb_ref[...] + xm * cw_ref[0:1, :] + cur * cw_ref[1:2, :] + xp * cw_ref[2:3, :]
    u = pc[:, W_BR:2 * W_BR] * pc[:, 2 * W_BR:3 * W_BR]
    x0_ref[0] = pc[:, 0:W_BR]
    u_ref[0] = u
    uh_ref[0, 0] = _dot(f_ref[...], u, HI)


def _hymain_kernel(nb, t, g_ref, uh_ref, x0_ref, u_ref, bias_ref, inv_ref, o_ref):
    i = pl.program_id(2)
    acc_re = jnp.zeros((t, LANE), F32)
    acc_im = jnp.zeros((t, LANE), F32)
    for j in range(nb):
        g = g_ref[i - j + nb - 1]
        gr, gi = g[0:t], g[t:2 * t]
        ur, ui = uh_ref[0, j, 0:t, :], uh_ref[0, j, t:2 * t, :]
        acc_re += gr * ur - gi * ui
        acc_im += gr * ui + gi * ur
    y = _dot(inv_ref[...], jnp.concatenate([acc_re, acc_im], axis=0), HI)
    o_ref[0] = x0_ref[0] * (y + bias_ref[...] * u_ref[0])


def hyena_mixer(hy, ghat, conv_w, conv_b, h_bias):
    b_, L, _ = hy.shape
    t = min(HY_BLOCK, L)
    nb = L // t
    fwd, inv = _dft_mats(t)
    prev_spec, next_spec = _halo_specs(t, nb, HY_COLS, 0, lambda s: s)
    x0, u, uh = pl.pallas_call(
        functools.partial(_hypre_kernel, nb),
        grid=(b_, nb),
        in_specs=[pl.BlockSpec((1, t, HY_COLS), lambda b, j: (b, j, 0)), prev_spec, next_spec,
                  pl.BlockSpec((3, HY_COLS), lambda b, j: (0, 0)),
                  pl.BlockSpec((1, HY_COLS), lambda b, j: (0, 0)),
                  pl.BlockSpec((2 * t, t), lambda b, j: (0, 0))],
        out_specs=[pl.BlockSpec((1, t, W_BR), lambda b, j: (b, j, 0)),
                   pl.BlockSpec((1, t, W_BR), lambda b, j: (b, j, 0)),
                   pl.BlockSpec((1, 1, 2 * t, W_BR), lambda b, j: (b, j, 0, 0))],
        out_shape=[jax.ShapeDtypeStruct((b_, L, W_BR), F32), jax.ShapeDtypeStruct((b_, L, W_BR), F32),
                   jax.ShapeDtypeStruct((b_, nb, 2 * t, W_BR), F32)],
        compiler_params=_params(("parallel", "parallel")),
    )(hy, hy, hy, conv_w, conv_b[None], jnp.asarray(fwd[:, :t]))
    ncb = W_BR // LANE
    return pl.pallas_call(
        functools.partial(_hymain_kernel, nb, t),
        grid=(ncb, b_, nb),
        in_specs=[pl.BlockSpec((2 * nb - 1, 2 * t, LANE), lambda c, b, i: (0, 0, c)),
                  pl.BlockSpec((1, nb, 2 * t, LANE), lambda c, b, i: (b, 0, 0, c)),
                  pl.BlockSpec((1, t, LANE), lambda c, b, i: (b, i, c)),
                  pl.BlockSpec((1, t, LANE), lambda c, b, i: (b, i, c)),
                  pl.BlockSpec((1, LANE), lambda c, b, i: (0, c)),
                  pl.BlockSpec((t, 2 * t), lambda c, b, i: (0, 0))],
        out_specs=pl.BlockSpec((1, t, LANE), lambda c, b, i: (b, i, c)),
        out_shape=jax.ShapeDtypeStruct((b_, L, W_BR), F32),
        compiler_params=_params(("parallel", "parallel", "parallel")),
    )(ghat, uh, x0, u, h_bias[None], jnp.asarray(inv))


def _s5_kernel(nc, u_ref, h0_ref, bb_ref, ap_ref, cb_ref, dsk_ref, wg_ref, y_ref, hf_ref, yacc, state):
    s = pl.program_id(1)
    t = u_ref.shape[1]
    n = S5_LANES
    n_steps = t.bit_length() - 1

    @pl.when(s == 0)
    def _():
        hf_ref[...] = jnp.zeros_like(hf_ref)

    def phase(d):
        chunk = s if d == 0 else 2 * nc - 1 - s
        first = (s == 0) if d == 0 else (s == nc)
        last = (s == nc - 1) if d == 0 else (s == 2 * nc - 1)

        @pl.when(first)
        def _():
            state[0:1, :] = h0_ref[0, d:d + 1, :]

        u = u_ref[0]
        x = _dot(u, bb_ref[d], HI)
        row = _iota((t, n), 0)
        st = state[0:1, :]
        a_re, a_im = ap_ref[d, 0:1, 0:n], ap_ref[d, 0:1, n:2 * n]
        c_re = a_re * st[:, 0:n] - a_im * st[:, n:2 * n]
        c_im = a_re * st[:, n:2 * n] + a_im * st[:, 0:n]
        entry = 0 if d == 0 else t - 1
        x_re = x[:, 0:n] + jnp.where(row == entry, c_re, 0.0)
        x_im = x[:, n:2 * n] + jnp.where(row == entry, c_im, 0.0)
        for k in range(n_steps):
            sh = 1 << k
            a_re, a_im = ap_ref[d, k:k + 1, 0:n], ap_ref[d, k:k + 1, n:2 * n]
            if d == 0:
                valid = row >= sh
                s_re = jnp.where(valid, pltpu.roll(x_re, sh, 0), 0.0)
                s_im = jnp.where(valid, pltpu.roll(x_im, sh, 0), 0.0)
            else:
                valid = row < t - sh
                s_re = jnp.where(valid, pltpu.roll(x_re, t - sh, 0), 0.0)
                s_im = jnp.where(valid, pltpu.roll(x_im, t - sh, 0), 0.0)
            x_re, x_im = x_re + a_re * s_re - a_im * s_im, x_im + a_re * s_im + a_im * s_re
        ex = t - 1 if d == 0 else 0
        state[0:1, 0:n] = x_re[ex:ex + 1, :]
        state[0:1, n:2 * n] = x_im[ex:ex + 1, :]
        yd = _dot(x_re, cb_ref[d, 0:n, :], HI) + _dot(x_im, cb_ref[d, n:2 * n, :], HI)
        rows = pl.ds(pl.multiple_of(chunk * t, t), t)
        if d == 0:
            yacc[rows, :] = dsk_ref[...] * u + yd
        else:
            y = yacc[rows, :] + yd
            y = 0.5 * y * (1.0 + jnp.tanh(math.sqrt(2.0 / math.pi) * (y + 0.044715 * y * y * y)))
            y_ref[0] = y * _sigmoid(_bdot(y, wg_ref[...]))

        @pl.when(last)
        def _():
            hf_ref[0, d:d + 1, :] = state[0:1, :]

    @pl.when(s < nc)
    def _():
        phase(0)

    @pl.when(s >= nc)
    def _():
        phase(1)


def _two_pass_chunk(nc):
    return lambda s: jnp.where(s < nc, s, 2 * nc - 1 - s)


def _two_pass_out(nc):
    return lambda s: jnp.where(s < nc, nc - 1, 2 * nc - 1 - s)


def s5_prepare(lam_re, lam_im, log_step, b_re, b_im, c_re, c_im, chunk):
    step = jnp.exp(log_step)[:, :, None]
    mag = jnp.exp(lam_re * step)
    lb_re, lb_im = mag * jnp.cos(lam_im * step), mag * jnp.sin(lam_im * step)
    den = lam_re * lam_re + lam_im * lam_im
    q_re = ((lb_re - 1.0) * lam_re + lb_im * lam_im) / den
    q_im = (lb_im * lam_re - (lb_re - 1.0) * lam_im) / den
    bb_re = q_re[..., None] * b_re - q_im[..., None] * b_im
    bb_im = q_re[..., None] * b_im + q_im[..., None] * b_re
    eye = jnp.eye(S5_GROUPS, dtype=F32)

    def blockdiag_in(m):
        return jnp.einsum('dgni,gh->dgihn', m, eye).reshape(2, W_BR, S5_LANES)

    def blockdiag_out(m):
        return jnp.einsum('dgin,gh->dgnhi', m, eye).reshape(2, S5_LANES, W_BR)

    bblk = jnp.concatenate([blockdiag_in(bb_re), blockdiag_in(bb_im)], axis=-1)
    cblk = jnp.concatenate([blockdiag_out(c_re), -blockdiag_out(c_im)], axis=1)
    pows_re, pows_im = [lb_re], [lb_im]
    for _ in range(chunk.bit_length() - 2):
        r, i = pows_re[-1], pows_im[-1]
        pows_re.append(r * r - i * i)
        pows_im.append(2.0 * r * i)
    flat = lambda ps: jnp.stack([p.reshape(2, S5_LANES) for p in ps], axis=1)
    apow = jnp.concatenate([flat(pows_re), flat(pows_im)], axis=-1)
    return bblk, apow, cblk


def s5_mixer(u, h0, bblk, apow, cblk, d_skip, w_glu):
    b_, L, _ = u.shape
    t = min(S5_CHUNK, L)
    nc = L // t
    chunk_of, out_of = _two_pass_chunk(nc), _two_pass_out(nc)
    const = lambda a: pl.BlockSpec(a.shape, lambda b, s: (0,) * a.ndim)
    dsk = d_skip[None]
    wg = w_glu.astype(BF16)
    return pl.pallas_call(
        functools.partial(_s5_kernel, nc),
        grid=(b_, 2 * nc),
        in_specs=[pl.BlockSpec((1, t, W_BR), lambda b, s: (b, chunk_of(s), 0)),
                  pl.BlockSpec((1, SUBLANE, 2 * S5_LANES), lambda b, s: (b, 0, 0)),
                  const(bblk), const(apow), const(cblk), const(dsk), const(wg)],
        out_specs=[pl.BlockSpec((1, t, W_BR), lambda b, s: (b, out_of(s), 0)),
                   pl.BlockSpec((1, SUBLANE, 2 * S5_LANES), lambda b, s: (b, 0, 0))],
        out_shape=[jax.ShapeDtypeStruct((b_, L, W_BR), F32),
                   jax.ShapeDtypeStruct((b_, SUBLANE, 2 * S5_LANES), F32)],
        scratch_shapes=[pltpu.VMEM((L, W_BR), F32), pltpu.VMEM((SUBLANE, 2 * S5_LANES), F32)],
        compiler_params=_params(("parallel", "arbitrary")),
    )(u, h0, bblk, apow, cblk, dsk, wg)


def _head_mask(h, shape, axis):
    lane = _iota(shape, axis)
    return (lane >= h * RW_HEAD) & (lane < (h + 1) * RW_HEAD)


def _wkv_chunk(d, r, logw, kd, v, kk, a, hst):
    t = r.shape[0]
    ri, ci = _iota((t, t), 0), _iota((t, t), 1)
    incl = (ci <= ri) if d == 0 else (ci >= ri)
    strict = (ci < ri) if d == 0 else (ci > ri)
    eye = (ci == ri).astype(F32)
    cs = _dot(incl.astype(F32), logw, HI)
    tot = jnp.sum(logw, axis=0, keepdims=True)
    g_inv = jnp.exp(-cs)
    qa = kk * jnp.exp(cs - logw)
    rh = r * jnp.exp(cs)
    bt = kk * a * g_inv
    kt = kd * g_inv
    g_tot = jnp.exp(tot)
    uloc = jnp.zeros_like(v)
    w0 = jnp.zeros_like(v)
    yloc = jnp.zeros_like(v)
    yh = jnp.zeros_like(v)
    for h in range(RW_HEADS):
        mh = _head_mask(h, (1, W_BR), 1).astype(F32)
        qa_h, rh_h = qa * mh, rh * mh
        a_ab = jnp.where(strict, _dot_nt(qa_h, bt, HI), 0.0)
        a_ak = jnp.where(strict, _dot_nt(qa_h, kt, HI), 0.0)
        a_rb = jnp.where(incl, _dot_nt(rh_h, bt, HI), 0.0)
        a_rk = jnp.where(incl, _dot_nt(rh_h, kt, HI), 0.0)
        m = eye - a_ab
        p = a_ab
        for _ in range(t.bit_length() - 2):
            p = _dot(p, p, HI)
            m = m + _dot(m, p, HI)
        ul = -_dot(m, _dot(a_ak, v, HI), HI)
        w0h = _dot(m, qa, HI)
        uloc += ul * mh
        w0 += w0h * mh
        yloc += (_dot(a_rk, v, HI) + _dot(a_rb, ul, HI)) * mh
        yh += (rh - _dot(a_rb, w0h, HI)) * mh
    n = W_BR
    ri, ci = _iota((n, n), 0), _iota((n, n), 1)
    bd = (ri // RW_HEAD) == (ci // RW_HEAD)
    bhat, khat = bt * g_tot, kt * g_tot
    p_bd = jnp.where(ri == ci, g_tot, 0.0) - jnp.where(bd, _dot_tn(bhat, w0, HI), 0.0)
    q_bd = jnp.where(bd, _dot_tn(bhat, uloc, HI) + _dot_tn(khat, v, HI), 0.0)
    y = yloc + _dot(yh, hst, HI)
    return y, _dot(p_bd, hst, HI) + q_bd


def _rwkv_kernel(ng, cur_ref, prev_ref, next_ref, h0_ref, mu_ref, w0_ref, w2_ref, a0_ref, a2_ref, g2_ref,
                 kk_ref, ka_ref, rk_ref, lnw_ref, lnb_ref, y_ref, hf_ref, yacc, hst):
    s = pl.program_id(1)
    t = cur_ref.shape[1]
    n = W_BR
    ri, ci = _iota((n, n), 0), _iota((n, n), 1)
    head_sum = ((ri // RW_HEAD) == (ci // RW_HEAD)).astype(F32)

    def phase(d):
        chunk = s if d == 0 else 2 * ng - 1 - s
        first = (s == 0) if d == 0 else (s == ng)
        last = (s == ng - 1) if d == 0 else (s == 2 * ng - 1)

        @pl.when(first)
        def _():
            hst[...] = h0_ref[0, d]

        cur = cur_ref[0]
        prev_row, next_row = _halo_rows(prev_ref, next_ref, chunk, ng)
        xm, xp = _shift_rows(cur, prev_row, next_row)
        p = cur + (0.5 * (xm + xp) - cur) * mu_ref[...]
        r, k, v, lr = p[:, 0:n], p[:, n:2 * n], p[:, 2 * n:3 * n], p[:, 3 * n:4 * n]
        kk = k * kk_ref[...]
        kk = kk * lax.rsqrt(jnp.maximum(_dot(kk * kk, head_sum, HI), 1e-24))
        w = w0_ref[d:d + 1, :] + _dot(jnp.tanh(lr), w2_ref[d], HI)
        logw = -math.exp(-0.5) * _sigmoid(w)
        a = _sigmoid(a0_ref[d:d + 1, :] + _dot(lr, a2_ref[d], HI))
        kd = k * (1.0 + (a - 1.0) * ka_ref[...])
        nsub = t // RW_SUB
        h = hst[...]
        ys = [None] * nsub
        for q in (range(nsub) if d == 0 else range(nsub - 1, -1, -1)):
            sl = slice(q * RW_SUB, (q + 1) * RW_SUB)
            ys[q], h = _wkv_chunk(d, r[sl], logw[sl], kd[sl], v[sl], kk[sl], a[sl], h)
        hst[...] = h
        yd = jnp.concatenate(ys, axis=0)
        rows = pl.ds(pl.multiple_of(chunk * t, t), t)
        if d == 0:
            yacc[rows, :] = yd
        else:
            y = yacc[rows, :] + yd
            mean = _dot(y, head_sum, HI) * (1.0 / RW_HEAD)
            yc = y - mean
            var = _dot(yc * yc, head_sum, HI) * (1.0 / RW_HEAD)
            y = yc * lax.rsqrt(var + RW_GN_EPS) * lnw_ref[...] + lnb_ref[...]
            bonus = _dot(r * k * rk_ref[...], head_sum, HI) * v
            g = _dot(_sigmoid(lr), g2_ref[...], HI)
            y_ref[0] = (y + bonus) * g

        @pl.when(last)
        def _():
            hf_ref[0, d] = hst[...]

    @pl.when(s < ng)
    def _():
        phase(0)

    @pl.when(s >= ng)
    def _():
        phase(1)


def rwkv_prepare(mu, w0, w2, a0, a2, g2, k_k, k_a, r_k, ln_w, ln_b):
    n = W_BR
    mu_p = jnp.zeros((1, RW_PAD), F32).at[0, :RW_COLS].set(mu)
    w2_p = jnp.zeros((2, n, n), F32)
    a2_p = jnp.zeros((2, n, n), F32)
    for d in range(2):
        w2_p = w2_p.at[d, d * RW_W_RANK:(d + 1) * RW_W_RANK].set(w2[d])
        o = 2 * RW_W_RANK + d * RW_A_RANK
        a2_p = a2_p.at[d, o:o + RW_A_RANK].set(a2[d])
    o = 2 * RW_W_RANK + 2 * RW_A_RANK
    g2_p = jnp.zeros((n, n), F32).at[o:o + RW_G_RANK].set(g2)
    return (mu_p, w0, w2_p, a0, a2_p, g2_p, k_k[None], k_a[None], r_k.reshape(1, n), ln_w[None], ln_b[None])


def rwkv_mixer(rw, h0, prm):
    b_, L, _ = rw.shape
    t = min(RW_STEP, L)
    ng = L // t
    chunk_of, out_of = _two_pass_chunk(ng), _two_pass_out(ng)
    prev_spec, next_spec = _halo_specs(t, ng, RW_PAD, 0, chunk_of)
    const = lambda a: pl.BlockSpec(a.shape, lambda b, s: (0,) * a.ndim)
    st_spec = pl.BlockSpec((1, 2, W_BR, W_BR), lambda b, s: (b, 0, 0, 0))
    return pl.pallas_call(
        functools.partial(_rwkv_kernel, ng),
        grid=(b_, 2 * ng),
        in_specs=[pl.BlockSpec((1, t, RW_PAD), lambda b, s: (b, chunk_of(s), 0)), prev_spec, next_spec, st_spec]
                 + [const(a) for a in prm],
        out_specs=[pl.BlockSpec((1, t, W_BR), lambda b, s: (b, out_of(s), 0)), st_spec],
        out_shape=[jax.ShapeDtypeStruct((b_, L, W_BR), F32), jax.ShapeDtypeStruct((b_, 2, W_BR, W_BR), F32)],
        scratch_shapes=[pltpu.VMEM((L, W_BR), F32), pltpu.VMEM((W_BR, W_BR), F32)],
        compiler_params=_params(("parallel", "arbitrary")),
    )(rw, rw, rw, h0, *prm)


def _m2_kernel(nc, z_ref, cur_ref, prev_ref, next_ref, dt_ref, h0_ref, cw_ref, cb_ref, ex_ref, a_ref, dtb_ref,
               dsk_ref, nw_ref, y_ref, hf_ref, yacc, hst):
    s = pl.program_id(1)
    t = cur_ref.shape[1]
    n = W_BR
    ns = M2_STATE
    ri, ci = _iota((t, t), 0), _iota((t, t), 1)

    def phase(d):
        chunk = s if d == 0 else 2 * nc - 1 - s
        first = (s == 0) if d == 0 else (s == nc)
        last = (s == nc - 1) if d == 0 else (s == 2 * nc - 1)

        @pl.when(first)
        def _():
            hst[...] = h0_ref[0, d]

        cur = cur_ref[0]
        prev_row, next_row = _halo_rows(prev_ref, next_ref, chunk, nc)
        xm, xp = _shift_rows(cur, prev_row, next_row)
        xbc = _silu(cb_ref[...] + xm * cw_ref[0:1, :] + cur * cw_ref[1:2, :] + xp * cw_ref[2:3, :])
        xs, bm, cm = xbc[:, 0:n], xbc[:, n:2 * n], xbc[:, 2 * n:3 * n]
        raw = _dot(dt_ref[0], ex_ref[d], HI) + dtb_ref[d:d + 1, :]
        dtd = jnp.maximum(raw, 0.0) + jnp.log(1.0 + jnp.exp(-jnp.abs(raw)))
        da = dtd * a_ref[d:d + 1, :]
        incl = (ci <= ri) if d == 0 else (ci >= ri)
        acs = _dot(incl.astype(F32), da, HI)
        tot = jnp.sum(da, axis=0, keepdims=True)
        xdt = xs * dtd
        sel = (_iota((SUBLANE, n), 1) // M2_HEADDIM == _iota((SUBLANE, n), 0)).astype(F32) * (1.0 / M2_HEADDIM)
        acs_t = _dot_nt(sel, acs, HI)
        ydiag = jnp.zeros((t, n), F32)
        for g in range(M2_GROUPS):
            cb = _dot_nt(cm[:, g * ns:(g + 1) * ns], bm[:, g * ns:(g + 1) * ns], HI)
            for h in range(g * (M2_HEADS // M2_GROUPS), (g + 1) * (M2_HEADS // M2_GROUPS)):
                seg = acs[:, h * M2_HEADDIM:h * M2_HEADDIM + 1] - acs_t[h:h + 1, :]
                scores = cb * jnp.exp(jnp.where(incl, seg, -jnp.inf))
                mh = _head_mask(h, (1, n), 1).astype(F32)
                ydiag += _dot(scores, xdt, HI) * mh
        h_in = hst[...]
        lane = _iota((t, n), 1)
        yoff = jnp.where(lane < n // M2_GROUPS, _dot(cm[:, 0:ns], h_in, HI), _dot(cm[:, ns:2 * ns], h_in, HI))
        yd = ydiag + jnp.exp(acs) * yoff
        xdec = xdt * jnp.exp(tot - acs)
        lane_s = _iota((ns, n), 1)
        new = jnp.where(lane_s < n // M2_GROUPS, _dot_tn(bm[:, 0:ns], xdec, HI), _dot_tn(bm[:, ns:2 * ns], xdec, HI))
        hst[...] = h_in * jnp.exp(tot) + new
        rows = pl.ds(pl.multiple_of(chunk * t, t), t)
        if d == 0:
            yacc[rows, :] = yd + dsk_ref[...] * xs
        else:
            y = (yacc[rows, :] + yd) * _silu(z_ref[0])
            y_ref[0] = _rms(y) * nw_ref[...]

        @pl.when(last)
        def _():
            hf_ref[0, d] = hst[...]

    @pl.when(s < nc)
    def _():
        phase(0)

    @pl.when(s >= nc)
    def _():
        phase(1)


def mamba_prepare(conv_w, conv_b, a_log, dt_bias, d_skip, norm_w):
    n = W_BR
    rep = lambda v: jnp.repeat(v, M2_HEADDIM, axis=-1)
    expand = jnp.zeros((2, LANE, n), F32)
    for d in range(2):
        for h in range(M2_HEADS):
            expand = expand.at[d, d * M2_HEADS + h, h * M2_HEADDIM:(h + 1) * M2_HEADDIM].set(1.0)
    return (conv_w, conv_b[None], expand, rep(-jnp.exp(a_log)), rep(dt_bias), rep(d_skip)[None], norm_w[None])


def mamba_mixer(m2, h0, prm):
    b_, L, _ = m2.shape
    t = min(M2_CHUNK, L)
    nc = L // t
    chunk_of, out_of = _two_pass_chunk(nc), _two_pass_out(nc)
    z, xbc, dt = m2[..., 0:W_BR], m2[..., W_BR:W_BR + M2_XBC], m2[..., W_BR + M2_XBC:]
    prev_spec, next_spec = _halo_specs(t, nc, M2_XBC, 0, chunk_of)
    const = lambda a: pl.BlockSpec(a.shape, lambda b, s: (0,) * a.ndim)
    st_spec = pl.BlockSpec((1, 2, M2_STATE, W_BR), lambda b, s: (b, 0, 0, 0))
    seq = lambda w: pl.BlockSpec((1, t, w), lambda b, s: (b, chunk_of(s), 0))
    return pl.pallas_call(
        functools.partial(_m2_kernel, nc),
        grid=(b_, 2 * nc),
        in_specs=[seq(W_BR), seq(M2_XBC), prev_spec, next_spec, seq(LANE), st_spec] + [const(a) for a in prm],
        out_specs=[pl.BlockSpec((1, t, W_BR), lambda b, s: (b, out_of(s), 0)), st_spec],
        out_shape=[jax.ShapeDtypeStruct((b_, L, W_BR), F32),
                   jax.ShapeDtypeStruct((b_, 2, M2_STATE, W_BR), F32)],
        scratch_shapes=[pltpu.VMEM((L, W_BR), F32), pltpu.VMEM((M2_STATE, W_BR), F32)],
        compiler_params=_params(("parallel", "arbitrary")),
    )(z, xbc, xbc, xbc, dt, h0, *prm)


def _merge_kernel(x_ref, g_ref, sh_ref, sc_ref, gt_ref, y0_ref, y1_ref, y2_ref, y3_ref, wg_ref, wb_ref, wo_ref,
                  o_ref):
    x = x_ref[...]
    h = _modulate(x, g_ref[...], sh_ref[0], sc_ref[0]).astype(BF16)
    acc = jnp.zeros(x.shape, F32)
    for i, y_ref in enumerate((y0_ref, y1_ref, y2_ref, y3_ref)):
        gate = _sigmoid(_dot(h, wg_ref[i]))
        acc += gate * _dot(y_ref[...].astype(BF16), wb_ref[i])
    o_ref[...] = x + gt_ref[0] * _dot(acc.astype(BF16), wo_ref[...])


def merge_branches(xt, g, shift, scale, gate, ys, w_gate, w_branch, w_out, rows_per_mod, tm):
    n = xt.shape[0]
    tiles_per_mod = rows_per_mod // tm
    mod_spec = pl.BlockSpec((1, 1, D_MODEL), lambda i: (i // tiles_per_mod, 0, 0))
    const = lambda a: pl.BlockSpec(a.shape, lambda i: (0,) * a.ndim)
    tok = lambda w: pl.BlockSpec((tm, w), lambda i: (i, 0))
    return pl.pallas_call(
        _merge_kernel,
        grid=(n // tm,),
        in_specs=[tok(D_MODEL), const(g), mod_spec, mod_spec, mod_spec] + [tok(W_BR)] * N_BRANCH
                 + [const(w_gate), const(w_branch), const(w_out)],
        out_specs=tok(D_MODEL),
        out_shape=jax.ShapeDtypeStruct((n, D_MODEL), F32),
        compiler_params=_params(("parallel",)),
    )(xt, g, shift, scale, gate, *ys, w_gate, w_branch, w_out)


def _router_kernel(x_ref, g_ref, sh_ref, sc_ref, wr_ref, comb_ref):
    u = _modulate(x_ref[...], g_ref[...], sh_ref[0], sc_ref[0])
    logits = _dot(u, wr_ref[...], HI)
    lane = _iota(logits.shape, 1).astype(F32)
    neg = -jnp.inf
    is_grp = (lane >= MOE_EXPERTS) & (lane < MOE_EXPERTS + MOE_GROUPS)
    gl = jnp.where(is_grp, logits, neg)
    gmax = jnp.max(gl, axis=-1, keepdims=True)
    grp_p = 1.0 / jnp.sum(jnp.exp(gl - gmax), axis=-1, keepdims=True)
    grp_idx = jnp.min(jnp.where(gl == gmax, lane, 4.0 * LANE), axis=-1, keepdims=True) - MOE_EXPERTS
    in_grp = (lane >= grp_idx * MOE_PER_GROUP) & (lane < (grp_idx + 1) * MOE_PER_GROUP)
    el = jnp.where(in_grp, logits, neg)
    v1 = jnp.max(el, axis=-1, keepdims=True)
    i1 = jnp.min(jnp.where(el == v1, lane, 4.0 * LANE), axis=-1, keepdims=True)
    el2 = jnp.where(lane == i1, neg, el)
    v2 = jnp.max(el2, axis=-1, keepdims=True)
    i2 = jnp.min(jnp.where(el2 == v2, lane, 4.0 * LANE), axis=-1, keepdims=True)
    e2 = jnp.exp(v2 - v1)
    w1 = grp_p / (1.0 + e2)
    comb_ref[...] = jnp.where(lane == i1, w1, jnp.where(lane == i2, w1 * e2, 0.0))


def moe_router(xt, g, shift, scale, w_route, rows_per_mod, tm):
    n = xt.shape[0]
    tiles_per_mod = rows_per_mod // tm
    mod_spec = pl.BlockSpec((1, 1, D_MODEL), lambda i: (i // tiles_per_mod, 0, 0))
    return pl.pallas_call(
        _router_kernel,
        grid=(n // tm,),
        in_specs=[pl.BlockSpec((tm, D_MODEL), lambda i: (i, 0)), pl.BlockSpec((1, D_MODEL), lambda i: (0, 0)),
                  mod_spec, mod_spec, pl.BlockSpec((D_MODEL, LANE), lambda i: (0, 0))],
        out_specs=pl.BlockSpec((tm, LANE), lambda i: (i, 0)),
        out_shape=jax.ShapeDtypeStruct((n, LANE), F32),
        compiler_params=_params(("parallel",)),
    )(xt, g, shift, scale, w_route)


def _moe_kernel(final_norm, x_ref, g_ref, sh_ref, sc_ref, gt_ref, comb_ref, wg_ref, wu_ref, wd_ref, nf_ref, o_ref,
                u_sc, acc):
    e = pl.program_id(1)

    @pl.when(e == 0)
    def _():
        u_sc[...] = _modulate(x_ref[...], g_ref[...], sh_ref[0], sc_ref[0]).astype(BF16)
        acc[...] = jnp.zeros_like(acc)

    u = u_sc[...]
    h = _silu(_dot(u, wg_ref[0])) * _dot(u, wu_ref[0])
    comb = comb_ref[...]
    w = jnp.sum(jnp.where(_iota(comb.shape, 1) == e, comb, 0.0), axis=-1, keepdims=True)
    acc[...] += w * _dot(h.astype(BF16), wd_ref[0])

    @pl.when(e == MOE_EXPERTS - 1)
    def _():
        y = x_ref[...] + gt_ref[0] * acc[...]
        o_ref[...] = _rms(y) * nf_ref[...] if final_norm else y


def moe_experts(xt, g, shift, scale, gate, comb, w_gate, w_up, w_down, norm_final, final_norm, rows_per_mod, tm):
    n = xt.shape[0]
    tiles_per_mod = rows_per_mod // tm
    mod_spec = pl.BlockSpec((1, 1, D_MODEL), lambda i, e: (i // tiles_per_mod, 0, 0))
    return pl.pallas_call(
        functools.partial(_moe_kernel, final_norm),
        grid=(n // tm, MOE_EXPERTS),
        in_specs=[pl.BlockSpec((tm, D_MODEL), lambda i, e: (i, 0)),
                  pl.BlockSpec((1, D_MODEL), lambda i, e: (0, 0)), mod_spec, mod_spec, mod_spec,
                  pl.BlockSpec((tm, LANE), lambda i, e: (i, 0)),
                  pl.BlockSpec((1, D_MODEL, MOE_FF), lambda i, e: (e, 0, 0)),
                  pl.BlockSpec((1, D_MODEL, MOE_FF), lambda i, e: (e, 0, 0)),
                  pl.BlockSpec((1, MOE_FF, D_MODEL), lambda i, e: (e, 0, 0)),
                  pl.BlockSpec((1, D_MODEL), lambda i, e: (0, 0))],
        out_specs=pl.BlockSpec((tm, D_MODEL), lambda i, e: (i, 0)),
        out_shape=jax.ShapeDtypeStruct((n, D_MODEL), F32),
        scratch_shapes=[pltpu.VMEM((tm, D_MODEL), BF16), pltpu.VMEM((tm, D_MODEL), F32)],
        compiler_params=_params(("parallel", "arbitrary")),
    )(xt, g, shift, scale, gate, comb, w_gate, w_up, w_down, norm_final)


def _mix_weights(w_in):
    o = 0
    parts = []
    for cols, padded in ((HY_COLS, HY_COLS), (S5_COLS, S5_COLS), (RW_COLS, RW_PAD), (M2_COLS, M2_PAD)):
        parts.append(jnp.pad(w_in[:, o:o + cols], ((0, 0), (0, padded - cols))))
        o += cols
    w_gate = w_in[:, o:].reshape(D_MODEL, N_BRANCH, D_MODEL).transpose(1, 0, 2)
    return jnp.concatenate(parts, axis=1).astype(BF16), w_gate.astype(BF16)


def kernel(x, c, ctx, c_ctx, mod_w, mod_b, norm_mix, norm_ffn, w_in, hy_conv_w, hy_conv_b, hy_f_w1, hy_f_b1, hy_f_w2, hy_f_b2, hy_f_w3, hy_f_freq, hy_bias, s5_lam_re, s5_lam_im, s5_log_step, s5_b_re, s5_b_im, s5_c_re, s5_c_im, s5_d, s5_w_glu, rw_mu, rw_w0, rw_w2, rw_a0, rw_a2, rw_g2, rw_k_k, rw_k_a, rw_r_k, rw_ln_w, rw_ln_b, m2_conv_w, m2_conv_b, m2_a_log, m2_dt_bias, m2_d, m2_norm_w, w_branch, w_out, moe_w_group, moe_w_expert, moe_w_gate, moe_w_up, moe_w_down, norm_final):
    b_, L, _ = x.shape
    lc = ctx.shape[1]
    depth = mod_w.shape[0]
    tm = 512
    tmc = min(tm, lc)

    cvec = jnp.zeros((SUBLANE, D_MODEL), F32).at[:b_].set(c).at[b_].set(c_ctx)
    mod = adaln_mod(cvec, mod_w, mod_b)

    xt = x.reshape(b_ * L, D_MODEL)
    ct = ctx.reshape(b_ * lc, D_MODEL)
    for l in range(depth):
        ctx_out = l < depth - 1
        mx = mod[l, :b_].reshape(b_, 1, N_MOD, D_MODEL)
        mc = mod[l, b_:b_ + 1].reshape(1, 1, N_MOD, D_MODEL)
        sh1, sc1, g1, sh2, sc2, g2 = (mx[:, :, i] for i in range(N_MOD))
        csh1, csc1, cg1, csh2, csc2, cg2 = (mc[:, :, i] for i in range(N_MOD))
        w_mix, w_gate = _mix_weights(w_in[l])
        nm, nf = norm_mix[l][None], norm_ffn[l][None]

        s5_prm = s5_prepare(s5_lam_re[l], s5_lam_im[l], s5_log_step[l], s5_b_re[l], s5_b_im[l], s5_c_re[l],
                            s5_c_im[l], S5_CHUNK)
        rw_prm = rwkv_prepare(rw_mu[l], rw_w0[l], rw_w2[l], rw_a0[l], rw_a2[l], rw_g2[l], rw_k_k[l], rw_k_a[l],
                              rw_r_k[l], rw_ln_w[l], rw_ln_b[l])
        m2_prm = mamba_prepare(m2_conv_w[l], m2_conv_b[l], m2_a_log[l], m2_dt_bias[l], m2_d[l], m2_norm_w[l])
        hy_f = (hy_f_w1[l], hy_f_b1[l], hy_f_w2[l], hy_f_b2[l], hy_f_w3[l], hy_f_freq[l])

        def mixers(tokens, n_tok, shift, scale, rows_per_mod, tile, states, want_hyena):
            hy, s5, rw, m2 = in_projection(tokens, nm, shift, scale, w_mix, rows_per_mod, tile)
            seq = lambda a: a.reshape(b_, n_tok, a.shape[-1])
            y_hy = None
            if want_hyena:
                t = min(HY_BLOCK, n_tok)
                ghat = hyena_filter_spectra(hyena_filter(n_tok, *hy_f), n_tok, t)
                y_hy = hyena_mixer(seq(hy), ghat, hy_conv_w[l], hy_conv_b[l], hy_bias[l])
            y_s5, s5_h = s5_mixer(seq(s5), states[0], *s5_prm, s5_d[l], s5_w_glu[l])
            y_rw, rw_h = rwkv_mixer(seq(rw), states[1], rw_prm)
            y_m2, m2_h = mamba_mixer(seq(m2), states[2], m2_prm)
            flat = lambda a: None if a is None else a.reshape(b_ * n_tok, W_BR)
            return [flat(y_hy), flat(y_s5), flat(y_rw), flat(y_m2)], (s5_h, rw_h, m2_h)

        zero_states = (jnp.zeros((b_, SUBLANE, 2 * S5_LANES), F32), jnp.zeros((b_, 2, W_BR, W_BR), F32),
                       jnp.zeros((b_, 2, M2_STATE, W_BR), F32))
        ys_c, ctx_states = mixers(ct, lc, csh1, csc1, b_ * lc, tmc, zero_states, ctx_out)
        ys_x, _ = mixers(xt, L, sh1, sc1, L, tm, ctx_states, True)

        w_route = jnp.zeros((D_MODEL, LANE), F32)
        w_route = w_route.at[:, :MOE_EXPERTS].set(moe_w_expert[l].transpose(1, 0, 2).reshape(D_MODEL, MOE_EXPERTS))
        w_route = w_route.at[:, MOE_EXPERTS:MOE_EXPERTS + MOE_GROUPS].set(moe_w_group[l])
        wb, wo = w_branch[l].astype(BF16), w_out[l].astype(BF16)
        wg, wu, wd = moe_w_gate[l].astype(BF16), moe_w_up[l].astype(BF16), moe_w_down[l].astype(BF16)
        nfin = norm_final[None]

        def channel_mix(tokens, ys, mods, rows_per_mod, tile, final):
            s1, c1, gt1, s2, c2, gt2 = mods
            t1 = merge_branches(tokens, nm, s1, c1, gt1, ys, w_gate, wb, wo, rows_per_mod, tile)
            comb = moe_router(t1, nf, s2, c2, w_route, rows_per_mod, tile)
            return moe_experts(t1, nf, s2, c2, gt2, comb, wg, wu, wd, nfin, final, rows_per_mod, tile)

        xt = channel_mix(xt, ys_x, (sh1, sc1, g1, sh2, sc2, g2), L, tm, l == depth - 1)
        if ctx_out:
            ct = channel_mix(ct, ys_c, (csh1, csc1, cg1, csh2, csc2, cg2), b_ * lc, tmc, False)
    return xt.reshape(b_, L, D_MODEL)
```

```python
import functools
import math

import numpy as np
import jax
import jax.numpy as jnp
from jax import lax
from jax.experimental import pallas as pl
from jax.experimental.pallas import tpu as pltpu

F32 = jnp.float32
BF16 = jnp.bfloat16
HI = lax.Precision.HIGHEST

D_MODEL = 1024
W_BR = 256
N_BRANCH = 4
N_MOD = 6
NORM_EPS = 1e-6
GRID_W = 64

HY_BANDS = 8
HY_HID = 64
HY_TARGET = 1e-2
HY_FAST_PCT = 0.3
HY_SLOW_PCT = 1.5
HY_BLOCK = 512

S5_GROUP = 16
S5_GROUPS = 16
S5_STATE = 64
S5_LANES = S5_GROUPS * S5_STATE
S5_CHUNK = 256

RW_HEAD = 64
RW_HEADS = 4
RW_W_RANK = 32
RW_A_RANK = 32
RW_G_RANK = 64
RW_GN_EPS = 64e-5
RW_SUB = 64
RW_STEP = 256

M2_HEADDIM = 64
M2_HEADS = 4
M2_GROUPS = 2
M2_STATE = 128
M2_CHUNK = 256

MOE_GROUPS = 4
MOE_PER_GROUP = 4
MOE_EXPERTS = 16
MOE_FF = 512

HY_COLS = 3 * W_BR
S5_COLS = W_BR
RW_COLS = 3 * W_BR + 2 * RW_W_RANK + 2 * RW_A_RANK + RW_G_RANK
M2_XBC = W_BR + 2 * M2_GROUPS * M2_STATE
M2_COLS = W_BR + M2_XBC + 2 * M2_HEADS
RW_PAD = 1024
M2_PAD = 1152
LANE = 128
SUBLANE = 8

VMEM_LIMIT = 56 * 1024 * 1024


def _dot(a, b, prec=None):
    return jnp.dot(a, b, preferred_element_type=F32, precision=prec)


def _dot_nt(a, b, prec=None):
    return lax.dot_general(a, b, (((1,), (1,)), ((), ())), preferred_element_type=F32, precision=prec)


def _dot_tn(a, b, prec=None):
    return lax.dot_general(a, b, (((0,), (0,)), ((), ())), preferred_element_type=F32, precision=prec)


def _bdot(a, b):
    return jnp.dot(a.astype(BF16), b.astype(BF16), preferred_element_type=F32)


def _sigmoid(x):
    return 1.0 / (1.0 + jnp.exp(-x))


def _silu(x):
    return x * _sigmoid(x)


def _params(sem):
    return pltpu.CompilerParams(dimension_semantics=sem, vmem_limit_bytes=VMEM_LIMIT)


def _rms(x):
    return x * lax.rsqrt(jnp.mean(x * x, axis=-1, keepdims=True) + NORM_EPS)


def _modulate(x, g, shift, scale):
    return _rms(x) * g * (1.0 + scale) + shift


def _iota(shape, axis):
    return lax.broadcasted_iota(jnp.int32, shape, axis)


def _shift_rows(cur, prev_row, next_row):
    t = cur.shape[0]
    row = _iota(cur.shape, 0)
    xm = jnp.where(row == 0, prev_row, pltpu.roll(cur, 1, 0))
    xp = jnp.where(row == t - 1, next_row, pltpu.roll(cur, t - 1, 0))
    return xm, xp


def _halo_specs(t, n_blocks, width, col_block, chunk_of):
    per = t // SUBLANE
    last = n_blocks * per - 1

    def prev_map(b, s):
        return (b, jnp.maximum(chunk_of(s) * per - 1, 0), col_block)

    def next_map(b, s):
        return (b, jnp.minimum((chunk_of(s) + 1) * per, last), col_block)

    return (pl.BlockSpec((1, SUBLANE, width), prev_map), pl.BlockSpec((1, SUBLANE, width), next_map))


def _halo_rows(prev_ref, next_ref, chunk, n_chunks):
    prev_row = jnp.where(chunk == 0, 0.0, prev_ref[0, SUBLANE - 1:SUBLANE, :])
    next_row = jnp.where(chunk == n_chunks - 1, 0.0, next_ref[0, 0:1, :])
    return prev_row, next_row


def _mod_kernel(c_ref, w_ref, b_ref, o_ref):
    o_ref[0] = _dot(_silu(c_ref[...]), w_ref[0], HI) + b_ref[0]


def adaln_mod(cvec, mod_w, mod_b):
    depth = mod_w.shape[0]
    return pl.pallas_call(
        _mod_kernel,
        name="adaln_mod",
        grid=(depth, N_MOD),
        in_specs=[pl.BlockSpec((SUBLANE, D_MODEL), lambda l, j: (0, 0)),
                  pl.BlockSpec((1, D_MODEL, D_MODEL), lambda l, j: (l, 0, j)),
                  pl.BlockSpec((1, 1, D_MODEL), lambda l, j: (l, 0, j))],
        out_specs=pl.BlockSpec((1, SUBLANE, D_MODEL), lambda l, j: (l, 0, j)),
        out_shape=jax.ShapeDtypeStruct((depth, SUBLANE, N_MOD * D_MODEL), F32),
        compiler_params=_params(("parallel", "parallel")),
    )(cvec, mod_w, mod_b.reshape(depth, 1, N_MOD * D_MODEL))


def _inproj_kernel(x_ref, g_ref, sh_ref, sc_ref, w_ref, *out_refs):
    h = _modulate(x_ref[...], g_ref[...], sh_ref[0], sc_ref[0]).astype(BF16)
    o = 0
    for ref in out_refs:
        n = ref.shape[-1]
        ref[...] = _dot(h, w_ref[:, o:o + n])
        o += n


def in_projection(xt, g, shift, scale, w_mix, rows_per_mod, tm):
    n = xt.shape[0]
    tiles_per_mod = rows_per_mod // tm
    widths = (HY_COLS, S5_COLS, RW_PAD, W_BR, M2_XBC, M2_PAD - W_BR - M2_XBC)
    mod_spec = pl.BlockSpec((1, 1, D_MODEL), lambda i: (i // tiles_per_mod, 0, 0))
    return pl.pallas_call(
        _inproj_kernel,
        name="in_projection",
        grid=(n // tm,),
        in_specs=[pl.BlockSpec((tm, D_MODEL), lambda i: (i, 0)),
                  pl.BlockSpec((1, D_MODEL), lambda i: (0, 0)),
                  mod_spec, mod_spec,
                  pl.BlockSpec(w_mix.shape, lambda i: (0, 0))],
        out_specs=[pl.BlockSpec((tm, w), lambda i: (i, 0)) for w in widths],
        out_shape=[jax.ShapeDtypeStruct((n, w), F32) for w in widths],
        compiler_params=_params(("parallel",)),
    )(xt, g, shift, scale, w_mix)


def _dft_mats(t):
    k = np.arange(t, dtype=np.float64)[:, None]
    n = np.arange(2 * t, dtype=np.float64)[None, :]
    ang = np.pi * (2.0 * k + 1.0) * n / (2.0 * t)
    fwd = np.concatenate([np.cos(ang), -np.sin(ang)], axis=0)
    inv = fwd[:, :t].T / t
    return fwd.astype(np.float32), inv.astype(np.float32)


def _hyfilt_kernel(w1_ref, b1_ref, w2_ref, b2_ref, w3_ref, fr_ref, o_ref):
    L = o_ref.shape[0]
    shape = (L, LANE)
    pos = _iota(shape, 0).astype(F32)
    lane = _iota(shape, 1)
    t = pos / (L - 1)
    w = (2.0 * math.pi / L) * pos
    band = jnp.where(lane >= 1 + HY_BANDS, lane - 1 - HY_BANDS, lane - 1).astype(F32)
    f = 1e-4 + band * ((HY_BANDS - 1 - 1e-4) / (HY_BANDS - 1))
    arg = f * w
    feats = jnp.where(lane == 0, t,
                      jnp.where(lane <= HY_BANDS, jnp.cos(arg),
                                jnp.where(lane <= 2 * HY_BANDS, -jnp.sin(arg), 0.0)))
    fr = fr_ref[...]
    h = jnp.sin(fr * (_dot(feats, w1_ref[...], HI) + b1_ref[...]))
    h = jnp.sin(fr * (_dot(h, w2_ref[...], HI) + b2_ref[...]))
    h = _dot(h, w3_ref[...], HI)
    ch = _iota((L, W_BR), 1).astype(F32)
    lo = math.log(HY_TARGET) / HY_SLOW_PCT
    hi = math.log(HY_TARGET) / HY_FAST_PCT
    deltas = jnp.abs(lo + ch * ((hi - lo) / (W_BR - 1)))
    decay = jnp.exp(-(t[:, 0:1]) * deltas)
    o_ref[:, 0:W_BR] = h[:, 0:W_BR] * decay
    o_ref[:, W_BR:2 * W_BR] = h[:, W_BR:2 * W_BR] * decay


def hyena_filter(L, f_w1, f_b1, f_w2, f_b2, f_w3, f_freq):
    pad = lambda a, r, c: jnp.zeros((r, c), F32).at[:a.shape[0], :a.shape[1]].set(a)
    args = (pad(f_w1, LANE, LANE), pad(f_b1[None], 1, LANE), pad(f_w2, LANE, LANE), pad(f_b2[None], 1, LANE),
            pad(f_w3, LANE, 2 * W_BR), pad(f_freq[None], 1, LANE))
    return pl.pallas_call(
        _hyfilt_kernel,
        name="hyena_filter",
        out_shape=jax.ShapeDtypeStruct((L, 2 * W_BR), F32),
        compiler_params=_params(None),
    )(*args)


def _ghat_kernel(f_ref, g_ref, o_ref):
    o_ref[0] = _dot(f_ref[...], g_ref[0], HI)


def hyena_filter_spectra(filt, L, t):
    nb = L // t
    h_f, h_b = filt[:, :W_BR], filt[:, W_BR:]
    kext = jnp.concatenate([jnp.zeros((1, W_BR), F32), h_b[:0:-1], h_f], axis=0)
    segs = []
    for d in range(-(nb - 1), nb):
        c = L + d * t
        segs.append(jnp.concatenate([kext[c:c + t], jnp.zeros((1, W_BR), F32), -kext[c - t + 1:c]], axis=0))
    g = jnp.stack(segs)
    fwd, _ = _dft_mats(t)
    return pl.pallas_call(
        _ghat_kernel,
        name="hyena_filter_spectra",
        grid=(2 * nb - 1,),
        in_specs=[pl.BlockSpec((2 * t, 2 * t), lambda d: (0, 0)),
                  pl.BlockSpec((1, 2 * t, W_BR), lambda d: (d, 0, 0))],
        out_specs=pl.BlockSpec((1, 2 * t, W_BR), lambda d: (d, 0, 0)),
        out_shape=jax.ShapeDtypeStruct((2 * nb - 1, 2 * t, W_BR), F32),
        compiler_params=_params(("parallel",)),
    )(jnp.asarray(fwd), g)


def _hypre_kernel(nb, cur_ref, prev_ref, next_ref, cw_ref, cb_ref, f_ref, x0_ref, u_ref, uh_ref):
    j = pl.program_id(1)
    cur = cur_ref[0]
    prev_row, next_row = _halo_rows(prev_ref, next_ref, j, nb)
    xm, xp = _shift_rows(cur, prev_row, next_row)
    pc = cb_ref[...] + xm * cw_ref[0:1, :] + cur * cw_ref[1:2, :] + xp * cw_ref[2:3, :]
    u = pc[:, W_BR:2 * W_BR] * pc[:, 2 * W_BR:3 * W_BR]
    x0_ref[0] = pc[:, 0:W_BR]
    u_ref[0] = u
    uh_ref[0, 0] = _dot(f_ref[...], u, HI)


def _hymain_kernel(nb, t, g_ref, uh_ref, x0_ref, u_ref, bias_ref, inv_ref, o_ref):
    i = pl.program_id(2)
    acc_re = jnp.zeros((t, LANE), F32)
    acc_im = jnp.zeros((t, LANE), F32)
    for j in range(nb):
        g = g_ref[i - j + nb - 1]
        gr, gi = g[0:t], g[t:2 * t]
        ur, ui = uh_ref[0, j, 0:t, :], uh_ref[0, j, t:2 * t, :]
        acc_re += gr * ur - gi * ui
        acc_im += gr * ui + gi * ur
    y = _dot(inv_ref[...], jnp.concatenate([acc_re, acc_im], axis=0), HI)
    o_ref[0] = x0_ref[0] * (y + bias_ref[...] * u_ref[0])


def hyena_mixer(hy, ghat, conv_w, conv_b, h_bias):
    b_, L, _ = hy.shape
    t = min(HY_BLOCK, L)
    nb = L // t
    fwd, inv = _dft_mats(t)
    prev_spec, next_spec = _halo_specs(t, nb, HY_COLS, 0, lambda s: s)
    x0, u, uh = pl.pallas_call(
        functools.partial(_hypre_kernel, nb),
        name="hyena_conv_dft",
        grid=(b_, nb),
        in_specs=[pl.BlockSpec((1, t, HY_COLS), lambda b, j: (b, j, 0)), prev_spec, next_spec,
                  pl.BlockSpec((3, HY_COLS), lambda b, j: (0, 0)),
                  pl.BlockSpec((1, HY_COLS), lambda b, j: (0, 0)),
                  pl.BlockSpec((2 * t, t), lambda b, j: (0, 0))],
        out_specs=[pl.BlockSpec((1, t, W_BR), lambda b, j: (b, j, 0)),
                   pl.BlockSpec((1, t, W_BR), lambda b, j: (b, j, 0)),
                   pl.BlockSpec((1, 1, 2 * t, W_BR), lambda b, j: (b, j, 0, 0))],
        out_shape=[jax.ShapeDtypeStruct((b_, L, W_BR), F32), jax.ShapeDtypeStruct((b_, L, W_BR), F32),
                   jax.ShapeDtypeStruct((b_, nb, 2 * t, W_BR), F32)],
        compiler_params=_params(("parallel", "parallel")),
    )(hy, hy, hy, conv_w, conv_b[None], jnp.asarray(fwd[:, :t]))
    ncb = W_BR // LANE
    return pl.pallas_call(
        functools.partial(_hymain_kernel, nb, t),
        name="hyena_longconv",
        grid=(ncb, b_, nb),
        in_specs=[pl.BlockSpec((2 * nb - 1, 2 * t, LANE), lambda c, b, i: (0, 0, c)),
                  pl.BlockSpec((1, nb, 2 * t, LANE), lambda c, b, i: (b, 0, 0, c)),
                  pl.BlockSpec((1, t, LANE), lambda c, b, i: (b, i, c)),
                  pl.BlockSpec((1, t, LANE), lambda c, b, i: (b, i, c)),
                  pl.BlockSpec((1, LANE), lambda c, b, i: (0, c)),
                  pl.BlockSpec((t, 2 * t), lambda c, b, i: (0, 0))],
        out_specs=pl.BlockSpec((1, t, LANE), lambda c, b, i: (b, i, c)),
        out_shape=jax.ShapeDtypeStruct((b_, L, W_BR), F32),
        compiler_params=_params(("parallel", "parallel", "parallel")),
    )(ghat, uh, x0, u, h_bias[None], jnp.asarray(inv))


def _s5_kernel(nc, u_ref, h0_ref, bb_ref, ap_ref, cb_ref, dsk_ref, wg_ref, y_ref, hf_ref, yacc, state):
    s = pl.program_id(1)
    t = u_ref.shape[1]
    n = S5_LANES
    n_steps = t.bit_length() - 1

    @pl.when(s == 0)
    def _():
        hf_ref[...] = jnp.zeros_like(hf_ref)

    def phase(d):
        chunk = s if d == 0 else 2 * nc - 1 - s
        first = (s == 0) if d == 0 else (s == nc)
        last = (s == nc - 1) if d == 0 else (s == 2 * nc - 1)

        @pl.when(first)
        def _():
            state[0:1, :] = h0_ref[0, d:d + 1, :]

        u = u_ref[0]
        x = _dot(u.astype(BF16), bb_ref[d])
        row = _iota((t, n), 0)
        st = state[0:1, :]
        a_re, a_im = ap_ref[d, 0:1, 0:n], ap_ref[d, 0:1, n:2 * n]
        c_re = a_re * st[:, 0:n] - a_im * st[:, n:2 * n]
        c_im = a_re * st[:, n:2 * n] + a_im * st[:, 0:n]
        entry = 0 if d == 0 else t - 1
        x_re = x[:, 0:n] + jnp.where(row == entry, c_re, 0.0)
        x_im = x[:, n:2 * n] + jnp.where(row == entry, c_im, 0.0)
        for k in range(n_steps):
            sh = 1 << k
            a_re, a_im = ap_ref[d, k:k + 1, 0:n], ap_ref[d, k:k + 1, n:2 * n]
            if d == 0:
                valid = row >= sh
                s_re = jnp.where(valid, pltpu.roll(x_re, sh, 0), 0.0)
                s_im = jnp.where(valid, pltpu.roll(x_im, sh, 0), 0.0)
            else:
                valid = row < t - sh
                s_re = jnp.where(valid, pltpu.roll(x_re, t - sh, 0), 0.0)
                s_im = jnp.where(valid, pltpu.roll(x_im, t - sh, 0), 0.0)
            x_re, x_im = x_re + a_re * s_re - a_im * s_im, x_im + a_re * s_im + a_im * s_re
        ex = t - 1 if d == 0 else 0
        state[0:1, 0:n] = x_re[ex:ex + 1, :]
        state[0:1, n:2 * n] = x_im[ex:ex + 1, :]
        yd = _dot(x_re.astype(BF16), cb_ref[d, 0:n, :]) + _dot(x_im.astype(BF16), cb_ref[d, n:2 * n, :])
        rows = pl.ds(pl.multiple_of(chunk * t, t), t)
        if d == 0:
            yacc[rows, :] = dsk_ref[...] * u + yd
        else:
            y = yacc[rows, :] + yd
            y = 0.5 * y * (1.0 + jnp.tanh(math.sqrt(2.0 / math.pi) * (y + 0.044715 * y * y * y)))
            y_ref[0] = y * _sigmoid(_bdot(y, wg_ref[...]))

        @pl.when(last)
        def _():
            hf_ref[0, d:d + 1, :] = state[0:1, :]

    @pl.when(s < nc)
    def _():
        phase(0)

    @pl.when(s >= nc)
    def _():
        phase(1)


def _two_pass_chunk(nc):
    return lambda s: jnp.where(s < nc, s, 2 * nc - 1 - s)


def _two_pass_out(nc):
    return lambda s: jnp.where(s < nc, nc - 1, 2 * nc - 1 - s)


def s5_prepare(lam_re, lam_im, log_step, b_re, b_im, c_re, c_im, chunk):
    step = jnp.exp(log_step)[:, :, None]
    mag = jnp.exp(lam_re * step)
    lb_re, lb_im = mag * jnp.cos(lam_im * step), mag * jnp.sin(lam_im * step)
    den = lam_re * lam_re + lam_im * lam_im
    q_re = ((lb_re - 1.0) * lam_re + lb_im * lam_im) / den
    q_im = (lb_im * lam_re - (lb_re - 1.0) * lam_im) / den
    bb_re = q_re[..., None] * b_re - q_im[..., None] * b_im
    bb_im = q_re[..., None] * b_im + q_im[..., None] * b_re
    eye = jnp.eye(S5_GROUPS, dtype=F32)

    def blockdiag_in(m):
        return jnp.einsum('dgni,gh->dgihn', m, eye).reshape(2, W_BR, S5_LANES)

    def blockdiag_out(m):
        return jnp.einsum('dgin,gh->dgnhi', m, eye).reshape(2, S5_LANES, W_BR)

    bblk = jnp.concatenate([blockdiag_in(bb_re), blockdiag_in(bb_im)], axis=-1)
    cblk = jnp.concatenate([blockdiag_out(c_re), -blockdiag_out(c_im)], axis=1)
    pows_re, pows_im = [lb_re], [lb_im]
    for _ in range(chunk.bit_length() - 2):
        r, i = pows_re[-1], pows_im[-1]
        pows_re.append(r * r - i * i)
        pows_im.append(2.0 * r * i)
    flat = lambda ps: jnp.stack([p.reshape(2, S5_LANES) for p in ps], axis=1)
    apow = jnp.concatenate([flat(pows_re), flat(pows_im)], axis=-1)
    return bblk.astype(BF16), apow, cblk.astype(BF16)


def s5_mixer(u, h0, bblk, apow, cblk, d_skip, w_glu):
    b_, L, _ = u.shape
    t = min(S5_CHUNK, L)
    nc = L // t
    chunk_of, out_of = _two_pass_chunk(nc), _two_pass_out(nc)
    const = lambda a: pl.BlockSpec(a.shape, lambda b, s: (0,) * a.ndim)
    dsk = d_skip[None]
    wg = w_glu.astype(BF16)
    return pl.pallas_call(
        functools.partial(_s5_kernel, nc),
        name="s5_mixer",
        grid=(b_, 2 * nc),
        in_specs=[pl.BlockSpec((1, t, W_BR), lambda b, s: (b, chunk_of(s), 0)),
                  pl.BlockSpec((1, SUBLANE, 2 * S5_LANES), lambda b, s: (b, 0, 0)),
                  const(bblk), const(apow), const(cblk), const(dsk), const(wg)],
        out_specs=[pl.BlockSpec((1, t, W_BR), lambda b, s: (b, out_of(s), 0)),
                   pl.BlockSpec((1, SUBLANE, 2 * S5_LANES), lambda b, s: (b, 0, 0))],
        out_shape=[jax.ShapeDtypeStruct((b_, L, W_BR), F32),
                   jax.ShapeDtypeStruct((b_, SUBLANE, 2 * S5_LANES), F32)],
        scratch_shapes=[pltpu.VMEM((L, W_BR), F32), pltpu.VMEM((SUBLANE, 2 * S5_LANES), F32)],
        compiler_params=_params(("parallel", "arbitrary")),
    )(u, h0, bblk, apow, cblk, dsk, wg)


def _head_mask(h, shape, axis):
    lane = _iota(shape, axis)
    return (lane >= h * RW_HEAD) & (lane < (h + 1) * RW_HEAD)


def _split3(x):
    hi = x.astype(BF16)
    r1 = x - hi.astype(F32)
    mid = r1.astype(BF16)
    return hi, mid, (r1 - mid.astype(F32)).astype(BF16)


def _stack_heads(x):
    xb = x.astype(BF16)
    zero = jnp.zeros_like(xb)
    return jnp.concatenate([jnp.where(_head_mask(h, xb.shape, 1), xb, zero) for h in range(RW_HEADS)], axis=0)


def _unstack_heads(x, t):
    return x[0:t] + x[t:2 * t] + x[2 * t:3 * t] + x[3 * t:4 * t]


def _split2(x):
    hi = x.astype(BF16)
    return hi, (x - hi.astype(F32)).astype(BF16)


def _dot_exact_rhs(a, b):
    hi, lo = _split2(a)
    return _dot(hi, b) + _dot(lo, b)


def _dot_split(a, b_hi, b_lo):
    hi, lo = _split2(a)
    return _dot(hi, b_hi) + _dot(hi, b_lo) + _dot(lo, b_hi)


RW_INV_BASE = 8


def _tri_inverse(nmat, t):
    n = nmat.shape[0]
    blk_r, blk_c = _iota((n, n), 0), _iota((n, n), 1)
    same = lambda s: (blk_r // s) == (blk_c // s)
    base = jnp.where(same(RW_INV_BASE), nmat, 0.0)
    m = (blk_r == blk_c).astype(F32) - base
    p = base
    for _ in range(RW_INV_BASE.bit_length() - 2):
        p = _bdot(p, p)
        m = m + _bdot(m, p)
    s = 2 * RW_INV_BASE
    while s <= t:
        c = jnp.where(same(s) & jnp.logical_not(same(s // 2)), nmat, 0.0)
        m = m - _bdot(_bdot(m, c), m)
        s *= 2
    return m


def _wkv_chunk(d, r, logw, kd, v, kk, a, hst):
    t = r.shape[0]
    n = RW_HEADS * t
    bf = lambda x: x.astype(BF16)
    ri, ci = _iota((t, t), 0), _iota((t, t), 1)
    tri = ((ci <= ri) if d == 0 else (ci >= ri)).astype(BF16)
    cs = sum(_dot(tri, part) for part in _split3(logw))
    tot = jnp.sum(logw, axis=0, keepdims=True)
    g_inv = jnp.exp(-cs)
    g_tot = jnp.exp(tot)
    bt, kt = kk * a * g_inv, kd * g_inv
    qas, rhs = _stack_heads(kk * jnp.exp(cs - logw)), _stack_heads(r * jnp.exp(cs))
    bts, kts, vs = _stack_heads(bt), _stack_heads(kt), _stack_heads(v)
    bhs, khs = _stack_heads(bt * g_tot), _stack_heads(kt * g_tot)
    rr, cc = _iota((n, n), 0) % t, _iota((n, n), 1) % t
    incl = (cc <= rr) if d == 0 else (cc >= rr)
    strict = (cc < rr) if d == 0 else (cc > rr)
    a_ab = jnp.where(strict, _dot_nt(qas, bts), 0.0)
    a_ak = jnp.where(strict, _dot_nt(qas, kts), 0.0)
    a_rb = jnp.where(incl, _dot_nt(rhs, bts), 0.0)
    a_rk = jnp.where(incl, _dot_nt(rhs, kts), 0.0)
    m = bf(_tri_inverse(a_ab, t))
    a_rb = bf(a_rb)
    ul = bf(-_dot(m, bf(_dot(bf(a_ak), vs))))
    w0 = bf(_dot(m, qas))
    yloc = _unstack_heads(_dot(bf(a_rk), vs) + _dot(a_rb, ul), t)
    yh = _unstack_heads(rhs.astype(F32) - _dot(a_rb, w0), t)
    dn = W_BR
    p_bd = jnp.where(_iota((dn, dn), 0) == _iota((dn, dn), 1), g_tot, 0.0) - _dot_tn(bhs, w0)
    q_bd = _dot_tn(bhs, ul) + _dot_tn(khs, vs)
    hb = bf(hst)
    return yloc + _dot(bf(yh), hb), _dot(bf(p_bd), hb) + q_bd


def _rwkv_kernel(ng, cur_ref, prev_ref, next_ref, h0_ref, mu_ref, w0_ref, a0_ref, lrh_ref, lrl_ref,
                 kk_ref, ka_ref, rk_ref, lnw_ref, lnb_ref, y_ref, hf_ref, yacc, hst):
    s = pl.program_id(1)
    t = cur_ref.shape[1]
    n = W_BR
    ri, ci = _iota((n, n), 0), _iota((n, n), 1)
    head_sum = ((ri // RW_HEAD) == (ci // RW_HEAD)).astype(BF16)

    def phase(d):
        chunk = s if d == 0 else 2 * ng - 1 - s
        first = (s == 0) if d == 0 else (s == ng)
        last = (s == ng - 1) if d == 0 else (s == 2 * ng - 1)

        @pl.when(first)
        def _():
            hst[...] = h0_ref[0, d]

        cur = cur_ref[0]
        prev_row, next_row = _halo_rows(prev_ref, next_ref, chunk, ng)
        xm, xp = _shift_rows(cur, prev_row, next_row)
        p = cur + (0.5 * (xm + xp) - cur) * mu_ref[...]
        r, k, v, lr = p[:, 0:n], p[:, n:2 * n], p[:, 2 * n:3 * n], p[:, 3 * n:4 * n]
        kk = k * kk_ref[...]
        kk = kk * lax.rsqrt(jnp.maximum(_dot_exact_rhs(kk * kk, head_sum), 1e-24))
        lane = _iota(lr.shape, 1)
        feats = jnp.where(lane < 2 * RW_W_RANK, jnp.tanh(lr),
                          jnp.where(lane < 2 * (RW_W_RANK + RW_A_RANK), lr, _sigmoid(lr)))
        cols = (2 if d == 0 else 3) * n
        low = _dot_split(feats, lrh_ref[d, :, 0:cols], lrl_ref[d, :, 0:cols])
        w = w0_ref[d:d + 1, :] + low[:, 0:n]
        logw = -math.exp(-0.5) * _sigmoid(w)
        a = _sigmoid(a0_ref[d:d + 1, :] + low[:, n:2 * n])
        kd = k * (1.0 + (a - 1.0) * ka_ref[...])
        nsub = t // RW_SUB
        h = hst[...]
        ys = [None] * nsub
        for q in (range(nsub) if d == 0 else range(nsub - 1, -1, -1)):
            sl = slice(q * RW_SUB, (q + 1) * RW_SUB)
            ys[q], h = _wkv_chunk(d, r[sl], logw[sl], kd[sl], v[sl], kk[sl], a[sl], h)
        hst[...] = h
        yd = jnp.concatenate(ys, axis=0)
        rows = pl.ds(pl.multiple_of(chunk * t, t), t)
        if d == 0:
            yacc[rows, :] = yd
        else:
            y = yacc[rows, :] + yd
            mean = _dot_exact_rhs(y, head_sum) * (1.0 / RW_HEAD)
            yc = y - mean
            var = _dot_exact_rhs(yc * yc, head_sum) * (1.0 / RW_HEAD)
            y = yc * lax.rsqrt(var + RW_GN_EPS) * lnw_ref[...] + lnb_ref[...]
            bonus = _dot_exact_rhs(r * k * rk_ref[...], head_sum) * v
            y_ref[0] = (y + bonus) * low[:, 2 * n:3 * n]

        @pl.when(last)
        def _():
            hf_ref[0, d] = hst[...]

    @pl.when(s < ng)
    def _():
        phase(0)

    @pl.when(s >= ng)
    def _():
        phase(1)


def rwkv_prepare(mu, w0, w2, a0, a2, g2, k_k, k_a, r_k, ln_w, ln_b):
    n = W_BR
    mu_p = jnp.zeros((1, RW_PAD), F32).at[0, :RW_COLS].set(mu)
    w2_p = jnp.zeros((2, n, n), F32)
    a2_p = jnp.zeros((2, n, n), F32)
    for d in range(2):
        w2_p = w2_p.at[d, d * RW_W_RANK:(d + 1) * RW_W_RANK].set(w2[d])
        o = 2 * RW_W_RANK + d * RW_A_RANK
        a2_p = a2_p.at[d, o:o + RW_A_RANK].set(a2[d])
    o = 2 * RW_W_RANK + 2 * RW_A_RANK
    g2_p = jnp.zeros((n, n), F32).at[o:o + RW_G_RANK].set(g2)
    low = jnp.concatenate([w2_p, a2_p, jnp.broadcast_to(g2_p, (2, n, n))], axis=-1)
    low_hi = low.astype(BF16)
    low_lo = (low - low_hi.astype(F32)).astype(BF16)
    return (mu_p, w0, a0, low_hi, low_lo, k_k[None], k_a[None], r_k.reshape(1, n), ln_w[None], ln_b[None])


def rwkv_mixer(rw, h0, prm):
    b_, L, _ = rw.shape
    t = min(RW_STEP, L)
    ng = L // t
    chunk_of, out_of = _two_pass_chunk(ng), _two_pass_out(ng)
    prev_spec, next_spec = _halo_specs(t, ng, RW_PAD, 0, chunk_of)
    const = lambda a: pl.BlockSpec(a.shape, lambda b, s: (0,) * a.ndim)
    st_spec = pl.BlockSpec((1, 2, W_BR, W_BR), lambda b, s: (b, 0, 0, 0))
    return pl.pallas_call(
        functools.partial(_rwkv_kernel, ng),
        name="rwkv_mixer",
        grid=(b_, 2 * ng),
        in_specs=[pl.BlockSpec((1, t, RW_PAD), lambda b, s: (b, chunk_of(s), 0)), prev_spec, next_spec, st_spec]
                 + [const(a) for a in prm],
        out_specs=[pl.BlockSpec((1, t, W_BR), lambda b, s: (b, out_of(s), 0)), st_spec],
        out_shape=[jax.ShapeDtypeStruct((b_, L, W_BR), F32), jax.ShapeDtypeStruct((b_, 2, W_BR, W_BR), F32)],
        scratch_shapes=[pltpu.VMEM((L, W_BR), F32), pltpu.VMEM((W_BR, W_BR), F32)],
        compiler_params=_params(("parallel", "arbitrary")),
    )(rw, rw, rw, h0, *prm)


def _m2_kernel(nc, z_ref, cur_ref, prev_ref, next_ref, dt_ref, h0_ref, cw_ref, cb_ref, ex_ref, a_ref, dtb_ref,
               dsk_ref, nw_ref, y_ref, hf_ref, yacc, hst):
    s = pl.program_id(1)
    t = cur_ref.shape[1]
    n = W_BR
    ns = M2_STATE
    ri, ci = _iota((t, t), 0), _iota((t, t), 1)

    def phase(d):
        chunk = s if d == 0 else 2 * nc - 1 - s
        first = (s == 0) if d == 0 else (s == nc)
        last = (s == nc - 1) if d == 0 else (s == 2 * nc - 1)

        @pl.when(first)
        def _():
            hst[...] = h0_ref[0, d]

        cur = cur_ref[0]
        prev_row, next_row = _halo_rows(prev_ref, next_ref, chunk, nc)
        xm, xp = _shift_rows(cur, prev_row, next_row)
        xbc = _silu(cb_ref[...] + xm * cw_ref[0:1, :] + cur * cw_ref[1:2, :] + xp * cw_ref[2:3, :])
        xs, bm, cm = xbc[:, 0:n], xbc[:, n:2 * n], xbc[:, 2 * n:3 * n]
        raw = sum(_dot(part, ex_ref[d]) for part in _split3(dt_ref[0])) + dtb_ref[d:d + 1, :]
        dtd = jnp.maximum(raw, 0.0) + jnp.log(1.0 + jnp.exp(-jnp.abs(raw)))
        da = dtd * a_ref[d:d + 1, :]
        incl = (ci <= ri) if d == 0 else (ci >= ri)
        tri = incl.astype(BF16)
        da_parts = _split3(da)
        acs = sum(_dot(tri, part) for part in da_parts)
        tot = jnp.sum(da, axis=0, keepdims=True)
        xdt = xs * dtd
        xdt_b = xdt.astype(BF16)
        bm_b, cm_b = bm.astype(BF16), cm.astype(BF16)
        sel = jnp.where(_iota((SUBLANE, n), 1) // M2_HEADDIM == _iota((SUBLANE, n), 0), 1.0 / M2_HEADDIM,
                        0.0).astype(BF16)
        acs_t = sum(_dot_nt(sel, part) for part in _split3(acs))
        ydiag = jnp.zeros((t, n), F32)
        for g in range(M2_GROUPS):
            cb = _dot_nt(cm_b[:, g * ns:(g + 1) * ns], bm_b[:, g * ns:(g + 1) * ns])
            for h in range(g * (M2_HEADS // M2_GROUPS), (g + 1) * (M2_HEADS // M2_GROUPS)):
                seg = acs[:, h * M2_HEADDIM:h * M2_HEADDIM + 1] - acs_t[h:h + 1, :]
                scores = cb * jnp.exp(jnp.where(incl, seg, -jnp.inf))
                ydiag = jnp.where(_head_mask(h, (t, n), 1), _dot(scores.astype(BF16), xdt_b), ydiag)
        h_in = hst[...]
        h_b = h_in.astype(BF16)
        lane = _iota((t, n), 1)
        yoff = jnp.where(lane < n // M2_GROUPS, _dot(cm_b[:, 0:ns], h_b), _dot(cm_b[:, ns:2 * ns], h_b))
        yd = ydiag + jnp.exp(acs) * yoff
        xdec = (xdt * jnp.exp(tot - acs)).astype(BF16)
        lane_s = _iota((ns, n), 1)
        new = jnp.where(lane_s < n // M2_GROUPS, _dot_tn(bm_b[:, 0:ns], xdec), _dot_tn(bm_b[:, ns:2 * ns], xdec))
        hst[...] = h_in * jnp.exp(tot) + new
        rows = pl.ds(pl.multiple_of(chunk * t, t), t)
        if d == 0:
            yacc[rows, :] = yd + dsk_ref[...] * xs
        else:
            y = (yacc[rows, :] + yd) * _silu(z_ref[0])
            y_ref[0] = _rms(y) * nw_ref[...]

        @pl.when(last)
        def _():
            hf_ref[0, d] = hst[...]

    @pl.when(s < nc)
    def _():
        phase(0)

    @pl.when(s >= nc)
    def _():
        phase(1)


def mamba_prepare(conv_w, conv_b, a_log, dt_bias, d_skip, norm_w):
    n = W_BR
    rep = lambda v: jnp.repeat(v, M2_HEADDIM, axis=-1)
    expand = jnp.zeros((2, LANE, n), F32)
    for d in range(2):
        for h in range(M2_HEADS):
            expand = expand.at[d, d * M2_HEADS + h, h * M2_HEADDIM:(h + 1) * M2_HEADDIM].set(1.0)
    return (conv_w, conv_b[None], expand.astype(BF16), rep(-jnp.exp(a_log)), rep(dt_bias), rep(d_skip)[None],
            norm_w[None])


def mamba_mixer(z, xbc, dt, h0, prm):
    b_, L, _ = z.shape
    t = min(M2_CHUNK, L)
    nc = L // t
    chunk_of, out_of = _two_pass_chunk(nc), _two_pass_out(nc)
    prev_spec, next_spec = _halo_specs(t, nc, M2_XBC, 0, chunk_of)
    const = lambda a: pl.BlockSpec(a.shape, lambda b, s: (0,) * a.ndim)
    st_spec = pl.BlockSpec((1, 2, M2_STATE, W_BR), lambda b, s: (b, 0, 0, 0))
    seq = lambda w: pl.BlockSpec((1, t, w), lambda b, s: (b, chunk_of(s), 0))
    return pl.pallas_call(
        functools.partial(_m2_kernel, nc),
        name="mamba_mixer",
        grid=(b_, 2 * nc),
        in_specs=[seq(W_BR), seq(M2_XBC), prev_spec, next_spec, seq(LANE), st_spec] + [const(a) for a in prm],
        out_specs=[pl.BlockSpec((1, t, W_BR), lambda b, s: (b, out_of(s), 0)), st_spec],
        out_shape=[jax.ShapeDtypeStruct((b_, L, W_BR), F32),
                   jax.ShapeDtypeStruct((b_, 2, M2_STATE, W_BR), F32)],
        scratch_shapes=[pltpu.VMEM((L, W_BR), F32), pltpu.VMEM((M2_STATE, W_BR), F32)],
        compiler_params=_params(("parallel", "arbitrary")),
    )(z, xbc, xbc, xbc, dt, h0, *prm)


def _merge_kernel(x_ref, g_ref, sh_ref, sc_ref, gt_ref, y0_ref, y1_ref, y2_ref, y3_ref, wg_ref, wb_ref, wo_ref,
                  o_ref):
    x = x_ref[...]
    h = _modulate(x, g_ref[...], sh_ref[0], sc_ref[0]).astype(BF16)
    acc = jnp.zeros(x.shape, F32)
    for i, y_ref in enumerate((y0_ref, y1_ref, y2_ref, y3_ref)):
        gate = _sigmoid(_dot(h, wg_ref[i]))
        acc += gate * _dot(y_ref[...].astype(BF16), wb_ref[i])
    o_ref[...] = x + gt_ref[0] * _dot(acc.astype(BF16), wo_ref[...])


def merge_branches(xt, g, shift, scale, gate, ys, w_gate, w_branch, w_out, rows_per_mod, tm):
    n = xt.shape[0]
    tiles_per_mod = rows_per_mod // tm
    mod_spec = pl.BlockSpec((1, 1, D_MODEL), lambda i: (i // tiles_per_mod, 0, 0))
    const = lambda a: pl.BlockSpec(a.shape, lambda i: (0,) * a.ndim)
    tok = lambda w: pl.BlockSpec((tm, w), lambda i: (i, 0))
    return pl.pallas_call(
        _merge_kernel,
        name="merge_branches",
        grid=(n // tm,),
        in_specs=[tok(D_MODEL), const(g), mod_spec, mod_spec, mod_spec] + [tok(W_BR)] * N_BRANCH
                 + [const(w_gate), const(w_branch), const(w_out)],
        out_specs=tok(D_MODEL),
        out_shape=jax.ShapeDtypeStruct((n, D_MODEL), F32),
        compiler_params=_params(("parallel",)),
    )(xt, g, shift, scale, gate, *ys, w_gate, w_branch, w_out)


def _router_kernel(x_ref, g_ref, sh_ref, sc_ref, wr_ref, comb_ref):
    u = _modulate(x_ref[...], g_ref[...], sh_ref[0], sc_ref[0])
    logits = _dot(u, wr_ref[...], HI)
    lane = _iota(logits.shape, 1).astype(F32)
    neg = -jnp.inf
    is_grp = (lane >= MOE_EXPERTS) & (lane < MOE_EXPERTS + MOE_GROUPS)
    gl = jnp.where(is_grp, logits, neg)
    gmax = jnp.max(gl, axis=-1, keepdims=True)
    grp_p = 1.0 / jnp.sum(jnp.exp(gl - gmax), axis=-1, keepdims=True)
    grp_idx = jnp.min(jnp.where(gl == gmax, lane, 4.0 * LANE), axis=-1, keepdims=True) - MOE_EXPERTS
    in_grp = (lane >= grp_idx * MOE_PER_GROUP) & (lane < (grp_idx + 1) * MOE_PER_GROUP)
    el = jnp.where(in_grp, logits, neg)
    v1 = jnp.max(el, axis=-1, keepdims=True)
    i1 = jnp.min(jnp.where(el == v1, lane, 4.0 * LANE), axis=-1, keepdims=True)
    el2 = jnp.where(lane == i1, neg, el)
    v2 = jnp.max(el2, axis=-1, keepdims=True)
    i2 = jnp.min(jnp.where(el2 == v2, lane, 4.0 * LANE), axis=-1, keepdims=True)
    e2 = jnp.exp(v2 - v1)
    w1 = grp_p / (1.0 + e2)
    comb_ref[...] = jnp.where(lane == i1, w1, jnp.where(lane == i2, w1 * e2, 0.0))


def moe_router(xt, g, shift, scale, w_route, rows_per_mod, tm):
    n = xt.shape[0]
    tiles_per_mod = rows_per_mod // tm
    mod_spec = pl.BlockSpec((1, 1, D_MODEL), lambda i: (i // tiles_per_mod, 0, 0))
    return pl.pallas_call(
        _router_kernel,
        name="moe_router",
        grid=(n // tm,),
        in_specs=[pl.BlockSpec((tm, D_MODEL), lambda i: (i, 0)), pl.BlockSpec((1, D_MODEL), lambda i: (0, 0)),
                  mod_spec, mod_spec, pl.BlockSpec((D_MODEL, LANE), lambda i: (0, 0))],
        out_specs=pl.BlockSpec((tm, LANE), lambda i: (i, 0)),
        out_shape=jax.ShapeDtypeStruct((n, LANE), F32),
        compiler_params=_params(("parallel",)),
    )(xt, g, shift, scale, w_route)


def _moe_kernel(final_norm, x_ref, g_ref, sh_ref, sc_ref, gt_ref, comb_ref, wg_ref, wu_ref, wd_ref, nf_ref, o_ref,
                u_sc, acc):
    e = pl.program_id(1)

    @pl.when(e == 0)
    def _():
        u_sc[...] = _modulate(x_ref[...], g_ref[...], sh_ref[0], sc_ref[0]).astype(BF16)
        acc[...] = jnp.zeros_like(acc)

    u = u_sc[...]
    h = _silu(_dot(u, wg_ref[0])) * _dot(u, wu_ref[0])
    comb = comb_ref[...]
    w = jnp.sum(jnp.where(_iota(comb.shape, 1) == e, comb, 0.0), axis=-1, keepdims=True)
    acc[...] += w * _dot(h.astype(BF16), wd_ref[0])

    @pl.when(e == MOE_EXPERTS - 1)
    def _():
        y = x_ref[...] + gt_ref[0] * acc[...]
        o_ref[...] = _rms(y) * nf_ref[...] if final_norm else y


def moe_experts(xt, g, shift, scale, gate, comb, w_gate, w_up, w_down, norm_final, final_norm, rows_per_mod, tm):
    n = xt.shape[0]
    tiles_per_mod = rows_per_mod // tm
    mod_spec = pl.BlockSpec((1, 1, D_MODEL), lambda i, e: (i // tiles_per_mod, 0, 0))
    return pl.pallas_call(
        functools.partial(_moe_kernel, final_norm),
        name="moe_experts",
        grid=(n // tm, MOE_EXPERTS),
        in_specs=[pl.BlockSpec((tm, D_MODEL), lambda i, e: (i, 0)),
                  pl.BlockSpec((1, D_MODEL), lambda i, e: (0, 0)), mod_spec, mod_spec, mod_spec,
                  pl.BlockSpec((tm, LANE), lambda i, e: (i, 0)),
                  pl.BlockSpec((1, D_MODEL, MOE_FF), lambda i, e: (e, 0, 0)),
                  pl.BlockSpec((1, D_MODEL, MOE_FF), lambda i, e: (e, 0, 0)),
                  pl.BlockSpec((1, MOE_FF, D_MODEL), lambda i, e: (e, 0, 0)),
                  pl.BlockSpec((1, D_MODEL), lambda i, e: (0, 0))],
        out_specs=pl.BlockSpec((tm, D_MODEL), lambda i, e: (i, 0)),
        out_shape=jax.ShapeDtypeStruct((n, D_MODEL), F32),
        scratch_shapes=[pltpu.VMEM((tm, D_MODEL), BF16), pltpu.VMEM((tm, D_MODEL), F32)],
        compiler_params=_params(("parallel", "arbitrary")),
    )(xt, g, shift, scale, gate, comb, w_gate, w_up, w_down, norm_final)


def _mix_weights(w_in):
    o = 0
    parts = []
    for cols, padded in ((HY_COLS, HY_COLS), (S5_COLS, S5_COLS), (RW_COLS, RW_PAD), (M2_COLS, M2_PAD)):
        parts.append(jnp.pad(w_in[:, o:o + cols], ((0, 0), (0, padded - cols))))
        o += cols
    w_gate = w_in[:, o:].reshape(D_MODEL, N_BRANCH, D_MODEL).transpose(1, 0, 2)
    return jnp.concatenate(parts, axis=1).astype(BF16), w_gate.astype(BF16)


def kernel(x, c, ctx, c_ctx, mod_w, mod_b, norm_mix, norm_ffn, w_in, hy_conv_w, hy_conv_b, hy_f_w1, hy_f_b1, hy_f_w2, hy_f_b2, hy_f_w3, hy_f_freq, hy_bias, s5_lam_re, s5_lam_im, s5_log_step, s5_b_re, s5_b_im, s5_c_re, s5_c_im, s5_d, s5_w_glu, rw_mu, rw_w0, rw_w2, rw_a0, rw_a2, rw_g2, rw_k_k, rw_k_a, rw_r_k, rw_ln_w, rw_ln_b, m2_conv_w, m2_conv_b, m2_a_log, m2_dt_bias, m2_d, m2_norm_w, w_branch, w_out, moe_w_group, moe_w_expert, moe_w_gate, moe_w_up, moe_w_down, norm_final):
    b_, L, _ = x.shape
    lc = ctx.shape[1]
    depth = mod_w.shape[0]
    tm = 512
    tmc = min(tm, lc)

    cvec = jnp.zeros((SUBLANE, D_MODEL), F32).at[:b_].set(c).at[b_].set(c_ctx)
    mod = adaln_mod(cvec, mod_w, mod_b)

    xt = x.reshape(b_ * L, D_MODEL)
    ct = ctx.reshape(b_ * lc, D_MODEL)
    for l in range(depth):
        ctx_out = l < depth - 1
        mx = mod[l, :b_].reshape(b_, 1, N_MOD, D_MODEL)
        mc = mod[l, b_:b_ + 1].reshape(1, 1, N_MOD, D_MODEL)
        sh1, sc1, g1, sh2, sc2, g2 = (mx[:, :, i] for i in range(N_MOD))
        csh1, csc1, cg1, csh2, csc2, cg2 = (mc[:, :, i] for i in range(N_MOD))
        w_mix, w_gate = _mix_weights(w_in[l])
        nm, nf = norm_mix[l][None], norm_ffn[l][None]

        s5_prm = s5_prepare(s5_lam_re[l], s5_lam_im[l], s5_log_step[l], s5_b_re[l], s5_b_im[l], s5_c_re[l],
                            s5_c_im[l], S5_CHUNK)
        rw_prm = rwkv_prepare(rw_mu[l], rw_w0[l], rw_w2[l], rw_a0[l], rw_a2[l], rw_g2[l], rw_k_k[l], rw_k_a[l],
                              rw_r_k[l], rw_ln_w[l], rw_ln_b[l])
        m2_prm = mamba_prepare(m2_conv_w[l], m2_conv_b[l], m2_a_log[l], m2_dt_bias[l], m2_d[l], m2_norm_w[l])
        hy_f = (hy_f_w1[l], hy_f_b1[l], hy_f_w2[l], hy_f_b2[l], hy_f_w3[l], hy_f_freq[l])

        def mixers(tokens, n_tok, shift, scale, rows_per_mod, tile, states, want_hyena):
            hy, s5, rw, m2z, m2x, m2dt = in_projection(tokens, nm, shift, scale, w_mix, rows_per_mod, tile)
            seq = lambda a: a.reshape(b_, n_tok, a.shape[-1])
            y_hy = None
            if want_hyena:
                t = min(HY_BLOCK, n_tok)
                ghat = hyena_filter_spectra(hyena_filter(n_tok, *hy_f), n_tok, t)
                y_hy = hyena_mixer(seq(hy), ghat, hy_conv_w[l], hy_conv_b[l], hy_bias[l])
            y_s5, s5_h = s5_mixer(seq(s5), states[0], *s5_prm, s5_d[l], s5_w_glu[l])
            y_rw, rw_h = rwkv_mixer(seq(rw), states[1], rw_prm)
            y_m2, m2_h = mamba_mixer(seq(m2z), seq(m2x), seq(m2dt), states[2], m2_prm)
            flat = lambda a: None if a is None else a.reshape(b_ * n_tok, W_BR)
            return [flat(y_hy), flat(y_s5), flat(y_rw), flat(y_m2)], (s5_h, rw_h, m2_h)

        zero_states = (jnp.zeros((b_, SUBLANE, 2 * S5_LANES), F32), jnp.zeros((b_, 2, W_BR, W_BR), F32),
                       jnp.zeros((b_, 2, M2_STATE, W_BR), F32))
        ys_c, ctx_states = mixers(ct, lc, csh1, csc1, b_ * lc, tmc, zero_states, ctx_out)
        ys_x, _ = mixers(xt, L, sh1, sc1, L, tm, ctx_states, True)

        w_route = jnp.zeros((D_MODEL, LANE), F32)
        w_route = w_route.at[:, :MOE_EXPERTS].set(moe_w_expert[l].transpose(1, 0, 2).reshape(D_MODEL, MOE_EXPERTS))
        w_route = w_route.at[:, MOE_EXPERTS:MOE_EXPERTS + MOE_GROUPS].set(moe_w_group[l])
        wb, wo = w_branch[l].astype(BF16), w_out[l].astype(BF16)
        wg, wu, wd = moe_w_gate[l].astype(BF16), moe_w_up[l].astype(BF16), moe_w_down[l].astype(BF16)
        nfin = norm_final[None]

        def channel_mix(tokens, ys, mods, rows_per_mod, tile, final):
            s1, c1, gt1, s2, c2, gt2 = mods
            t1 = merge_branches(tokens, nm, s1, c1, gt1, ys, w_gate, wb, wo, rows_per_mod, tile)
            comb = moe_router(t1, nf, s2, c2, w_route, rows_per_mod, tile)
            return moe_experts(t1, nf, s2, c2, gt2, comb, wg, wu, wd, nfin, final, rows_per_mod, tile)

        xt = channel_mix(xt, ys_x, (sh1, sc1, g1, sh2, sc2, g2), L, tm, l == depth - 1)
        if ctx_out:
            ct = channel_mix(ct, ys_c, (csh1, csc1, cg1, csh2, csc2, cg2), b_ * lc, tmc, False)
    return xt.reshape(b_, L, D_MODEL)
```

```python
import functools
import math

import numpy as np
import jax
import jax.numpy as jnp
from jax import lax
from jax.experimental import pallas as pl
from jax.experimental.pallas import tpu as pltpu

F32 = jnp.float32
BF16 = jnp.bfloat16
HI = lax.Precision.HIGHEST

D_MODEL = 1024
W_BR = 256
N_BRANCH = 4
N_MOD = 6
NORM_EPS = 1e-6
GRID_W = 64

HY_BANDS = 8
HY_HID = 64
HY_TARGET = 1e-2
HY_FAST_PCT = 0.3
HY_SLOW_PCT = 1.5
HY_BLOCK = 512

S5_GROUP = 16
S5_GROUPS = 16
S5_STATE = 64
S5_LANES = S5_GROUPS * S5_STATE
S5_CHUNK = 256

RW_HEAD = 64
RW_HEADS = 4
RW_W_RANK = 32
RW_A_RANK = 32
RW_G_RANK = 64
RW_GN_EPS = 64e-5
RW_SUB = 64
RW_STEP = 256

M2_HEADDIM = 64
M2_HEADS = 4
M2_GROUPS = 2
M2_STATE = 128
M2_CHUNK = 256

MOE_GROUPS = 4
MOE_PER_GROUP = 4
MOE_EXPERTS = 16
MOE_FF = 512
MOE_GROUP_LANE = MOE_EXPERTS
MOE_TILE = 1024
MOE_SUB = 128

HY_COLS = 3 * W_BR
S5_COLS = W_BR
RW_COLS = 3 * W_BR + 2 * RW_W_RANK + 2 * RW_A_RANK + RW_G_RANK
M2_XBC = W_BR + 2 * M2_GROUPS * M2_STATE
M2_COLS = W_BR + M2_XBC + 2 * M2_HEADS
RW_PAD = 1024
M2_PAD = 1152
LANE = 128
SUBLANE = 8

VMEM_LIMIT = 56 * 1024 * 1024


def _dot(a, b, prec=None):
    return jnp.dot(a, b, preferred_element_type=F32, precision=prec)


def _dot_nt(a, b, prec=None):
    return lax.dot_general(a, b, (((1,), (1,)), ((), ())), preferred_element_type=F32, precision=prec)


def _dot_tn(a, b, prec=None):
    return lax.dot_general(a, b, (((0,), (0,)), ((), ())), preferred_element_type=F32, precision=prec)


def _bdot(a, b):
    return jnp.dot(a.astype(BF16), b.astype(BF16), preferred_element_type=F32)


def _sigmoid(x):
    return 1.0 / (1.0 + jnp.exp(-x))


def _silu(x):
    return x * _sigmoid(x)


def _params(sem):
    return pltpu.CompilerParams(dimension_semantics=sem, vmem_limit_bytes=VMEM_LIMIT)


def _rms(x):
    return x * lax.rsqrt(jnp.mean(x * x, axis=-1, keepdims=True) + NORM_EPS)


def _modulate(x, g, shift, scale):
    return _rms(x) * g * (1.0 + scale) + shift


def _iota(shape, axis):
    return lax.broadcasted_iota(jnp.int32, shape, axis)


def _shift_rows(cur, prev_row, next_row):
    t = cur.shape[0]
    row = _iota(cur.shape, 0)
    xm = jnp.where(row == 0, prev_row, pltpu.roll(cur, 1, 0))
    xp = jnp.where(row == t - 1, next_row, pltpu.roll(cur, t - 1, 0))
    return xm, xp


def _halo_specs(t, n_blocks, width, col_block, chunk_of):
    per = t // SUBLANE
    last = n_blocks * per - 1

    def prev_map(b, s):
        return (b, jnp.maximum(chunk_of(s) * per - 1, 0), col_block)

    def next_map(b, s):
        return (b, jnp.minimum((chunk_of(s) + 1) * per, last), col_block)

    return (pl.BlockSpec((1, SUBLANE, width), prev_map), pl.BlockSpec((1, SUBLANE, width), next_map))


def _halo_rows(prev_ref, next_ref, chunk, n_chunks):
    prev_row = jnp.where(chunk == 0, 0.0, prev_ref[0, SUBLANE - 1:SUBLANE, :])
    next_row = jnp.where(chunk == n_chunks - 1, 0.0, next_ref[0, 0:1, :])
    return prev_row, next_row


def _mod_kernel(c_ref, w_ref, b_ref, o_ref):
    o_ref[0] = _dot(_silu(c_ref[...]), w_ref[0], HI) + b_ref[0]


def adaln_mod(cvec, mod_w, mod_b):
    depth = mod_w.shape[0]
    return pl.pallas_call(
        _mod_kernel,
        name="adaln_mod",
        grid=(depth, N_MOD),
        in_specs=[pl.BlockSpec((SUBLANE, D_MODEL), lambda l, j: (0, 0)),
                  pl.BlockSpec((1, D_MODEL, D_MODEL), lambda l, j: (l, 0, j)),
                  pl.BlockSpec((1, 1, D_MODEL), lambda l, j: (l, 0, j))],
        out_specs=pl.BlockSpec((1, SUBLANE, D_MODEL), lambda l, j: (l, 0, j)),
        out_shape=jax.ShapeDtypeStruct((depth, SUBLANE, N_MOD * D_MODEL), F32),
        compiler_params=_params(("parallel", "parallel")),
    )(cvec, mod_w, mod_b.reshape(depth, 1, N_MOD * D_MODEL))


def _inproj_kernel(x_ref, g_ref, sh_ref, sc_ref, w_ref, *out_refs):
    h = _modulate(x_ref[...], g_ref[...], sh_ref[0], sc_ref[0]).astype(BF16)
    o = 0
    for ref in out_refs:
        n = ref.shape[-1]
        ref[...] = _dot(h, w_ref[:, o:o + n])
        o += n


def in_projection(xt, g, shift, scale, w_mix, rows_per_mod, tm):
    n = xt.shape[0]
    tiles_per_mod = rows_per_mod // tm
    widths = (HY_COLS, S5_COLS, RW_PAD, W_BR, M2_XBC, M2_PAD - W_BR - M2_XBC)
    mod_spec = pl.BlockSpec((1, 1, D_MODEL), lambda i: (i // tiles_per_mod, 0, 0))
    return pl.pallas_call(
        _inproj_kernel,
        name="in_projection",
        grid=(n // tm,),
        in_specs=[pl.BlockSpec((tm, D_MODEL), lambda i: (i, 0)),
                  pl.BlockSpec((1, D_MODEL), lambda i: (0, 0)),
                  mod_spec, mod_spec,
                  pl.BlockSpec(w_mix.shape, lambda i: (0, 0))],
        out_specs=[pl.BlockSpec((tm, w), lambda i: (i, 0)) for w in widths],
        out_shape=[jax.ShapeDtypeStruct((n, w), F32) for w in widths],
        compiler_params=_params(("parallel",)),
    )(xt, g, shift, scale, w_mix)


def _dft_mats(t):
    k = np.arange(t, dtype=np.float64)[:, None]
    n = np.arange(2 * t, dtype=np.float64)[None, :]
    ang = np.pi * (2.0 * k + 1.0) * n / (2.0 * t)
    fwd = np.concatenate([np.cos(ang), -np.sin(ang)], axis=0)
    inv = fwd[:, :t].T / t
    return fwd.astype(np.float32), inv.astype(np.float32)


def _hyfilt_kernel(w1_ref, b1_ref, w2_ref, b2_ref, w3_ref, fr_ref, o_ref):
    L = o_ref.shape[0]
    shape = (L, LANE)
    pos = _iota(shape, 0).astype(F32)
    lane = _iota(shape, 1)
    t = pos / (L - 1)
    w = (2.0 * math.pi / L) * pos
    band = jnp.where(lane >= 1 + HY_BANDS, lane - 1 - HY_BANDS, lane - 1).astype(F32)
    f = 1e-4 + band * ((HY_BANDS - 1 - 1e-4) / (HY_BANDS - 1))
    arg = f * w
    feats = jnp.where(lane == 0, t,
                      jnp.where(lane <= HY_BANDS, jnp.cos(arg),
                                jnp.where(lane <= 2 * HY_BANDS, -jnp.sin(arg), 0.0)))
    fr = fr_ref[...]
    h = jnp.sin(fr * (_dot(feats, w1_ref[...], HI) + b1_ref[...]))
    h = jnp.sin(fr * (_dot(h, w2_ref[...], HI) + b2_ref[...]))
    h = _dot(h, w3_ref[...], HI)
    ch = _iota((L, W_BR), 1).astype(F32)
    lo = math.log(HY_TARGET) / HY_SLOW_PCT
    hi = math.log(HY_TARGET) / HY_FAST_PCT
    deltas = jnp.abs(lo + ch * ((hi - lo) / (W_BR - 1)))
    decay = jnp.exp(-(t[:, 0:1]) * deltas)
    o_ref[:, 0:W_BR] = h[:, 0:W_BR] * decay
    o_ref[:, W_BR:2 * W_BR] = h[:, W_BR:2 * W_BR] * decay


def hyena_filter(L, f_w1, f_b1, f_w2, f_b2, f_w3, f_freq):
    pad = lambda a, r, c: jnp.zeros((r, c), F32).at[:a.shape[0], :a.shape[1]].set(a)
    args = (pad(f_w1, LANE, LANE), pad(f_b1[None], 1, LANE), pad(f_w2, LANE, LANE), pad(f_b2[None], 1, LANE),
            pad(f_w3, LANE, 2 * W_BR), pad(f_freq[None], 1, LANE))
    return pl.pallas_call(
        _hyfilt_kernel,
        name="hyena_filter",
        out_shape=jax.ShapeDtypeStruct((L, 2 * W_BR), F32),
        compiler_params=_params(None),
    )(*args)


def _ghat_kernel(f_ref, g_ref, o_ref):
    o_ref[0] = _dot(f_ref[...], g_ref[0], HI)


def hyena_filter_spectra(filt, L, t):
    nb = L // t
    h_f, h_b = filt[:, :W_BR], filt[:, W_BR:]
    kext = jnp.concatenate([jnp.zeros((1, W_BR), F32), h_b[:0:-1], h_f], axis=0)
    segs = []
    for d in range(-(nb - 1), nb):
        c = L + d * t
        segs.append(jnp.concatenate([kext[c:c + t], jnp.zeros((1, W_BR), F32), -kext[c - t + 1:c]], axis=0))
    g = jnp.stack(segs)
    fwd, _ = _dft_mats(t)
    return pl.pallas_call(
        _ghat_kernel,
        name="hyena_filter_spectra",
        grid=(2 * nb - 1,),
        in_specs=[pl.BlockSpec((2 * t, 2 * t), lambda d: (0, 0)),
                  pl.BlockSpec((1, 2 * t, W_BR), lambda d: (d, 0, 0))],
        out_specs=pl.BlockSpec((1, 2 * t, W_BR), lambda d: (d, 0, 0)),
        out_shape=jax.ShapeDtypeStruct((2 * nb - 1, 2 * t, W_BR), F32),
        compiler_params=_params(("parallel",)),
    )(jnp.asarray(fwd), g)


def _hypre_kernel(nb, cur_ref, prev_ref, next_ref, cw_ref, cb_ref, f_ref, x0_ref, u_ref, uh_ref):
    j = pl.program_id(1)
    cur = cur_ref[0]
    prev_row, next_row = _halo_rows(prev_ref, next_ref, j, nb)
    xm, xp = _shift_rows(cur, prev_row, next_row)
    pc = cb_ref[...] + xm * cw_ref[0:1, :] + cur * cw_ref[1:2, :] + xp * cw_ref[2:3, :]
    u = pc[:, W_BR:2 * W_BR] * pc[:, 2 * W_BR:3 * W_BR]
    x0_ref[0] = pc[:, 0:W_BR]
    u_ref[0] = u
    uh_ref[0, 0] = _dot(f_ref[...], u, HI)


def _hymain_kernel(nb, t, g_ref, uh_ref, x0_ref, u_ref, bias_ref, inv_ref, o_ref):
    i = pl.program_id(2)
    acc_re = jnp.zeros((t, LANE), F32)
    acc_im = jnp.zeros((t, LANE), F32)
    for j in range(nb):
        g = g_ref[i - j + nb - 1]
        gr, gi = g[0:t], g[t:2 * t]
        ur, ui = uh_ref[0, j, 0:t, :], uh_ref[0, j, t:2 * t, :]
        acc_re += gr * ur - gi * ui
        acc_im += gr * ui + gi * ur
    y = _dot(inv_ref[...], jnp.concatenate([acc_re, acc_im], axis=0), HI)
    o_ref[0] = x0_ref[0] * (y + bias_ref[...] * u_ref[0])


def hyena_mixer(hy, ghat, conv_w, conv_b, h_bias):
    b_, L, _ = hy.shape
    t = min(HY_BLOCK, L)
    nb = L // t
    fwd, inv = _dft_mats(t)
    prev_spec, next_spec = _halo_specs(t, nb, HY_COLS, 0, lambda s: s)
    x0, u, uh = pl.pallas_call(
        functools.partial(_hypre_kernel, nb),
        name="hyena_conv_dft",
        grid=(b_, nb),
        in_specs=[pl.BlockSpec((1, t, HY_COLS), lambda b, j: (b, j, 0)), prev_spec, next_spec,
                  pl.BlockSpec((3, HY_COLS), lambda b, j: (0, 0)),
                  pl.BlockSpec((1, HY_COLS), lambda b, j: (0, 0)),
                  pl.BlockSpec((2 * t, t), lambda b, j: (0, 0))],
        out_specs=[pl.BlockSpec((1, t, W_BR), lambda b, j: (b, j, 0)),
                   pl.BlockSpec((1, t, W_BR), lambda b, j: (b, j, 0)),
                   pl.BlockSpec((1, 1, 2 * t, W_BR), lambda b, j: (b, j, 0, 0))],
        out_shape=[jax.ShapeDtypeStruct((b_, L, W_BR), F32), jax.ShapeDtypeStruct((b_, L, W_BR), F32),
                   jax.ShapeDtypeStruct((b_, nb, 2 * t, W_BR), F32)],
        compiler_params=_params(("parallel", "parallel")),
    )(hy, hy, hy, conv_w, conv_b[None], jnp.asarray(fwd[:, :t]))
    ncb = W_BR // LANE
    return pl.pallas_call(
        functools.partial(_hymain_kernel, nb, t),
        name="hyena_longconv",
        grid=(ncb, b_, nb),
        in_specs=[pl.BlockSpec((2 * nb - 1, 2 * t, LANE), lambda c, b, i: (0, 0, c)),
                  pl.BlockSpec((1, nb, 2 * t, LANE), lambda c, b, i: (b, 0, 0, c)),
                  pl.BlockSpec((1, t, LANE), lambda c, b, i: (b, i, c)),
                  pl.BlockSpec((1, t, LANE), lambda c, b, i: (b, i, c)),
                  pl.BlockSpec((1, LANE), lambda c, b, i: (0, c)),
                  pl.BlockSpec((t, 2 * t), lambda c, b, i: (0, 0))],
        out_specs=pl.BlockSpec((1, t, LANE), lambda c, b, i: (b, i, c)),
        out_shape=jax.ShapeDtypeStruct((b_, L, W_BR), F32),
        compiler_params=_params(("parallel", "parallel", "parallel")),
    )(ghat, uh, x0, u, h_bias[None], jnp.asarray(inv))


def _s5_kernel(nc, u_ref, h0_ref, bb_ref, ap_ref, cb_ref, dsk_ref, wg_ref, y_ref, hf_ref, yacc, state):
    s = pl.program_id(1)
    t = u_ref.shape[1]
    n = S5_LANES

    @pl.when(s == 0)
    def _():
        hf_ref[...] = jnp.zeros_like(hf_ref)

    def phase(d):
        chunk = s if d == 0 else 2 * nc - 1 - s
        first = (s == 0) if d == 0 else (s == nc)
        last = (s == nc - 1) if d == 0 else (s == 2 * nc - 1)

        @pl.when(first)
        def _():
            state[0:1, :] = h0_ref[0, d:d + 1, :]

        u = u_ref[0]
        x = _dot(u.astype(BF16), bb_ref[d])
        row = _iota((t, n), 0)
        st = state[0:1, :]
        pw = (lambda p: p - 1) if d == 0 else (lambda p: SUBLANE - p)
        a_re, a_im = ap_ref[d, pw(1):pw(1) + 1, 0:n], ap_ref[d, pw(1):pw(1) + 1, n:2 * n]
        c_re = a_re * st[:, 0:n] - a_im * st[:, n:2 * n]
        c_im = a_re * st[:, n:2 * n] + a_im * st[:, 0:n]
        entry = 0 if d == 0 else t - 1
        x_re = x[:, 0:n] + jnp.where(row == entry, c_re, 0.0)
        x_im = x[:, n:2 * n] + jnp.where(row == entry, c_im, 0.0)
        sub = row % SUBLANE
        for sh in (1, 2, 4):
            a_re, a_im = ap_ref[d, pw(sh):pw(sh) + 1, 0:n], ap_ref[d, pw(sh):pw(sh) + 1, n:2 * n]
            if d == 0:
                valid = sub >= sh
                s_re = jnp.where(valid, pltpu.roll(x_re, sh, 0), 0.0)
                s_im = jnp.where(valid, pltpu.roll(x_im, sh, 0), 0.0)
            else:
                valid = sub < SUBLANE - sh
                s_re = jnp.where(valid, pltpu.roll(x_re, t - sh, 0), 0.0)
                s_im = jnp.where(valid, pltpu.roll(x_im, t - sh, 0), 0.0)
            x_re, x_im = x_re + a_re * s_re - a_im * s_im, x_im + a_re * s_im + a_im * s_re
        p_re, p_im = ap_ref[d, :, 0:n], ap_ref[d, :, n:2 * n]
        n_grp = t // SUBLANE
        out_re, out_im = [None] * n_grp, [None] * n_grp
        c_re = c_im = None
        for g in (range(n_grp) if d == 0 else range(n_grp - 1, -1, -1)):
            g_re, g_im = x_re[g * SUBLANE:(g + 1) * SUBLANE], x_im[g * SUBLANE:(g + 1) * SUBLANE]
            if c_re is not None:
                g_re, g_im = g_re + p_re * c_re - p_im * c_im, g_im + p_re * c_im + p_im * c_re
            out_re[g], out_im[g] = g_re, g_im
            edge = SUBLANE - 1 if d == 0 else 0
            c_re, c_im = g_re[edge:edge + 1], g_im[edge:edge + 1]
        x_re, x_im = jnp.concatenate(out_re, axis=0), jnp.concatenate(out_im, axis=0)
        ex = t - 1 if d == 0 else 0
        state[0:1, 0:n] = x_re[ex:ex + 1, :]
        state[0:1, n:2 * n] = x_im[ex:ex + 1, :]
        yd = _dot(x_re.astype(BF16), cb_ref[d, 0:n, :]) + _dot(x_im.astype(BF16), cb_ref[d, n:2 * n, :])
        rows = pl.ds(pl.multiple_of(chunk * t, t), t)
        if d == 0:
            yacc[rows, :] = dsk_ref[...] * u + yd
        else:
            y = yacc[rows, :] + yd
            y = 0.5 * y * (1.0 + jnp.tanh(math.sqrt(2.0 / math.pi) * (y + 0.044715 * y * y * y)))
            y_ref[0] = y * _sigmoid(_bdot(y, wg_ref[...]))

        @pl.when(last)
        def _():
            hf_ref[0, d:d + 1, :] = state[0:1, :]

    @pl.when(s < nc)
    def _():
        phase(0)

    @pl.when(s >= nc)
    def _():
        phase(1)


def _two_pass_chunk(nc):
    return lambda s: jnp.where(s < nc, s, 2 * nc - 1 - s)


def _two_pass_out(nc):
    return lambda s: jnp.where(s < nc, nc - 1, 2 * nc - 1 - s)


def s5_prepare(lam_re, lam_im, log_step, b_re, b_im, c_re, c_im, chunk):
    step = jnp.exp(log_step)[:, :, None]
    mag = jnp.exp(lam_re * step)
    lb_re, lb_im = mag * jnp.cos(lam_im * step), mag * jnp.sin(lam_im * step)
    den = lam_re * lam_re + lam_im * lam_im
    q_re = ((lb_re - 1.0) * lam_re + lb_im * lam_im) / den
    q_im = (lb_im * lam_re - (lb_re - 1.0) * lam_im) / den
    bb_re = q_re[..., None] * b_re - q_im[..., None] * b_im
    bb_im = q_re[..., None] * b_im + q_im[..., None] * b_re
    eye = jnp.eye(S5_GROUPS, dtype=F32)

    def blockdiag_in(m):
        return jnp.einsum('dgni,gh->dgihn', m, eye).reshape(2, W_BR, S5_LANES)

    def blockdiag_out(m):
        return jnp.einsum('dgin,gh->dgnhi', m, eye).reshape(2, S5_LANES, W_BR)

    bblk = jnp.concatenate([blockdiag_in(bb_re), blockdiag_in(bb_im)], axis=-1)
    cblk = jnp.concatenate([blockdiag_out(c_re), -blockdiag_out(c_im)], axis=1)
    pows_re, pows_im = [lb_re], [lb_im]
    for _ in range(SUBLANE - 1):
        r, i = pows_re[-1], pows_im[-1]
        pows_re.append(r * lb_re - i * lb_im)
        pows_im.append(r * lb_im + i * lb_re)
    flat = lambda ps: jnp.stack([p.reshape(2, S5_LANES) for p in ps], axis=1)
    apow = jnp.concatenate([flat(pows_re), flat(pows_im)], axis=-1)
    apow = jnp.stack([apow[0], apow[1, ::-1]])
    return bblk.astype(BF16), apow, cblk.astype(BF16)


def s5_mixer(u, h0, bblk, apow, cblk, d_skip, w_glu):
    b_, L, _ = u.shape
    t = min(S5_CHUNK, L)
    nc = L // t
    chunk_of, out_of = _two_pass_chunk(nc), _two_pass_out(nc)
    const = lambda a: pl.BlockSpec(a.shape, lambda b, s: (0,) * a.ndim)
    dsk = d_skip[None]
    wg = w_glu.astype(BF16)
    return pl.pallas_call(
        functools.partial(_s5_kernel, nc),
        name="s5_mixer",
        grid=(b_, 2 * nc),
        in_specs=[pl.BlockSpec((1, t, W_BR), lambda b, s: (b, chunk_of(s), 0)),
                  pl.BlockSpec((1, SUBLANE, 2 * S5_LANES), lambda b, s: (b, 0, 0)),
                  const(bblk), const(apow), const(cblk), const(dsk), const(wg)],
        out_specs=[pl.BlockSpec((1, t, W_BR), lambda b, s: (b, out_of(s), 0)),
                   pl.BlockSpec((1, SUBLANE, 2 * S5_LANES), lambda b, s: (b, 0, 0))],
        out_shape=[jax.ShapeDtypeStruct((b_, L, W_BR), F32),
                   jax.ShapeDtypeStruct((b_, SUBLANE, 2 * S5_LANES), F32)],
        scratch_shapes=[pltpu.VMEM((L, W_BR), F32), pltpu.VMEM((SUBLANE, 2 * S5_LANES), F32)],
        compiler_params=_params(("parallel", "arbitrary")),
    )(u, h0, bblk, apow, cblk, dsk, wg)


def _head_mask(h, shape, axis):
    lane = _iota(shape, axis)
    return (lane >= h * RW_HEAD) & (lane < (h + 1) * RW_HEAD)


def _split3(x):
    hi = x.astype(BF16)
    r1 = x - hi.astype(F32)
    mid = r1.astype(BF16)
    return hi, mid, (r1 - mid.astype(F32)).astype(BF16)


def _stack_heads(x):
    xb = x.astype(BF16)
    zero = jnp.zeros_like(xb)
    return jnp.concatenate([jnp.where(_head_mask(h, xb.shape, 1), xb, zero) for h in range(RW_HEADS)], axis=0)


def _unstack_heads(x, t):
    return x[0:t] + x[t:2 * t] + x[2 * t:3 * t] + x[3 * t:4 * t]


def _split2(x):
    hi = x.astype(BF16)
    return hi, (x - hi.astype(F32)).astype(BF16)


def _dot_exact_rhs(a, b):
    hi, lo = _split2(a)
    return _dot(hi, b) + _dot(lo, b)


def _dot_split(a, b_hi, b_lo):
    hi, lo = _split2(a)
    return _dot(hi, b_hi) + _dot(hi, b_lo) + _dot(lo, b_hi)


RW_INV_BASE = 8


def _cat_to_blockdiag(x):
    t, n = x.shape
    xb = x.astype(BF16)
    tiled = jnp.concatenate([xb] * (n // t), axis=0)
    same = (_iota((n, n), 0) // t) == (_iota((n, n), 1) // t)
    return jnp.where(same, tiled, jnp.zeros_like(tiled))


def _tri_inverse(nmat):
    t = nmat.shape[0]
    row, col = _iota(nmat.shape, 0), _iota(nmat.shape, 1) % t
    same = lambda s: (row // s) == (col // s)
    mul = lambda x, y: _dot(x.astype(BF16), _cat_to_blockdiag(y))
    base = jnp.where(same(RW_INV_BASE), nmat, 0.0)
    m = jnp.where(row == col, 1.0, 0.0) - base
    p = base
    for _ in range(RW_INV_BASE.bit_length() - 2):
        p = mul(p, p)
        m = m + mul(m, p)
    s = 2 * RW_INV_BASE
    while s <= t:
        c = jnp.where(same(s) & jnp.logical_not(same(s // 2)), nmat, 0.0)
        m = m - mul(mul(m, c), m)
        s *= 2
    return m


def _wkv_chunk(d, r, logw, kd, v, kk, a, hst):
    t = r.shape[0]
    bf = lambda x: x.astype(BF16)
    ri, ci = _iota((t, t), 0), _iota((t, t), 1)
    tri = ((ci <= ri) if d == 0 else (ci >= ri)).astype(BF16)
    cs = sum(_dot(tri, part) for part in _split3(logw))
    tot = jnp.sum(logw, axis=0, keepdims=True)
    g_inv = jnp.exp(-cs)
    g_tot = jnp.exp(tot)
    bt, kt = kk * a * g_inv, kd * g_inv
    qa, rh = kk * jnp.exp(cs - logw), r * jnp.exp(cs)
    bts, kts, vs = _stack_heads(bt), _stack_heads(kt), _stack_heads(v)
    bhs, khs = _stack_heads(bt * g_tot), _stack_heads(kt * g_tot)
    row, col = _iota((t, RW_HEADS * t), 0), _iota((t, RW_HEADS * t), 1) % t
    incl = (col <= row) if d == 0 else (col >= row)
    strict = (col < row) if d == 0 else (col > row)
    qa_b, rh_b = bf(qa), bf(rh)
    a_ab = jnp.where(strict, _dot_nt(qa_b, bts), 0.0)
    a_ak = jnp.where(strict, _dot_nt(qa_b, kts), 0.0)
    a_rb = bf(jnp.where(incl, _dot_nt(rh_b, bts), 0.0))
    a_rk = jnp.where(incl, _dot_nt(rh_b, kts), 0.0)
    m = bf(_tri_inverse(a_ab))
    ul = _stack_heads(-_dot(m, _stack_heads(_dot(bf(a_ak), vs))))
    w0 = _stack_heads(_dot(m, _stack_heads(qa)))
    yloc = _dot(bf(a_rk), vs) + _dot(a_rb, ul)
    yh = rh - _dot(a_rb, w0)
    dn = W_BR
    p_bd = jnp.where(_iota((dn, dn), 0) == _iota((dn, dn), 1), g_tot, 0.0) - _dot_tn(bhs, w0)
    q_bd = _dot_tn(bhs, ul) + _dot_tn(khs, vs)
    hb = bf(hst)
    return yloc + _dot(bf(yh), hb), _dot(bf(p_bd), hb) + q_bd


def _rwkv_kernel(ng, cur_ref, prev_ref, next_ref, h0_ref, mu_ref, w0_ref, a0_ref, lrh_ref, lrl_ref,
                 kk_ref, ka_ref, rk_ref, lnw_ref, lnb_ref, y_ref, hf_ref, yacc, hst):
    s = pl.program_id(1)
    t = cur_ref.shape[1]
    n = W_BR
    ri, ci = _iota((n, n), 0), _iota((n, n), 1)
    head_sum = ((ri // RW_HEAD) == (ci // RW_HEAD)).astype(BF16)

    def phase(d):
        chunk = s if d == 0 else 2 * ng - 1 - s
        first = (s == 0) if d == 0 else (s == ng)
        last = (s == ng - 1) if d == 0 else (s == 2 * ng - 1)

        @pl.when(first)
        def _():
            hst[...] = h0_ref[0, d]

        cur = cur_ref[0]
        prev_row, next_row = _halo_rows(prev_ref, next_ref, chunk, ng)
        xm, xp = _shift_rows(cur, prev_row, next_row)
        p = cur + (0.5 * (xm + xp) - cur) * mu_ref[...]
        r, k, v, lr = p[:, 0:n], p[:, n:2 * n], p[:, 2 * n:3 * n], p[:, 3 * n:4 * n]
        kk = k * kk_ref[...]
        kk = kk * lax.rsqrt(jnp.maximum(_dot_exact_rhs(kk * kk, head_sum), 1e-24))
        lane = _iota(lr.shape, 1)
        feats = jnp.where(lane < 2 * RW_W_RANK, jnp.tanh(lr),
                          jnp.where(lane < 2 * (RW_W_RANK + RW_A_RANK), lr, _sigmoid(lr)))
        cols = (2 if d == 0 else 3) * n
        low = _dot_split(feats, lrh_ref[d, :, 0:cols], lrl_ref[d, :, 0:cols])
        w = w0_ref[d:d + 1, :] + low[:, 0:n]
        logw = -math.exp(-0.5) * _sigmoid(w)
        a = _sigmoid(a0_ref[d:d + 1, :] + low[:, n:2 * n])
        kd = k * (1.0 + (a - 1.0) * ka_ref[...])
        nsub = t // RW_SUB
        h = hst[...]
        ys = [None] * nsub
        for q in (range(nsub) if d == 0 else range(nsub - 1, -1, -1)):
            sl = slice(q * RW_SUB, (q + 1) * RW_SUB)
            ys[q], h = _wkv_chunk(d, r[sl], logw[sl], kd[sl], v[sl], kk[sl], a[sl], h)
        hst[...] = h
        yd = jnp.concatenate(ys, axis=0)
        rows = pl.ds(pl.multiple_of(chunk * t, t), t)
        if d == 0:
            yacc[rows, :] = yd
        else:
            y = yacc[rows, :] + yd
            mean = _dot_exact_rhs(y, head_sum) * (1.0 / RW_HEAD)
            yc = y - mean
            var = _dot_exact_rhs(yc * yc, head_sum) * (1.0 / RW_HEAD)
            y = yc * lax.rsqrt(var + RW_GN_EPS) * lnw_ref[...] + lnb_ref[...]
            bonus = _dot_exact_rhs(r * k * rk_ref[...], head_sum) * v
            y_ref[0] = (y + bonus) * low[:, 2 * n:3 * n]

        @pl.when(last)
        def _():
            hf_ref[0, d] = hst[...]

    @pl.when(s < ng)
    def _():
        phase(0)

    @pl.when(s >= ng)
    def _():
        phase(1)


def rwkv_prepare(mu, w0, w2, a0, a2, g2, k_k, k_a, r_k, ln_w, ln_b):
    n = W_BR
    mu_p = jnp.zeros((1, RW_PAD), F32).at[0, :RW_COLS].set(mu)
    w2_p = jnp.zeros((2, n, n), F32)
    a2_p = jnp.zeros((2, n, n), F32)
    for d in range(2):
        w2_p = w2_p.at[d, d * RW_W_RANK:(d + 1) * RW_W_RANK].set(w2[d])
        o = 2 * RW_W_RANK + d * RW_A_RANK
        a2_p = a2_p.at[d, o:o + RW_A_RANK].set(a2[d])
    o = 2 * RW_W_RANK + 2 * RW_A_RANK
    g2_p = jnp.zeros((n, n), F32).at[o:o + RW_G_RANK].set(g2)
    low = jnp.concatenate([w2_p, a2_p, jnp.broadcast_to(g2_p, (2, n, n))], axis=-1)
    low_hi = low.astype(BF16)
    low_lo = (low - low_hi.astype(F32)).astype(BF16)
    return (mu_p, w0, a0, low_hi, low_lo, k_k[None], k_a[None], r_k.reshape(1, n), ln_w[None], ln_b[None])


def rwkv_mixer(rw, h0, prm):
    b_, L, _ = rw.shape
    t = min(RW_STEP, L)
    ng = L // t
    chunk_of, out_of = _two_pass_chunk(ng), _two_pass_out(ng)
    prev_spec, next_spec = _halo_specs(t, ng, RW_PAD, 0, chunk_of)
    const = lambda a: pl.BlockSpec(a.shape, lambda b, s: (0,) * a.ndim)
    st_spec = pl.BlockSpec((1, 2, W_BR, W_BR), lambda b, s: (b, 0, 0, 0))
    return pl.pallas_call(
        functools.partial(_rwkv_kernel, ng),
        name="rwkv_mixer",
        grid=(b_, 2 * ng),
        in_specs=[pl.BlockSpec((1, t, RW_PAD), lambda b, s: (b, chunk_of(s), 0)), prev_spec, next_spec, st_spec]
                 + [const(a) for a in prm],
        out_specs=[pl.BlockSpec((1, t, W_BR), lambda b, s: (b, out_of(s), 0)), st_spec],
        out_shape=[jax.ShapeDtypeStruct((b_, L, W_BR), F32), jax.ShapeDtypeStruct((b_, 2, W_BR, W_BR), F32)],
        scratch_shapes=[pltpu.VMEM((L, W_BR), F32), pltpu.VMEM((W_BR, W_BR), F32)],
        compiler_params=_params(("parallel", "arbitrary")),
    )(rw, rw, rw, h0, *prm)


def _m2_kernel(nc, z_ref, cur_ref, prev_ref, next_ref, dt_ref, h0_ref, cw_ref, cb_ref, ex_ref, a_ref, dtb_ref,
               dsk_ref, nw_ref, y_ref, hf_ref, yacc, hst):
    s = pl.program_id(1)
    t = cur_ref.shape[1]
    n = W_BR
    ns = M2_STATE
    ri, ci = _iota((t, t), 0), _iota((t, t), 1)

    def phase(d):
        chunk = s if d == 0 else 2 * nc - 1 - s
        first = (s == 0) if d == 0 else (s == nc)
        last = (s == nc - 1) if d == 0 else (s == 2 * nc - 1)

        @pl.when(first)
        def _():
            hst[...] = h0_ref[0, d]

        cur = cur_ref[0]
        prev_row, next_row = _halo_rows(prev_ref, next_ref, chunk, nc)
        xm, xp = _shift_rows(cur, prev_row, next_row)
        xbc = _silu(cb_ref[...] + xm * cw_ref[0:1, :] + cur * cw_ref[1:2, :] + xp * cw_ref[2:3, :])
        xs, bm, cm = xbc[:, 0:n], xbc[:, n:2 * n], xbc[:, 2 * n:3 * n]
        raw = sum(_dot(part, ex_ref[d]) for part in _split3(dt_ref[0])) + dtb_ref[d:d + 1, :]
        dtd = jnp.maximum(raw, 0.0) + jnp.log(1.0 + jnp.exp(-jnp.abs(raw)))
        da = dtd * a_ref[d:d + 1, :]
        incl = (ci <= ri) if d == 0 else (ci >= ri)
        tri = incl.astype(BF16)
        da_parts = _split3(da)
        acs = sum(_dot(tri, part) for part in da_parts)
        tot = jnp.sum(da, axis=0, keepdims=True)
        xdt = xs * dtd
        xdt_b = xdt.astype(BF16)
        bm_b, cm_b = bm.astype(BF16), cm.astype(BF16)
        sel = jnp.where(_iota((SUBLANE, n), 1) // M2_HEADDIM == _iota((SUBLANE, n), 0), 1.0 / M2_HEADDIM,
                        0.0).astype(BF16)
        acs_t = sum(_dot_nt(sel, part) for part in _split3(acs))
        ydiag = jnp.zeros((t, n), F32)
        for g in range(M2_GROUPS):
            cb = _dot_nt(cm_b[:, g * ns:(g + 1) * ns], bm_b[:, g * ns:(g + 1) * ns])
            for h in range(g * (M2_HEADS // M2_GROUPS), (g + 1) * (M2_HEADS // M2_GROUPS)):
                seg = acs[:, h * M2_HEADDIM:h * M2_HEADDIM + 1] - acs_t[h:h + 1, :]
                scores = cb * jnp.exp(jnp.where(incl, seg, -jnp.inf))
                ydiag = jnp.where(_head_mask(h, (t, n), 1), _dot(scores.astype(BF16), xdt_b), ydiag)
        h_in = hst[...]
        h_b = h_in.astype(BF16)
        lane = _iota((t, n), 1)
        yoff = jnp.where(lane < n // M2_GROUPS, _dot(cm_b[:, 0:ns], h_b), _dot(cm_b[:, ns:2 * ns], h_b))
        yd = ydiag + jnp.exp(acs) * yoff
        xdec = (xdt * jnp.exp(tot - acs)).astype(BF16)
        lane_s = _iota((ns, n), 1)
        new = jnp.where(lane_s < n // M2_GROUPS, _dot_tn(bm_b[:, 0:ns], xdec), _dot_tn(bm_b[:, ns:2 * ns], xdec))
        hst[...] = h_in * jnp.exp(tot) + new
        rows = pl.ds(pl.multiple_of(chunk * t, t), t)
        if d == 0:
            yacc[rows, :] = yd + dsk_ref[...] * xs
        else:
            y = (yacc[rows, :] + yd) * _silu(z_ref[0])
            y_ref[0] = _rms(y) * nw_ref[...]

        @pl.when(last)
        def _():
            hf_ref[0, d] = hst[...]

    @pl.when(s < nc)
    def _():
        phase(0)

    @pl.when(s >= nc)
    def _():
        phase(1)


def mamba_prepare(conv_w, conv_b, a_log, dt_bias, d_skip, norm_w):
    n = W_BR
    rep = lambda v: jnp.repeat(v, M2_HEADDIM, axis=-1)
    expand = jnp.zeros((2, LANE, n), F32)
    for d in range(2):
        for h in range(M2_HEADS):
            expand = expand.at[d, d * M2_HEADS + h, h * M2_HEADDIM:(h + 1) * M2_HEADDIM].set(1.0)
    return (conv_w, conv_b[None], expand.astype(BF16), rep(-jnp.exp(a_log)), rep(dt_bias), rep(d_skip)[None],
            norm_w[None])


def mamba_mixer(z, xbc, dt, h0, prm):
    b_, L, _ = z.shape
    t = min(M2_CHUNK, L)
    nc = L // t
    chunk_of, out_of = _two_pass_chunk(nc), _two_pass_out(nc)
    prev_spec, next_spec = _halo_specs(t, nc, M2_XBC, 0, chunk_of)
    const = lambda a: pl.BlockSpec(a.shape, lambda b, s: (0,) * a.ndim)
    st_spec = pl.BlockSpec((1, 2, M2_STATE, W_BR), lambda b, s: (b, 0, 0, 0))
    seq = lambda w: pl.BlockSpec((1, t, w), lambda b, s: (b, chunk_of(s), 0))
    return pl.pallas_call(
        functools.partial(_m2_kernel, nc),
        name="mamba_mixer",
        grid=(b_, 2 * nc),
        in_specs=[seq(W_BR), seq(M2_XBC), prev_spec, next_spec, seq(LANE), st_spec] + [const(a) for a in prm],
        out_specs=[pl.BlockSpec((1, t, W_BR), lambda b, s: (b, out_of(s), 0)), st_spec],
        out_shape=[jax.ShapeDtypeStruct((b_, L, W_BR), F32),
                   jax.ShapeDtypeStruct((b_, 2, M2_STATE, W_BR), F32)],
        scratch_shapes=[pltpu.VMEM((L, W_BR), F32), pltpu.VMEM((M2_STATE, W_BR), F32)],
        compiler_params=_params(("parallel", "arbitrary")),
    )(z, xbc, xbc, xbc, dt, h0, *prm)


def _merge_kernel(x_ref, g_ref, sh_ref, sc_ref, gt_ref, y0_ref, y1_ref, y2_ref, y3_ref, wg_ref, wb_ref, wo_ref,
                  o_ref):
    x = x_ref[...]
    h = _modulate(x, g_ref[...], sh_ref[0], sc_ref[0]).astype(BF16)
    acc = jnp.zeros(x.shape, F32)
    for i, y_ref in enumerate((y0_ref, y1_ref, y2_ref, y3_ref)):
        gate = _sigmoid(_dot(h, wg_ref[i]))
        acc += gate * _dot(y_ref[...].astype(BF16), wb_ref[i])
    o_ref[...] = x + gt_ref[0] * _dot(acc.astype(BF16), wo_ref[...])


def merge_branches(xt, g, shift, scale, gate, ys, w_gate, w_branch, w_out, rows_per_mod, tm):
    n = xt.shape[0]
    tiles_per_mod = rows_per_mod // tm
    mod_spec = pl.BlockSpec((1, 1, D_MODEL), lambda i: (i // tiles_per_mod, 0, 0))
    const = lambda a: pl.BlockSpec(a.shape, lambda i: (0,) * a.ndim)
    tok = lambda w: pl.BlockSpec((tm, w), lambda i: (i, 0))
    return pl.pallas_call(
        _merge_kernel,
        name="merge_branches",
        grid=(n // tm,),
        in_specs=[tok(D_MODEL), const(g), mod_spec, mod_spec, mod_spec] + [tok(W_BR)] * N_BRANCH
                 + [const(w_gate), const(w_branch), const(w_out)],
        out_specs=tok(D_MODEL),
        out_shape=jax.ShapeDtypeStruct((n, D_MODEL), F32),
        compiler_params=_params(("parallel",)),
    )(xt, g, shift, scale, gate, *ys, w_gate, w_branch, w_out)


def _router_kernel(x_ref, g_ref, sh_ref, sc_ref, wr_ref, comb_ref):
    u = _modulate(x_ref[...], g_ref[...], sh_ref[0], sc_ref[0])
    logits = _dot(u, wr_ref[...], HI)
    lane = _iota(logits.shape, 1).astype(F32)
    neg = -jnp.inf
    is_grp = (lane >= MOE_EXPERTS) & (lane < MOE_EXPERTS + MOE_GROUPS)
    gl = jnp.where(is_grp, logits, neg)
    gmax = jnp.max(gl, axis=-1, keepdims=True)
    grp_p = 1.0 / jnp.sum(jnp.exp(gl - gmax), axis=-1, keepdims=True)
    grp_idx = jnp.min(jnp.where(gl == gmax, lane, 4.0 * LANE), axis=-1, keepdims=True) - MOE_EXPERTS
    in_grp = (lane >= grp_idx * MOE_PER_GROUP) & (lane < (grp_idx + 1) * MOE_PER_GROUP)
    el = jnp.where(in_grp, logits, neg)
    v1 = jnp.max(el, axis=-1, keepdims=True)
    i1 = jnp.min(jnp.where(el == v1, lane, 4.0 * LANE), axis=-1, keepdims=True)
    el2 = jnp.where(lane == i1, neg, el)
    v2 = jnp.max(el2, axis=-1, keepdims=True)
    i2 = jnp.min(jnp.where(el2 == v2, lane, 4.0 * LANE), axis=-1, keepdims=True)
    e2 = jnp.exp(v2 - v1)
    w1 = grp_p / (1.0 + e2)
    comb = jnp.where(lane == i1, w1, jnp.where(lane == i2, w1 * e2, 0.0))
    comb_ref[...] = jnp.where(lane == MOE_GROUP_LANE, grp_idx, comb)


def moe_router(xt, g, shift, scale, w_route, rows_per_mod, tm):
    n = xt.shape[0]
    tiles_per_mod = rows_per_mod // tm
    mod_spec = pl.BlockSpec((1, 1, D_MODEL), lambda i: (i // tiles_per_mod, 0, 0))
    return pl.pallas_call(
        _router_kernel,
        name="moe_router",
        grid=(n // tm,),
        in_specs=[pl.BlockSpec((tm, D_MODEL), lambda i: (i, 0)), pl.BlockSpec((1, D_MODEL), lambda i: (0, 0)),
                  mod_spec, mod_spec, pl.BlockSpec((D_MODEL, LANE), lambda i: (0, 0))],
        out_specs=pl.BlockSpec((tm, LANE), lambda i: (i, 0)),
        out_shape=jax.ShapeDtypeStruct((n, LANE), F32),
        compiler_params=_params(("parallel",)),
    )(xt, g, shift, scale, w_route)


def _moe_kernel(final_norm, meta_ref, x_ref, g_ref, sh_ref, sc_ref, gt_ref, comb_ref, wg_ref, wu_ref, wd_ref, nf_ref,
                o_ref, pt_sc, xs_sc, cs_sc, acc):
    i = pl.program_id(0)
    e = pl.program_id(1)
    tm = x_ref.shape[0]

    @pl.when(e == 0)
    def _():
        u = _modulate(x_ref[...], g_ref[...], sh_ref[0], sc_ref[0]).astype(BF16)
        comb = comb_ref[...]
        lane = _iota((tm, LANE), 1)
        member = comb[:, MOE_GROUP_LANE:MOE_GROUP_LANE + 1] == lane.astype(F32)
        tri = jnp.where(_iota((tm, tm), 1) <= _iota((tm, tm), 0), 1.0, 0.0).astype(BF16)
        rank = _dot(tri, jnp.where(member, 1.0, 0.0).astype(BF16))
        offs = jnp.zeros((tm, LANE), jnp.int32)
        for grp in range(MOE_GROUPS):
            offs = jnp.where(lane == grp, meta_ref[i, grp], offs)
        pos = jnp.sum(jnp.where(member, offs.astype(F32) + rank - 1.0, 0.0), axis=-1, keepdims=True)
        pt = jnp.where(pos == _iota((tm, tm), 1).astype(F32), 1.0, 0.0).astype(BF16)
        pt_sc[...] = pt
        xs_sc[...] = _dot_tn(pt, u).astype(BF16)
        cs_sc[...] = sum(_dot_tn(pt, part) for part in _split3(comb))
        acc[...] = jnp.zeros_like(acc)

    grp = e // MOE_PER_GROUP
    start = meta_ref[i, grp]
    count = meta_ref[i, MOE_GROUPS + grp]
    lo = start // MOE_SUB
    hi = jnp.where(count > 0, (start + count + MOE_SUB - 1) // MOE_SUB, lo)

    def body(j, carry):
        rows = pl.ds(pl.multiple_of(j * MOE_SUB, MOE_SUB), MOE_SUB)
        xs = xs_sc[rows, :]
        h = _silu(_dot(xs, wg_ref[0])) * _dot(xs, wu_ref[0])
        cs = cs_sc[rows, :]
        w = jnp.sum(jnp.where(_iota(cs.shape, 1) == e, cs, 0.0), axis=-1, keepdims=True)
        acc[rows, :] += w * _dot(h.astype(BF16), wd_ref[0])
        return carry

    lax.fori_loop(lo, hi, body, 0)

    @pl.when(e == MOE_EXPERTS - 1)
    def _():
        y = x_ref[...] + gt_ref[0] * _dot_exact_rhs_t(pt_sc[...], acc[...])
        o_ref[...] = _rms(y) * nf_ref[...] if final_norm else y


def _dot_exact_rhs_t(sel, a):
    hi, lo = _split2(a)
    return _dot(sel, hi) + _dot(sel, lo)


def moe_experts(xt, g, shift, scale, gate, comb, w_gate, w_up, w_down, norm_final, final_norm, rows_per_mod, tm):
    n = xt.shape[0]
    tiles_per_mod = rows_per_mod // tm
    gid = comb[:, MOE_GROUP_LANE].astype(jnp.int32).reshape(n // tm, tm)
    counts = jnp.sum(gid[:, :, None] == jnp.arange(MOE_GROUPS, dtype=jnp.int32), axis=1, dtype=jnp.int32)
    meta = jnp.concatenate([jnp.cumsum(counts, axis=1) - counts, counts], axis=1)
    mod_spec = pl.BlockSpec((1, 1, D_MODEL), lambda i, e, m: (i // tiles_per_mod, 0, 0))
    return pl.pallas_call(
        functools.partial(_moe_kernel, final_norm),
        name="moe_experts",
        grid_spec=pltpu.PrefetchScalarGridSpec(
            num_scalar_prefetch=1,
            grid=(n // tm, MOE_EXPERTS),
            in_specs=[pl.BlockSpec((tm, D_MODEL), lambda i, e, m: (i, 0)),
                      pl.BlockSpec((1, D_MODEL), lambda i, e, m: (0, 0)), mod_spec, mod_spec, mod_spec,
                      pl.BlockSpec((tm, LANE), lambda i, e, m: (i, 0)),
                      pl.BlockSpec((1, D_MODEL, MOE_FF), lambda i, e, m: (e, 0, 0)),
                      pl.BlockSpec((1, D_MODEL, MOE_FF), lambda i, e, m: (e, 0, 0)),
                      pl.BlockSpec((1, MOE_FF, D_MODEL), lambda i, e, m: (e, 0, 0)),
                      pl.BlockSpec((1, D_MODEL), lambda i, e, m: (0, 0))],
            out_specs=pl.BlockSpec((tm, D_MODEL), lambda i, e, m: (i, 0)),
            scratch_shapes=[pltpu.VMEM((tm, tm), BF16), pltpu.VMEM((tm, D_MODEL), BF16),
                            pltpu.VMEM((tm, LANE), F32), pltpu.VMEM((tm, D_MODEL), F32)]),
        out_shape=jax.ShapeDtypeStruct((n, D_MODEL), F32),
        compiler_params=_params(("parallel", "arbitrary")),
    )(meta, xt, g, shift, scale, gate, comb, w_gate, w_up, w_down, norm_final)


def _mix_weights(w_in):
    o = 0
    parts = []
    for cols, padded in ((HY_COLS, HY_COLS), (S5_COLS, S5_COLS), (RW_COLS, RW_PAD), (M2_COLS, M2_PAD)):
        parts.append(jnp.pad(w_in[:, o:o + cols], ((0, 0), (0, padded - cols))))
        o += cols
    w_gate = w_in[:, o:].reshape(D_MODEL, N_BRANCH, D_MODEL).transpose(1, 0, 2)
    return jnp.concatenate(parts, axis=1).astype(BF16), w_gate.astype(BF16)


def kernel(x, c, ctx, c_ctx, mod_w, mod_b, norm_mix, norm_ffn, w_in, hy_conv_w, hy_conv_b, hy_f_w1, hy_f_b1, hy_f_w2, hy_f_b2, hy_f_w3, hy_f_freq, hy_bias, s5_lam_re, s5_lam_im, s5_log_step, s5_b_re, s5_b_im, s5_c_re, s5_c_im, s5_d, s5_w_glu, rw_mu, rw_w0, rw_w2, rw_a0, rw_a2, rw_g2, rw_k_k, rw_k_a, rw_r_k, rw_ln_w, rw_ln_b, m2_conv_w, m2_conv_b, m2_a_log, m2_dt_bias, m2_d, m2_norm_w, w_branch, w_out, moe_w_group, moe_w_expert, moe_w_gate, moe_w_up, moe_w_down, norm_final):
    b_, L, _ = x.shape
    lc = ctx.shape[1]
    depth = mod_w.shape[0]
    tm = 512
    tmc = min(tm, lc)

    cvec = jnp.zeros((SUBLANE, D_MODEL), F32).at[:b_].set(c).at[b_].set(c_ctx)
    mod = adaln_mod(cvec, mod_w, mod_b)

    xt = x.reshape(b_ * L, D_MODEL)
    ct = ctx.reshape(b_ * lc, D_MODEL)
    for l in range(depth):
        ctx_out = l < depth - 1
        mx = mod[l, :b_].reshape(b_, 1, N_MOD, D_MODEL)
        mc = mod[l, b_:b_ + 1].reshape(1, 1, N_MOD, D_MODEL)
        sh1, sc1, g1, sh2, sc2, g2 = (mx[:, :, i] for i in range(N_MOD))
        csh1, csc1, cg1, csh2, csc2, cg2 = (mc[:, :, i] for i in range(N_MOD))
        w_mix, w_gate = _mix_weights(w_in[l])
        nm, nf = norm_mix[l][None], norm_ffn[l][None]

        s5_prm = s5_prepare(s5_lam_re[l], s5_lam_im[l], s5_log_step[l], s5_b_re[l], s5_b_im[l], s5_c_re[l],
                            s5_c_im[l], S5_CHUNK)
        rw_prm = rwkv_prepare(rw_mu[l], rw_w0[l], rw_w2[l], rw_a0[l], rw_a2[l], rw_g2[l], rw_k_k[l], rw_k_a[l],
                              rw_r_k[l], rw_ln_w[l], rw_ln_b[l])
        m2_prm = mamba_prepare(m2_conv_w[l], m2_conv_b[l], m2_a_log[l], m2_dt_bias[l], m2_d[l], m2_norm_w[l])
        hy_f = (hy_f_w1[l], hy_f_b1[l], hy_f_w2[l], hy_f_b2[l], hy_f_w3[l], hy_f_freq[l])

        def mixers(tokens, n_tok, shift, scale, rows_per_mod, tile, states, want_hyena):
            hy, s5, rw, m2z, m2x, m2dt = in_projection(tokens, nm, shift, scale, w_mix, rows_per_mod, tile)
            seq = lambda a: a.reshape(b_, n_tok, a.shape[-1])
            y_hy = None
            if want_hyena:
                t = min(HY_BLOCK, n_tok)
                ghat = hyena_filter_spectra(hyena_filter(n_tok, *hy_f), n_tok, t)
                y_hy = hyena_mixer(seq(hy), ghat, hy_conv_w[l], hy_conv_b[l], hy_bias[l])
            y_s5, s5_h = s5_mixer(seq(s5), states[0], *s5_prm, s5_d[l], s5_w_glu[l])
            y_rw, rw_h = rwkv_mixer(seq(rw), states[1], rw_prm)
            y_m2, m2_h = mamba_mixer(seq(m2z), seq(m2x), seq(m2dt), states[2], m2_prm)
            flat = lambda a: None if a is None else a.reshape(b_ * n_tok, W_BR)
            return [flat(y_hy), flat(y_s5), flat(y_rw), flat(y_m2)], (s5_h, rw_h, m2_h)

        zero_states = (jnp.zeros((b_, SUBLANE, 2 * S5_LANES), F32), jnp.zeros((b_, 2, W_BR, W_BR), F32),
                       jnp.zeros((b_, 2, M2_STATE, W_BR), F32))
        ys_c, ctx_states = mixers(ct, lc, csh1, csc1, b_ * lc, tmc, zero_states, ctx_out)
        ys_x, _ = mixers(xt, L, sh1, sc1, L, tm, ctx_states, True)

        w_route = jnp.zeros((D_MODEL, LANE), F32)
        w_route = w_route.at[:, :MOE_EXPERTS].set(moe_w_expert[l].transpose(1, 0, 2).reshape(D_MODEL, MOE_EXPERTS))
        w_route = w_route.at[:, MOE_EXPERTS:MOE_EXPERTS + MOE_GROUPS].set(moe_w_group[l])
        wb, wo = w_branch[l].astype(BF16), w_out[l].astype(BF16)
        wg, wu, wd = moe_w_gate[l].astype(BF16), moe_w_up[l].astype(BF16), moe_w_down[l].astype(BF16)
        nfin = norm_final[None]

        def channel_mix(tokens, ys, mods, rows_per_mod, tile, final):
            s1, c1, gt1, s2, c2, gt2 = mods
            t1 = merge_branches(tokens, nm, s1, c1, gt1, ys, w_gate, wb, wo, rows_per_mod, tile)
            comb = moe_router(t1, nf, s2, c2, w_route, rows_per_mod, tile)
            moe_tile = min(MOE_TILE, rows_per_mod)
            return moe_experts(t1, nf, s2, c2, gt2, comb, wg, wu, wd, nfin, final, rows_per_mod, moe_tile)

        xt = channel_mix(xt, ys_x, (sh1, sc1, g1, sh2, sc2, g2), L, tm, l == depth - 1)
        if ctx_out:
            ct = channel_mix(ct, ys_c, (csh1, csc1, cg1, csh2, csc2, cg2), b_ * lc, tmc, False)
    return xt.reshape(b_, L, D_MODEL)
```

```python
import functools
import math

import numpy as np
import jax
import jax.numpy as jnp
from jax import lax
from jax.experimental import pallas as pl
from jax.experimental.pallas import tpu as pltpu

F32 = jnp.float32
BF16 = jnp.bfloat16
HI = lax.Precision.HIGHEST

D_MODEL = 1024
W_BR = 256
N_BRANCH = 4
N_MOD = 6
NORM_EPS = 1e-6
GRID_W = 64

HY_BANDS = 8
HY_HID = 64
HY_TARGET = 1e-2
HY_FAST_PCT = 0.3
HY_SLOW_PCT = 1.5
HY_BLOCK = 512

S5_GROUP = 16
S5_GROUPS = 16
S5_STATE = 64
S5_LANES = S5_GROUPS * S5_STATE
S5_CHUNK = 256

RW_HEAD = 64
RW_HEADS = 4
RW_W_RANK = 32
RW_A_RANK = 32
RW_G_RANK = 64
RW_GN_EPS = 64e-5
RW_SUB = 64
RW_STEP = 512

M2_HEADDIM = 64
M2_HEADS = 4
M2_GROUPS = 2
M2_STATE = 128
M2_CHUNK = 256

MOE_GROUPS = 4
MOE_PER_GROUP = 4
MOE_EXPERTS = 16
MOE_FF = 512
MOE_GROUP_LANE = MOE_EXPERTS
MOE_TILE = 1024
MOE_SUB = 128

HY_COLS = 3 * W_BR
S5_COLS = W_BR
RW_COLS = 3 * W_BR + 2 * RW_W_RANK + 2 * RW_A_RANK + RW_G_RANK
M2_XBC = W_BR + 2 * M2_GROUPS * M2_STATE
M2_COLS = W_BR + M2_XBC + 2 * M2_HEADS
RW_PAD = 1024
M2_PAD = 1152
LANE = 128
SUBLANE = 8

VMEM_LIMIT = 56 * 1024 * 1024


def _dot(a, b, prec=None):
    return jnp.dot(a, b, preferred_element_type=F32, precision=prec)


def _dot_nt(a, b, prec=None):
    return lax.dot_general(a, b, (((1,), (1,)), ((), ())), preferred_element_type=F32, precision=prec)


def _dot_tn(a, b, prec=None):
    return lax.dot_general(a, b, (((0,), (0,)), ((), ())), preferred_element_type=F32, precision=prec)


def _bdot(a, b):
    return jnp.dot(a.astype(BF16), b.astype(BF16), preferred_element_type=F32)


def _sigmoid(x):
    return 1.0 / (1.0 + jnp.exp(-x))


def _silu(x):
    return x * _sigmoid(x)


def _params(sem):
    return pltpu.CompilerParams(dimension_semantics=sem, vmem_limit_bytes=VMEM_LIMIT)


def _rms(x):
    return x * lax.rsqrt(jnp.mean(x * x, axis=-1, keepdims=True) + NORM_EPS)


def _modulate(x, g, shift, scale):
    return _rms(x) * g * (1.0 + scale) + shift


def _iota(shape, axis):
    return lax.broadcasted_iota(jnp.int32, shape, axis)


def _shift_rows(cur, prev_row, next_row):
    t = cur.shape[0]
    row = _iota(cur.shape, 0)
    xm = jnp.where(row == 0, prev_row, pltpu.roll(cur, 1, 0))
    xp = jnp.where(row == t - 1, next_row, pltpu.roll(cur, t - 1, 0))
    return xm, xp


def _halo_specs(t, n_blocks, width, col_block, chunk_of):
    per = t // SUBLANE
    last = n_blocks * per - 1

    def prev_map(b, s):
        return (b, jnp.maximum(chunk_of(s) * per - 1, 0), col_block)

    def next_map(b, s):
        return (b, jnp.minimum((chunk_of(s) + 1) * per, last), col_block)

    return (pl.BlockSpec((1, SUBLANE, width), prev_map), pl.BlockSpec((1, SUBLANE, width), next_map))


def _halo_rows(prev_ref, next_ref, chunk, n_chunks):
    prev_row = jnp.where(chunk == 0, 0.0, prev_ref[0, SUBLANE - 1:SUBLANE, :])
    next_row = jnp.where(chunk == n_chunks - 1, 0.0, next_ref[0, 0:1, :])
    return prev_row, next_row


def _mod_kernel(c_ref, w_ref, b_ref, o_ref):
    o_ref[0] = _dot(_silu(c_ref[...]), w_ref[0], HI) + b_ref[0]


def adaln_mod(cvec, mod_w, mod_b):
    depth = mod_w.shape[0]
    return pl.pallas_call(
        _mod_kernel,
        name="adaln_mod",
        grid=(depth, N_MOD),
        in_specs=[pl.BlockSpec((SUBLANE, D_MODEL), lambda l, j: (0, 0)),
                  pl.BlockSpec((1, D_MODEL, D_MODEL), lambda l, j: (l, 0, j)),
                  pl.BlockSpec((1, 1, D_MODEL), lambda l, j: (l, 0, j))],
        out_specs=pl.BlockSpec((1, SUBLANE, D_MODEL), lambda l, j: (l, 0, j)),
        out_shape=jax.ShapeDtypeStruct((depth, SUBLANE, N_MOD * D_MODEL), F32),
        compiler_params=_params(("parallel", "parallel")),
    )(cvec, mod_w, mod_b.reshape(depth, 1, N_MOD * D_MODEL))


def _inproj_kernel(x_ref, g_ref, sh_ref, sc_ref, w_ref, *out_refs):
    h = _modulate(x_ref[...], g_ref[...], sh_ref[0], sc_ref[0]).astype(BF16)
    o = 0
    for ref in out_refs:
        n = ref.shape[-1]
        ref[...] = _dot(h, w_ref[:, o:o + n])
        o += n


def in_projection(xt, g, shift, scale, w_mix, rows_per_mod, tm):
    n = xt.shape[0]
    tiles_per_mod = rows_per_mod // tm
    widths = (HY_COLS, S5_COLS, RW_PAD, W_BR, M2_XBC, M2_PAD - W_BR - M2_XBC)
    mod_spec = pl.BlockSpec((1, 1, D_MODEL), lambda i: (i // tiles_per_mod, 0, 0))
    return pl.pallas_call(
        _inproj_kernel,
        name="in_projection",
        grid=(n // tm,),
        in_specs=[pl.BlockSpec((tm, D_MODEL), lambda i: (i, 0)),
                  pl.BlockSpec((1, D_MODEL), lambda i: (0, 0)),
                  mod_spec, mod_spec,
                  pl.BlockSpec(w_mix.shape, lambda i: (0, 0))],
        out_specs=[pl.BlockSpec((tm, w), lambda i: (i, 0)) for w in widths],
        out_shape=[jax.ShapeDtypeStruct((n, w), F32) for w in widths],
        compiler_params=_params(("parallel",)),
    )(xt, g, shift, scale, w_mix)


def _dft_mats(t):
    k = np.arange(t, dtype=np.float64)[:, None]
    n = np.arange(2 * t, dtype=np.float64)[None, :]
    ang = np.pi * (2.0 * k + 1.0) * n / (2.0 * t)
    fwd = np.concatenate([np.cos(ang), -np.sin(ang)], axis=0)
    inv = fwd[:, :t].T / t
    return fwd.astype(np.float32), inv.astype(np.float32)


def _hyfilt_kernel(w1_ref, b1_ref, w2_ref, b2_ref, w3_ref, fr_ref, o_ref):
    L = o_ref.shape[0]
    shape = (L, LANE)
    pos = _iota(shape, 0).astype(F32)
    lane = _iota(shape, 1)
    t = pos / (L - 1)
    w = (2.0 * math.pi / L) * pos
    band = jnp.where(lane >= 1 + HY_BANDS, lane - 1 - HY_BANDS, lane - 1).astype(F32)
    f = 1e-4 + band * ((HY_BANDS - 1 - 1e-4) / (HY_BANDS - 1))
    arg = f * w
    feats = jnp.where(lane == 0, t,
                      jnp.where(lane <= HY_BANDS, jnp.cos(arg),
                                jnp.where(lane <= 2 * HY_BANDS, -jnp.sin(arg), 0.0)))
    fr = fr_ref[...]
    h = jnp.sin(fr * (_dot(feats, w1_ref[...], HI) + b1_ref[...]))
    h = jnp.sin(fr * (_dot(h, w2_ref[...], HI) + b2_ref[...]))
    h = _dot(h, w3_ref[...], HI)
    ch = _iota((L, W_BR), 1).astype(F32)
    lo = math.log(HY_TARGET) / HY_SLOW_PCT
    hi = math.log(HY_TARGET) / HY_FAST_PCT
    deltas = jnp.abs(lo + ch * ((hi - lo) / (W_BR - 1)))
    decay = jnp.exp(-(t[:, 0:1]) * deltas)
    o_ref[:, 0:W_BR] = h[:, 0:W_BR] * decay
    o_ref[:, W_BR:2 * W_BR] = h[:, W_BR:2 * W_BR] * decay


def hyena_filter(L, f_w1, f_b1, f_w2, f_b2, f_w3, f_freq):
    pad = lambda a, r, c: jnp.zeros((r, c), F32).at[:a.shape[0], :a.shape[1]].set(a)
    args = (pad(f_w1, LANE, LANE), pad(f_b1[None], 1, LANE), pad(f_w2, LANE, LANE), pad(f_b2[None], 1, LANE),
            pad(f_w3, LANE, 2 * W_BR), pad(f_freq[None], 1, LANE))
    return pl.pallas_call(
        _hyfilt_kernel,
        name="hyena_filter",
        out_shape=jax.ShapeDtypeStruct((L, 2 * W_BR), F32),
        compiler_params=_params(None),
    )(*args)


def _ghat_kernel(f_ref, g_ref, o_ref):
    o_ref[0] = _dot(f_ref[...], g_ref[0], HI)


def hyena_filter_spectra(filt, L, t):
    nb = L // t
    table = jnp.concatenate([filt[:, :W_BR], filt[:, W_BR:], jnp.zeros((1, W_BR), F32)], axis=0)
    m = np.arange(2 * t)[None, :]
    d = np.arange(-(nb - 1), nb)[:, None]
    off = d * t + np.where(m < t, m, m - 2 * t)
    idx = np.where(m == t, 2 * L, np.where(off >= 0, off, L - off)).astype(np.int32)
    sign = np.where(m == t, 0.0, np.where(m < t, 1.0, -1.0)).astype(np.float32) * np.ones_like(off, np.float32)
    g = table[idx] * sign[:, :, None]
    fwd, _ = _dft_mats(t)
    return pl.pallas_call(
        _ghat_kernel,
        name="hyena_filter_spectra",
        grid=(2 * nb - 1,),
        in_specs=[pl.BlockSpec((2 * t, 2 * t), lambda d: (0, 0)),
                  pl.BlockSpec((1, 2 * t, W_BR), lambda d: (d, 0, 0))],
        out_specs=pl.BlockSpec((1, 2 * t, W_BR), lambda d: (d, 0, 0)),
        out_shape=jax.ShapeDtypeStruct((2 * nb - 1, 2 * t, W_BR), F32),
        compiler_params=_params(("parallel",)),
    )(jnp.asarray(fwd), g)


def _hypre_kernel(nb, cur_ref, prev_ref, next_ref, cw_ref, cb_ref, f_ref, x0_ref, u_ref, uh_ref):
    j = pl.program_id(1)
    cur = cur_ref[0]
    prev_row, next_row = _halo_rows(prev_ref, next_ref, j, nb)
    xm, xp = _shift_rows(cur, prev_row, next_row)
    pc = cb_ref[...] + xm * cw_ref[0:1, :] + cur * cw_ref[1:2, :] + xp * cw_ref[2:3, :]
    u = pc[:, W_BR:2 * W_BR] * pc[:, 2 * W_BR:3 * W_BR]
    x0_ref[0] = pc[:, 0:W_BR]
    u_ref[0] = u
    uh_ref[0, 0] = _dot(f_ref[...], u, HI)


def _hymain_kernel(nb, t, g_ref, uh_ref, x0_ref, u_ref, bias_ref, inv_ref, o_ref):
    i = pl.program_id(2)
    acc_re = jnp.zeros((t, LANE), F32)
    acc_im = jnp.zeros((t, LANE), F32)
    for j in range(nb):
        g = g_ref[i - j + nb - 1]
        gr, gi = g[0:t], g[t:2 * t]
        ur, ui = uh_ref[0, j, 0:t, :], uh_ref[0, j, t:2 * t, :]
        acc_re += gr * ur - gi * ui
        acc_im += gr * ui + gi * ur
    y = _dot(inv_ref[...], jnp.concatenate([acc_re, acc_im], axis=0), HI)
    o_ref[0] = x0_ref[0] * (y + bias_ref[...] * u_ref[0])


def hyena_mixer(hy, ghat, conv_w, conv_b, h_bias):
    b_, L, _ = hy.shape
    t = min(HY_BLOCK, L)
    nb = L // t
    fwd, inv = _dft_mats(t)
    prev_spec, next_spec = _halo_specs(t, nb, HY_COLS, 0, lambda s: s)
    x0, u, uh = pl.pallas_call(
        functools.partial(_hypre_kernel, nb),
        name="hyena_conv_dft",
        grid=(b_, nb),
        in_specs=[pl.BlockSpec((1, t, HY_COLS), lambda b, j: (b, j, 0)), prev_spec, next_spec,
                  pl.BlockSpec((3, HY_COLS), lambda b, j: (0, 0)),
                  pl.BlockSpec((1, HY_COLS), lambda b, j: (0, 0)),
                  pl.BlockSpec((2 * t, t), lambda b, j: (0, 0))],
        out_specs=[pl.BlockSpec((1, t, W_BR), lambda b, j: (b, j, 0)),
                   pl.BlockSpec((1, t, W_BR), lambda b, j: (b, j, 0)),
                   pl.BlockSpec((1, 1, 2 * t, W_BR), lambda b, j: (b, j, 0, 0))],
        out_shape=[jax.ShapeDtypeStruct((b_, L, W_BR), F32), jax.ShapeDtypeStruct((b_, L, W_BR), F32),
                   jax.ShapeDtypeStruct((b_, nb, 2 * t, W_BR), F32)],
        compiler_params=_params(("parallel", "parallel")),
    )(hy, hy, hy, conv_w, conv_b[None], jnp.asarray(fwd[:, :t]))
    ncb = W_BR // LANE
    return pl.pallas_call(
        functools.partial(_hymain_kernel, nb, t),
        name="hyena_longconv",
        grid=(ncb, b_, nb),
        in_specs=[pl.BlockSpec((2 * nb - 1, 2 * t, LANE), lambda c, b, i: (0, 0, c)),
                  pl.BlockSpec((1, nb, 2 * t, LANE), lambda c, b, i: (b, 0, 0, c)),
                  pl.BlockSpec((1, t, LANE), lambda c, b, i: (b, i, c)),
                  pl.BlockSpec((1, t, LANE), lambda c, b, i: (b, i, c)),
                  pl.BlockSpec((1, LANE), lambda c, b, i: (0, c)),
                  pl.BlockSpec((t, 2 * t), lambda c, b, i: (0, 0))],
        out_specs=pl.BlockSpec((1, t, LANE), lambda c, b, i: (b, i, c)),
        out_shape=jax.ShapeDtypeStruct((b_, L, W_BR), F32),
        compiler_params=_params(("parallel", "parallel", "parallel")),
    )(ghat, uh, x0, u, h_bias[None], jnp.asarray(inv))


def _s5_kernel(nc, u_ref, h0_ref, bb_ref, ap_ref, cb_ref, dsk_ref, wg_ref, y_ref, hf_ref, yacc, state):
    s = pl.program_id(1)
    t = u_ref.shape[1]
    n = S5_LANES

    @pl.when(s == 0)
    def _():
        hf_ref[...] = jnp.zeros_like(hf_ref)

    def phase(d):
        chunk = s if d == 0 else 2 * nc - 1 - s
        first = (s == 0) if d == 0 else (s == nc)
        last = (s == nc - 1) if d == 0 else (s == 2 * nc - 1)

        @pl.when(first)
        def _():
            state[0:1, :] = h0_ref[0, d:d + 1, :]

        u = u_ref[0]
        x = _dot(u.astype(BF16), bb_ref[d])
        row = _iota((t, n), 0)
        st = state[0:1, :]
        pw = (lambda p: p - 1) if d == 0 else (lambda p: SUBLANE - p)
        a_re, a_im = ap_ref[d, pw(1):pw(1) + 1, 0:n], ap_ref[d, pw(1):pw(1) + 1, n:2 * n]
        c_re = a_re * st[:, 0:n] - a_im * st[:, n:2 * n]
        c_im = a_re * st[:, n:2 * n] + a_im * st[:, 0:n]
        entry = 0 if d == 0 else t - 1
        x_re = x[:, 0:n] + jnp.where(row == entry, c_re, 0.0)
        x_im = x[:, n:2 * n] + jnp.where(row == entry, c_im, 0.0)
        n_grp = t // SUBLANE
        x_re, x_im = x_re.reshape(n_grp, SUBLANE, n), x_im.reshape(n_grp, SUBLANE, n)
        sub = _iota((SUBLANE, n), 0)
        for sh in (1, 2, 4):
            valid = (sub >= sh) if d == 0 else (sub < SUBLANE - sh)
            a_re = jnp.where(valid, ap_ref[d, pw(sh):pw(sh) + 1, 0:n], 0.0)[None]
            a_im = jnp.where(valid, ap_ref[d, pw(sh):pw(sh) + 1, n:2 * n], 0.0)[None]
            rot = sh if d == 0 else SUBLANE - sh
            s_re, s_im = pltpu.roll(x_re, rot, 1), pltpu.roll(x_im, rot, 1)
            x_re, x_im = x_re + a_re * s_re - a_im * s_im, x_im + a_re * s_im + a_im * s_re
        p_re, p_im = ap_ref[d, :, 0:n], ap_ref[d, :, n:2 * n]
        out_re, out_im = [None] * n_grp, [None] * n_grp
        c_re = c_im = None
        for g in (range(n_grp) if d == 0 else range(n_grp - 1, -1, -1)):
            g_re, g_im = x_re[g], x_im[g]
            if c_re is not None:
                g_re, g_im = g_re + p_re * c_re - p_im * c_im, g_im + p_re * c_im + p_im * c_re
            out_re[g], out_im[g] = g_re, g_im
            edge = SUBLANE - 1 if d == 0 else 0
            c_re, c_im = g_re[edge:edge + 1], g_im[edge:edge + 1]
        x_re, x_im = jnp.concatenate(out_re, axis=0), jnp.concatenate(out_im, axis=0)
        ex = t - 1 if d == 0 else 0
        state[0:1, 0:n] = x_re[ex:ex + 1, :]
        state[0:1, n:2 * n] = x_im[ex:ex + 1, :]
        yd = _dot(x_re.astype(BF16), cb_ref[d, 0:n, :]) + _dot(x_im.astype(BF16), cb_ref[d, n:2 * n, :])
        rows = pl.ds(pl.multiple_of(chunk * t, t), t)
        if d == 0:
            yacc[rows, :] = dsk_ref[...] * u + yd
        else:
            y = yacc[rows, :] + yd
            y = 0.5 * y * (1.0 + jnp.tanh(math.sqrt(2.0 / math.pi) * (y + 0.044715 * y * y * y)))
            y_ref[0] = y * _sigmoid(_bdot(y, wg_ref[...]))

        @pl.when(last)
        def _():
            hf_ref[0, d:d + 1, :] = state[0:1, :]

    @pl.when(s < nc)
    def _():
        phase(0)

    @pl.when(s >= nc)
    def _():
        phase(1)


def _two_pass_chunk(nc):
    return lambda s: jnp.where(s < nc, s, 2 * nc - 1 - s)


def _two_pass_out(nc):
    return lambda s: jnp.where(s < nc, nc - 1, 2 * nc - 1 - s)


def s5_prepare(lam_re, lam_im, log_step, b_re, b_im, c_re, c_im):
    step = jnp.exp(log_step)[:, :, None]
    mag = jnp.exp(lam_re * step)
    lb_re, lb_im = mag * jnp.cos(lam_im * step), mag * jnp.sin(lam_im * step)
    den = lam_re * lam_re + lam_im * lam_im
    q_re = ((lb_re - 1.0) * lam_re + lb_im * lam_im) / den
    q_im = (lb_im * lam_re - (lb_re - 1.0) * lam_im) / den
    bb_re = q_re[..., None] * b_re - q_im[..., None] * b_im
    bb_im = q_re[..., None] * b_im + q_im[..., None] * b_re
    eye = jnp.eye(S5_GROUPS, dtype=F32)

    def blockdiag_in(m):
        return jnp.einsum('dgni,gh->dgihn', m, eye).reshape(2, W_BR, S5_LANES)

    def blockdiag_out(m):
        return jnp.einsum('dgin,gh->dgnhi', m, eye).reshape(2, S5_LANES, W_BR)

    bblk = jnp.concatenate([blockdiag_in(bb_re), blockdiag_in(bb_im)], axis=-1)
    cblk = jnp.concatenate([blockdiag_out(c_re), -blockdiag_out(c_im)], axis=1)
    j = np.arange(SUBLANE, dtype=np.float32)
    pw = jnp.asarray(np.stack([j + 1.0, SUBLANE - j]))[:, :, None]
    arg_re = (lam_re * step).reshape(2, 1, S5_LANES) * pw
    arg_im = (lam_im * step).reshape(2, 1, S5_LANES) * pw
    apow = jnp.concatenate([jnp.exp(arg_re) * jnp.cos(arg_im), jnp.exp(arg_re) * jnp.sin(arg_im)], axis=-1)
    return bblk.astype(BF16), apow, cblk.astype(BF16)


def s5_mixer(u, h0, bblk, apow, cblk, d_skip, w_glu):
    b_, L, _ = u.shape
    t = min(S5_CHUNK, L)
    nc = L // t
    chunk_of, out_of = _two_pass_chunk(nc), _two_pass_out(nc)
    const = lambda a: pl.BlockSpec(a.shape, lambda b, s: (0,) * a.ndim)
    dsk = d_skip[None]
    wg = w_glu.astype(BF16)
    return pl.pallas_call(
        functools.partial(_s5_kernel, nc),
        name="s5_mixer",
        grid=(b_, 2 * nc),
        in_specs=[pl.BlockSpec((1, t, W_BR), lambda b, s: (b, chunk_of(s), 0)),
                  pl.BlockSpec((1, SUBLANE, 2 * S5_LANES), lambda b, s: (b, 0, 0)),
                  const(bblk), const(apow), const(cblk), const(dsk), const(wg)],
        out_specs=[pl.BlockSpec((1, t, W_BR), lambda b, s: (b, out_of(s), 0)),
                   pl.BlockSpec((1, SUBLANE, 2 * S5_LANES), lambda b, s: (b, 0, 0))],
        out_shape=[jax.ShapeDtypeStruct((b_, L, W_BR), F32),
                   jax.ShapeDtypeStruct((b_, SUBLANE, 2 * S5_LANES), F32)],
        scratch_shapes=[pltpu.VMEM((L, W_BR), F32), pltpu.VMEM((SUBLANE, 2 * S5_LANES), F32)],
        compiler_params=_params(("parallel", "arbitrary")),
    )(u, h0, bblk, apow, cblk, dsk, wg)


def _head_mask(h, shape, axis):
    lane = _iota(shape, axis)
    return (lane >= h * RW_HEAD) & (lane < (h + 1) * RW_HEAD)


def _split3(x):
    hi = x.astype(BF16)
    r1 = x - hi.astype(F32)
    mid = r1.astype(BF16)
    return hi, mid, (r1 - mid.astype(F32)).astype(BF16)


def _stack_heads(x):
    xb = x.astype(BF16)
    zero = jnp.zeros_like(xb)
    return jnp.concatenate([jnp.where(_head_mask(h, xb.shape, 1), xb, zero) for h in range(RW_HEADS)], axis=0)


def _split2(x):
    hi = x.astype(BF16)
    return hi, (x - hi.astype(F32)).astype(BF16)


def _dot_exact_rhs(a, b):
    hi, lo = _split2(a)
    return _dot(hi, b) + _dot(lo, b)


def _dot_split(a, b_hi, b_lo):
    hi, lo = _split2(a)
    return _dot(hi, b_hi) + _dot(hi, b_lo) + _dot(lo, b_hi)


RW_INV_BASE = 8


def _cat_to_blockdiag(x):
    t, n = x.shape
    xb = x.astype(BF16)
    tiled = jnp.concatenate([xb] * (n // t), axis=0)
    same = (_iota((n, n), 0) // t) == (_iota((n, n), 1) // t)
    return jnp.where(same, tiled, jnp.zeros_like(tiled))


def _tri_inverse(nmat):
    t = nmat[0].shape[0]
    row, col = _iota(nmat[0].shape, 0), _iota(nmat[0].shape, 1) % t
    same = lambda s: (row // s) == (col // s)
    mul = lambda x, y: _dot(x.astype(BF16), _cat_to_blockdiag(y))
    base = [jnp.where(same(RW_INV_BASE), x, 0.0) for x in nmat]
    eye = jnp.where(row == col, 1.0, 0.0)
    m = [eye - x for x in base]
    p = base
    for _ in range(RW_INV_BASE.bit_length() - 2):
        p = [mul(x, x) for x in p]
        m = [x + mul(x, y) for x, y in zip(m, p)]
    s = 2 * RW_INV_BASE
    while s <= t:
        off = same(s) & jnp.logical_not(same(s // 2))
        mc = [mul(x, jnp.where(off, y, 0.0)) for x, y in zip(m, nmat)]
        m = [x - mul(y, x) for x, y in zip(m, mc)]
        s *= 2
    return m


def _wkv_local(d, r, logw, kd, v, kk, a):
    t = r[0].shape[0]
    bf = lambda x: x.astype(BF16)
    each = lambda f, *ls: [f(*xs) for xs in zip(*ls)]
    ri, ci = _iota((t, t), 0), _iota((t, t), 1)
    tri = ((ci <= ri) if d == 0 else (ci >= ri)).astype(BF16)
    row, col = _iota((t, RW_HEADS * t), 0), _iota((t, RW_HEADS * t), 1) % t
    incl = (col <= row) if d == 0 else (col >= row)
    strict = (col < row) if d == 0 else (col > row)
    dn = W_BR
    diag = _iota((dn, dn), 0) == _iota((dn, dn), 1)

    cs = each(lambda x: sum(_dot(tri, part) for part in _split3(x)), logw)
    g_tot = each(lambda x: jnp.exp(jnp.sum(x, axis=0, keepdims=True)), logw)
    g_inv = each(lambda c: jnp.exp(-c), cs)
    bt = each(lambda k_, a_, g: k_ * a_ * g, kk, a, g_inv)
    kt = each(lambda k_, g: k_ * g, kd, g_inv)
    qa = each(lambda k_, c, w: k_ * jnp.exp(c - w), kk, cs, logw)
    rh = each(lambda r_, c: r_ * jnp.exp(c), r, cs)
    bts, kts, vs = each(_stack_heads, bt), each(_stack_heads, kt), each(_stack_heads, v)
    bhs = each(lambda x, g: _stack_heads(x * g), bt, g_tot)
    khs = each(lambda x, g: _stack_heads(x * g), kt, g_tot)
    a_ab = each(lambda q, b: jnp.where(strict, _dot_nt(bf(q), b), 0.0), qa, bts)
    a_ak = each(lambda q, k_: bf(jnp.where(strict, _dot_nt(bf(q), k_), 0.0)), qa, kts)
    a_rb = each(lambda q, b: bf(jnp.where(incl, _dot_nt(bf(q), b), 0.0)), rh, bts)
    a_rk = each(lambda q, k_: bf(jnp.where(incl, _dot_nt(bf(q), k_), 0.0)), rh, kts)
    m = each(bf, _tri_inverse(a_ab))
    x1 = each(lambda a_, v_: _stack_heads(_dot(a_, v_)), a_ak, vs)
    ul = each(lambda m_, x: _stack_heads(-_dot(m_, x)), m, x1)
    w0 = each(lambda m_, q: _stack_heads(_dot(m_, _stack_heads(q))), m, qa)
    yloc = each(lambda ak, v_, ab, u: _dot(ak, v_) + _dot(ab, u), a_rk, vs, a_rb, ul)
    yh = each(lambda r_, ab, w: r_ - _dot(ab, w), rh, a_rb, w0)
    p_bd = each(lambda g, b, w: jnp.where(diag, g, 0.0) - _dot_tn(b, w), g_tot, bhs, w0)
    q_bd = each(lambda b, u, k_, v_: _dot_tn(b, u) + _dot_tn(k_, v_), bhs, ul, khs, vs)
    return list(zip(yloc, yh, p_bd, q_bd))


def _rwkv_kernel(ng, cur_ref, prev_ref, next_ref, h0_ref, mu_ref, w0_ref, a0_ref, lrh_ref, lrl_ref,
                 kk_ref, ka_ref, rk_ref, lnw_ref, lnb_ref, y_ref, hf_ref, yacc, hst):
    s = pl.program_id(1)
    t = cur_ref.shape[1]
    n = W_BR
    ri, ci = _iota((n, n), 0), _iota((n, n), 1)
    head_sum = ((ri // RW_HEAD) == (ci // RW_HEAD)).astype(BF16)

    def phase(d):
        chunk = s if d == 0 else 2 * ng - 1 - s
        first = (s == 0) if d == 0 else (s == ng)
        last = (s == ng - 1) if d == 0 else (s == 2 * ng - 1)

        @pl.when(first)
        def _():
            hst[...] = h0_ref[0, d]

        cur = cur_ref[0]
        prev_row, next_row = _halo_rows(prev_ref, next_ref, chunk, ng)
        xm, xp = _shift_rows(cur, prev_row, next_row)
        p = cur + (0.5 * (xm + xp) - cur) * mu_ref[...]
        r, k, v, lr = p[:, 0:n], p[:, n:2 * n], p[:, 2 * n:3 * n], p[:, 3 * n:4 * n]
        kk = k * kk_ref[...]
        kk = kk * lax.rsqrt(jnp.maximum(_dot_exact_rhs(kk * kk, head_sum), 1e-24))
        lane = _iota(lr.shape, 1)
        feats = jnp.where(lane < 2 * RW_W_RANK, jnp.tanh(lr),
                          jnp.where(lane < 2 * (RW_W_RANK + RW_A_RANK), lr, _sigmoid(lr)))
        cols = (2 if d == 0 else 3) * n
        low = _dot_split(feats, lrh_ref[d, :, 0:cols], lrl_ref[d, :, 0:cols])
        w = w0_ref[d:d + 1, :] + low[:, 0:n]
        logw = -math.exp(-0.5) * _sigmoid(w)
        a = _sigmoid(a0_ref[d:d + 1, :] + low[:, n:2 * n])
        kd = k * (1.0 + (a - 1.0) * ka_ref[...])
        nsub = t // RW_SUB
        subs = lambda x: [x[q * RW_SUB:(q + 1) * RW_SUB] for q in range(nsub)]
        local = _wkv_local(d, subs(r), subs(logw), subs(kd), subs(v), subs(kk), subs(a))
        h = hst[...]
        ys = [None] * nsub
        for q in (range(nsub) if d == 0 else range(nsub - 1, -1, -1)):
            yloc, yh, p_bd, q_bd = local[q]
            hb = h.astype(BF16)
            ys[q] = yloc + _dot(yh.astype(BF16), hb)
            h = _dot(p_bd.astype(BF16), hb) + q_bd
        hst[...] = h
        yd = jnp.concatenate(ys, axis=0)
        rows = pl.ds(pl.multiple_of(chunk * t, t), t)
        if d == 0:
            yacc[rows, :] = yd
        else:
            y = yacc[rows, :] + yd
            mean = _dot_exact_rhs(y, head_sum) * (1.0 / RW_HEAD)
            yc = y - mean
            var = _dot_exact_rhs(yc * yc, head_sum) * (1.0 / RW_HEAD)
            y = yc * lax.rsqrt(var + RW_GN_EPS) * lnw_ref[...] + lnb_ref[...]
            bonus = _dot_exact_rhs(r * k * rk_ref[...], head_sum) * v
            y_ref[0] = (y + bonus) * low[:, 2 * n:3 * n]

        @pl.when(last)
        def _():
            hf_ref[0, d] = hst[...]

    @pl.when(s < ng)
    def _():
        phase(0)

    @pl.when(s >= ng)
    def _():
        phase(1)


def rwkv_prepare(mu, w0, w2, a0, a2, g2, k_k, k_a, r_k, ln_w, ln_b):
    n = W_BR
    mu_p = jnp.zeros((1, RW_PAD), F32).at[0, :RW_COLS].set(mu)
    w2_p = jnp.zeros((2, n, n), F32)
    a2_p = jnp.zeros((2, n, n), F32)
    for d in range(2):
        w2_p = w2_p.at[d, d * RW_W_RANK:(d + 1) * RW_W_RANK].set(w2[d])
        o = 2 * RW_W_RANK + d * RW_A_RANK
        a2_p = a2_p.at[d, o:o + RW_A_RANK].set(a2[d])
    o = 2 * RW_W_RANK + 2 * RW_A_RANK
    g2_p = jnp.zeros((n, n), F32).at[o:o + RW_G_RANK].set(g2)
    low = jnp.concatenate([w2_p, a2_p, jnp.broadcast_to(g2_p, (2, n, n))], axis=-1)
    low_hi = low.astype(BF16)
    low_lo = (low - low_hi.astype(F32)).astype(BF16)
    return (mu_p, w0, a0, low_hi, low_lo, k_k[None], k_a[None], r_k.reshape(1, n), ln_w[None], ln_b[None])


def rwkv_mixer(rw, h0, prm):
    b_, L, _ = rw.shape
    t = min(RW_STEP, L)
    ng = L // t
    chunk_of, out_of = _two_pass_chunk(ng), _two_pass_out(ng)
    prev_spec, next_spec = _halo_specs(t, ng, RW_PAD, 0, chunk_of)
    const = lambda a: pl.BlockSpec(a.shape, lambda b, s: (0,) * a.ndim)
    st_spec = pl.BlockSpec((1, 2, W_BR, W_BR), lambda b, s: (b, 0, 0, 0))
    return pl.pallas_call(
        functools.partial(_rwkv_kernel, ng),
        name="rwkv_mixer",
        grid=(b_, 2 * ng),
        in_specs=[pl.BlockSpec((1, t, RW_PAD), lambda b, s: (b, chunk_of(s), 0)), prev_spec, next_spec, st_spec]
                 + [const(a) for a in prm],
        out_specs=[pl.BlockSpec((1, t, W_BR), lambda b, s: (b, out_of(s), 0)), st_spec],
        out_shape=[jax.ShapeDtypeStruct((b_, L, W_BR), F32), jax.ShapeDtypeStruct((b_, 2, W_BR, W_BR), F32)],
        scratch_shapes=[pltpu.VMEM((L, W_BR), F32), pltpu.VMEM((W_BR, W_BR), F32)],
        compiler_params=_params(("parallel", "arbitrary")),
    )(rw, rw, rw, h0, *prm)


def _m2_kernel(nc, z_ref, cur_ref, prev_ref, next_ref, dt_ref, h0_ref, cw_ref, cb_ref, ex_ref, a_ref, dtb_ref,
               dsk_ref, nw_ref, y_ref, hf_ref, yacc, hst):
    s = pl.program_id(1)
    t = cur_ref.shape[1]
    n = W_BR
    ns = M2_STATE
    ri, ci = _iota((t, t), 0), _iota((t, t), 1)

    def phase(d):
        chunk = s if d == 0 else 2 * nc - 1 - s
        first = (s == 0) if d == 0 else (s == nc)
        last = (s == nc - 1) if d == 0 else (s == 2 * nc - 1)

        @pl.when(first)
        def _():
            hst[...] = h0_ref[0, d]

        cur = cur_ref[0]
        prev_row, next_row = _halo_rows(prev_ref, next_ref, chunk, nc)
        xm, xp = _shift_rows(cur, prev_row, next_row)
        xbc = _silu(cb_ref[...] + xm * cw_ref[0:1, :] + cur * cw_ref[1:2, :] + xp * cw_ref[2:3, :])
        xs, bm, cm = xbc[:, 0:n], xbc[:, n:2 * n], xbc[:, 2 * n:3 * n]
        raw = sum(_dot(part, ex_ref[d]) for part in _split3(dt_ref[0])) + dtb_ref[d:d + 1, :]
        dtd = jnp.maximum(raw, 0.0) + jnp.log(1.0 + jnp.exp(-jnp.abs(raw)))
        da = dtd * a_ref[d:d + 1, :]
        incl = (ci <= ri) if d == 0 else (ci >= ri)
        tri = incl.astype(BF16)
        da_parts = _split3(da)
        acs = sum(_dot(tri, part) for part in da_parts)
        tot = jnp.sum(da, axis=0, keepdims=True)
        xdt = xs * dtd
        xdt_b = xdt.astype(BF16)
        bm_b, cm_b = bm.astype(BF16), cm.astype(BF16)
        sel = jnp.where(_iota((SUBLANE, n), 1) // M2_HEADDIM == _iota((SUBLANE, n), 0), 1.0 / M2_HEADDIM,
                        0.0).astype(BF16)
        acs_t = sum(_dot_nt(sel, part) for part in _split3(acs))
        ydiag = jnp.zeros((t, n), F32)
        for g in range(M2_GROUPS):
            cb = _dot_nt(cm_b[:, g * ns:(g + 1) * ns], bm_b[:, g * ns:(g + 1) * ns])
            for h in range(g * (M2_HEADS // M2_GROUPS), (g + 1) * (M2_HEADS // M2_GROUPS)):
                seg = acs[:, h * M2_HEADDIM:h * M2_HEADDIM + 1] - acs_t[h:h + 1, :]
                scores = cb * jnp.exp(jnp.where(incl, seg, -jnp.inf))
                ydiag = jnp.where(_head_mask(h, (t, n), 1), _dot(scores.astype(BF16), xdt_b), ydiag)
        h_in = hst[...]
        h_b = h_in.astype(BF16)
        lane = _iota((t, n), 1)
        yoff = jnp.where(lane < n // M2_GROUPS, _dot(cm_b[:, 0:ns], h_b), _dot(cm_b[:, ns:2 * ns], h_b))
        yd = ydiag + jnp.exp(acs) * yoff
        xdec = (xdt * jnp.exp(tot - acs)).astype(BF16)
        lane_s = _iota((ns, n), 1)
        new = jnp.where(lane_s < n // M2_GROUPS, _dot_tn(bm_b[:, 0:ns], xdec), _dot_tn(bm_b[:, ns:2 * ns], xdec))
        hst[...] = h_in * jnp.exp(tot) + new
        rows = pl.ds(pl.multiple_of(chunk * t, t), t)
        if d == 0:
            yacc[rows, :] = yd + dsk_ref[...] * xs
        else:
            y = (yacc[rows, :] + yd) * _silu(z_ref[0])
            y_ref[0] = _rms(y) * nw_ref[...]

        @pl.when(last)
        def _():
            hf_ref[0, d] = hst[...]

    @pl.when(s < nc)
    def _():
        phase(0)

    @pl.when(s >= nc)
    def _():
        phase(1)


def mamba_prepare(conv_w, conv_b, a_log, dt_bias, d_skip, norm_w):
    n = W_BR
    rep = lambda v: jnp.repeat(v, M2_HEADDIM, axis=-1)
    expand = jnp.zeros((2, LANE, n), F32)
    for d in range(2):
        for h in range(M2_HEADS):
            expand = expand.at[d, d * M2_HEADS + h, h * M2_HEADDIM:(h + 1) * M2_HEADDIM].set(1.0)
    return (conv_w, conv_b[None], expand.astype(BF16), rep(-jnp.exp(a_log)), rep(dt_bias), rep(d_skip)[None],
            norm_w[None])


def mamba_mixer(z, xbc, dt, h0, prm):
    b_, L, _ = z.shape
    t = min(M2_CHUNK, L)
    nc = L // t
    chunk_of, out_of = _two_pass_chunk(nc), _two_pass_out(nc)
    prev_spec, next_spec = _halo_specs(t, nc, M2_XBC, 0, chunk_of)
    const = lambda a: pl.BlockSpec(a.shape, lambda b, s: (0,) * a.ndim)
    st_spec = pl.BlockSpec((1, 2, M2_STATE, W_BR), lambda b, s: (b, 0, 0, 0))
    seq = lambda w: pl.BlockSpec((1, t, w), lambda b, s: (b, chunk_of(s), 0))
    return pl.pallas_call(
        functools.partial(_m2_kernel, nc),
        name="mamba_mixer",
        grid=(b_, 2 * nc),
        in_specs=[seq(W_BR), seq(M2_XBC), prev_spec, next_spec, seq(LANE), st_spec] + [const(a) for a in prm],
        out_specs=[pl.BlockSpec((1, t, W_BR), lambda b, s: (b, out_of(s), 0)), st_spec],
        out_shape=[jax.ShapeDtypeStruct((b_, L, W_BR), F32),
                   jax.ShapeDtypeStruct((b_, 2, M2_STATE, W_BR), F32)],
        scratch_shapes=[pltpu.VMEM((L, W_BR), F32), pltpu.VMEM((M2_STATE, W_BR), F32)],
        compiler_params=_params(("parallel", "arbitrary")),
    )(z, xbc, xbc, xbc, dt, h0, *prm)


def _merge_kernel(x_ref, g_ref, sh_ref, sc_ref, gt_ref, y0_ref, y1_ref, y2_ref, y3_ref, wg_ref, wb_ref, wo_ref,
                  o_ref):
    x = x_ref[...]
    h = _modulate(x, g_ref[...], sh_ref[0], sc_ref[0]).astype(BF16)
    acc = jnp.zeros(x.shape, F32)
    for i, y_ref in enumerate((y0_ref, y1_ref, y2_ref, y3_ref)):
        gate = _sigmoid(_dot(h, wg_ref[i]))
        acc += gate * _dot(y_ref[...].astype(BF16), wb_ref[i])
    o_ref[...] = x + gt_ref[0] * _dot(acc.astype(BF16), wo_ref[...])


def merge_branches(xt, g, shift, scale, gate, ys, w_gate, w_branch, w_out, rows_per_mod, tm):
    n = xt.shape[0]
    tiles_per_mod = rows_per_mod // tm
    mod_spec = pl.BlockSpec((1, 1, D_MODEL), lambda i: (i // tiles_per_mod, 0, 0))
    const = lambda a: pl.BlockSpec(a.shape, lambda i: (0,) * a.ndim)
    tok = lambda w: pl.BlockSpec((tm, w), lambda i: (i, 0))
    return pl.pallas_call(
        _merge_kernel,
        name="merge_branches",
        grid=(n // tm,),
        in_specs=[tok(D_MODEL), const(g), mod_spec, mod_spec, mod_spec] + [tok(W_BR)] * N_BRANCH
                 + [const(w_gate), const(w_branch), const(w_out)],
        out_specs=tok(D_MODEL),
        out_shape=jax.ShapeDtypeStruct((n, D_MODEL), F32),
        compiler_params=_params(("parallel",)),
    )(xt, g, shift, scale, gate, *ys, w_gate, w_branch, w_out)


def _router_kernel(x_ref, g_ref, sh_ref, sc_ref, wr_ref, comb_ref):
    u = _modulate(x_ref[...], g_ref[...], sh_ref[0], sc_ref[0])
    logits = _dot(u, wr_ref[...], HI)
    lane = _iota(logits.shape, 1).astype(F32)
    neg = -jnp.inf
    is_grp = (lane >= MOE_EXPERTS) & (lane < MOE_EXPERTS + MOE_GROUPS)
    gl = jnp.where(is_grp, logits, neg)
    gmax = jnp.max(gl, axis=-1, keepdims=True)
    grp_p = 1.0 / jnp.sum(jnp.exp(gl - gmax), axis=-1, keepdims=True)
    grp_idx = jnp.min(jnp.where(gl == gmax, lane, 4.0 * LANE), axis=-1, keepdims=True) - MOE_EXPERTS
    in_grp = (lane >= grp_idx * MOE_PER_GROUP) & (lane < (grp_idx + 1) * MOE_PER_GROUP)
    el = jnp.where(in_grp, logits, neg)
    v1 = jnp.max(el, axis=-1, keepdims=True)
    i1 = jnp.min(jnp.where(el == v1, lane, 4.0 * LANE), axis=-1, keepdims=True)
    el2 = jnp.where(lane == i1, neg, el)
    v2 = jnp.max(el2, axis=-1, keepdims=True)
    i2 = jnp.min(jnp.where(el2 == v2, lane, 4.0 * LANE), axis=-1, keepdims=True)
    e2 = jnp.exp(v2 - v1)
    w1 = grp_p / (1.0 + e2)
    comb = jnp.where(lane == i1, w1, jnp.where(lane == i2, w1 * e2, 0.0))
    comb_ref[...] = jnp.where(lane == MOE_GROUP_LANE, grp_idx, comb)


def moe_router(xt, g, shift, scale, w_route, rows_per_mod, tm):
    n = xt.shape[0]
    tiles_per_mod = rows_per_mod // tm
    mod_spec = pl.BlockSpec((1, 1, D_MODEL), lambda i: (i // tiles_per_mod, 0, 0))
    return pl.pallas_call(
        _router_kernel,
        name="moe_router",
        grid=(n // tm,),
        in_specs=[pl.BlockSpec((tm, D_MODEL), lambda i: (i, 0)), pl.BlockSpec((1, D_MODEL), lambda i: (0, 0)),
                  mod_spec, mod_spec, pl.BlockSpec((D_MODEL, LANE), lambda i: (0, 0))],
        out_specs=pl.BlockSpec((tm, LANE), lambda i: (i, 0)),
        out_shape=jax.ShapeDtypeStruct((n, LANE), F32),
        compiler_params=_params(("parallel",)),
    )(xt, g, shift, scale, w_route)


def _moe_kernel(final_norm, meta_ref, x_ref, g_ref, sh_ref, sc_ref, gt_ref, comb_ref, wg_ref, wu_ref, wd_ref, nf_ref,
                o_ref, pt_sc, xs_sc, cs_sc, acc):
    i = pl.program_id(0)
    e = pl.program_id(1)
    tm = x_ref.shape[0]

    @pl.when(e == 0)
    def _():
        u = _modulate(x_ref[...], g_ref[...], sh_ref[0], sc_ref[0]).astype(BF16)
        comb = comb_ref[...]
        lane = _iota((tm, LANE), 1)
        member = comb[:, MOE_GROUP_LANE:MOE_GROUP_LANE + 1] == lane.astype(F32)
        tri = jnp.where(_iota((tm, tm), 1) <= _iota((tm, tm), 0), 1.0, 0.0).astype(BF16)
        rank = _dot(tri, jnp.where(member, 1.0, 0.0).astype(BF16))
        offs = jnp.zeros((tm, LANE), jnp.int32)
        for grp in range(MOE_GROUPS):
            offs = jnp.where(lane == grp, meta_ref[i, grp], offs)
        pos = jnp.sum(jnp.where(member, offs.astype(F32) + rank - 1.0, 0.0), axis=-1, keepdims=True)
        pt = jnp.where(pos == _iota((tm, tm), 1).astype(F32), 1.0, 0.0).astype(BF16)
        pt_sc[...] = pt
        xs_sc[...] = _dot_tn(pt, u).astype(BF16)
        cs_sc[...] = sum(_dot_tn(pt, part) for part in _split3(comb))
        acc[...] = jnp.zeros_like(acc)

    grp = e // MOE_PER_GROUP
    start = meta_ref[i, grp]
    count = meta_ref[i, MOE_GROUPS + grp]
    lo = start // MOE_SUB
    hi = jnp.where(count > 0, (start + count + MOE_SUB - 1) // MOE_SUB, lo)

    def body(j, carry):
        rows = pl.ds(pl.multiple_of(j * MOE_SUB, MOE_SUB), MOE_SUB)
        xs = xs_sc[rows, :]
        h = _silu(_dot(xs, wg_ref[0])) * _dot(xs, wu_ref[0])
        cs = cs_sc[rows, :]
        w = jnp.sum(jnp.where(_iota(cs.shape, 1) == e, cs, 0.0), axis=-1, keepdims=True)
        acc[rows, :] += w * _dot(h.astype(BF16), wd_ref[0])
        return carry

    lax.fori_loop(lo, hi, body, 0)

    @pl.when(e == MOE_EXPERTS - 1)
    def _():
        y = x_ref[...] + gt_ref[0] * _dot_exact_rhs_t(pt_sc[...], acc[...])
        o_ref[...] = _rms(y) * nf_ref[...] if final_norm else y


def _dot_exact_rhs_t(sel, a):
    hi, lo = _split2(a)
    return _dot(sel, hi) + _dot(sel, lo)


def moe_experts(xt, g, shift, scale, gate, comb, w_gate, w_up, w_down, norm_final, final_norm, rows_per_mod, tm):
    n = xt.shape[0]
    tiles_per_mod = rows_per_mod // tm
    gid = comb[:, MOE_GROUP_LANE].astype(jnp.int32).reshape(n // tm, tm)
    counts = jnp.sum(gid[:, :, None] == jnp.arange(MOE_GROUPS, dtype=jnp.int32), axis=1, dtype=jnp.int32)
    meta = jnp.concatenate([jnp.cumsum(counts, axis=1) - counts, counts], axis=1)
    mod_spec = pl.BlockSpec((1, 1, D_MODEL), lambda i, e, m: (i // tiles_per_mod, 0, 0))
    return pl.pallas_call(
        functools.partial(_moe_kernel, final_norm),
        name="moe_experts",
        grid_spec=pltpu.PrefetchScalarGridSpec(
            num_scalar_prefetch=1,
            grid=(n // tm, MOE_EXPERTS),
            in_specs=[pl.BlockSpec((tm, D_MODEL), lambda i, e, m: (i, 0)),
                      pl.BlockSpec((1, D_MODEL), lambda i, e, m: (0, 0)), mod_spec, mod_spec, mod_spec,
                      pl.BlockSpec((tm, LANE), lambda i, e, m: (i, 0)),
                      pl.BlockSpec((1, D_MODEL, MOE_FF), lambda i, e, m: (e, 0, 0)),
                      pl.BlockSpec((1, D_MODEL, MOE_FF), lambda i, e, m: (e, 0, 0)),
                      pl.BlockSpec((1, MOE_FF, D_MODEL), lambda i, e, m: (e, 0, 0)),
                      pl.BlockSpec((1, D_MODEL), lambda i, e, m: (0, 0))],
            out_specs=pl.BlockSpec((tm, D_MODEL), lambda i, e, m: (i, 0)),
            scratch_shapes=[pltpu.VMEM((tm, tm), BF16), pltpu.VMEM((tm, D_MODEL), BF16),
                            pltpu.VMEM((tm, LANE), F32), pltpu.VMEM((tm, D_MODEL), F32)]),
        out_shape=jax.ShapeDtypeStruct((n, D_MODEL), F32),
        compiler_params=_params(("parallel", "arbitrary")),
    )(meta, xt, g, shift, scale, gate, comb, w_gate, w_up, w_down, norm_final)


def _mix_weights(w_in):
    o = 0
    parts = []
    for cols, padded in ((HY_COLS, HY_COLS), (S5_COLS, S5_COLS), (RW_COLS, RW_PAD), (M2_COLS, M2_PAD)):
        parts.append(jnp.pad(w_in[:, o:o + cols], ((0, 0), (0, padded - cols))))
        o += cols
    w_gate = w_in[:, o:].reshape(D_MODEL, N_BRANCH, D_MODEL).transpose(1, 0, 2)
    return jnp.concatenate(parts, axis=1).astype(BF16), w_gate.astype(BF16)


def kernel(x, c, ctx, c_ctx, mod_w, mod_b, norm_mix, norm_ffn, w_in, hy_conv_w, hy_conv_b, hy_f_w1, hy_f_b1, hy_f_w2, hy_f_b2, hy_f_w3, hy_f_freq, hy_bias, s5_lam_re, s5_lam_im, s5_log_step, s5_b_re, s5_b_im, s5_c_re, s5_c_im, s5_d, s5_w_glu, rw_mu, rw_w0, rw_w2, rw_a0, rw_a2, rw_g2, rw_k_k, rw_k_a, rw_r_k, rw_ln_w, rw_ln_b, m2_conv_w, m2_conv_b, m2_a_log, m2_dt_bias, m2_d, m2_norm_w, w_branch, w_out, moe_w_group, moe_w_expert, moe_w_gate, moe_w_up, moe_w_down, norm_final):
    b_, L, _ = x.shape
    lc = ctx.shape[1]
    depth = mod_w.shape[0]
    tm = 512
    tmc = min(tm, lc)

    cvec = jnp.zeros((SUBLANE, D_MODEL), F32).at[:b_].set(c).at[b_].set(c_ctx)
    mod = adaln_mod(cvec, mod_w, mod_b)

    xt = x.reshape(b_ * L, D_MODEL)
    ct = ctx.reshape(b_ * lc, D_MODEL)
    for l in range(depth):
        ctx_out = l < depth - 1
        mx = mod[l, :b_].reshape(b_, 1, N_MOD, D_MODEL)
        mc = mod[l, b_:b_ + 1].reshape(1, 1, N_MOD, D_MODEL)
        sh1, sc1, g1, sh2, sc2, g2 = (mx[:, :, i] for i in range(N_MOD))
        csh1, csc1, cg1, csh2, csc2, cg2 = (mc[:, :, i] for i in range(N_MOD))
        w_mix, w_gate = _mix_weights(w_in[l])
        nm, nf = norm_mix[l][None], norm_ffn[l][None]

        s5_prm = s5_prepare(s5_lam_re[l], s5_lam_im[l], s5_log_step[l], s5_b_re[l], s5_b_im[l], s5_c_re[l],
                            s5_c_im[l])
        rw_prm = rwkv_prepare(rw_mu[l], rw_w0[l], rw_w2[l], rw_a0[l], rw_a2[l], rw_g2[l], rw_k_k[l], rw_k_a[l],
                              rw_r_k[l], rw_ln_w[l], rw_ln_b[l])
        m2_prm = mamba_prepare(m2_conv_w[l], m2_conv_b[l], m2_a_log[l], m2_dt_bias[l], m2_d[l], m2_norm_w[l])
        hy_f = (hy_f_w1[l], hy_f_b1[l], hy_f_w2[l], hy_f_b2[l], hy_f_w3[l], hy_f_freq[l])

        def mixers(tokens, n_tok, shift, scale, rows_per_mod, tile, states, want_hyena):
            hy, s5, rw, m2z, m2x, m2dt = in_projection(tokens, nm, shift, scale, w_mix, rows_per_mod, tile)
            seq = lambda a: a.reshape(b_, n_tok, a.shape[-1])
            y_hy = None
            if want_hyena:
                t = min(HY_BLOCK, n_tok)
                ghat = hyena_filter_spectra(hyena_filter(n_tok, *hy_f), n_tok, t)
                y_hy = hyena_mixer(seq(hy), ghat, hy_conv_w[l], hy_conv_b[l], hy_bias[l])
            y_s5, s5_h = s5_mixer(seq(s5), states[0], *s5_prm, s5_d[l], s5_w_glu[l])
            y_rw, rw_h = rwkv_mixer(seq(rw), states[1], rw_prm)
            y_m2, m2_h = mamba_mixer(seq(m2z), seq(m2x), seq(m2dt), states[2], m2_prm)
            flat = lambda a: None if a is None else a.reshape(b_ * n_tok, W_BR)
            return [flat(y_hy), flat(y_s5), flat(y_rw), flat(y_m2)], (s5_h, rw_h, m2_h)

        zero_states = (jnp.zeros((b_, SUBLANE, 2 * S5_LANES), F32), jnp.zeros((b_, 2, W_BR, W_BR), F32),
                       jnp.zeros((b_, 2, M2_STATE, W_BR), F32))
        ys_c, ctx_states = mixers(ct, lc, csh1, csc1, b_ * lc, tmc, zero_states, ctx_out)
        ys_x, _ = mixers(xt, L, sh1, sc1, L, tm, ctx_states, True)

        w_route = jnp.zeros((D_MODEL, LANE), F32)
        w_route = w_route.at[:, :MOE_EXPERTS].set(moe_w_expert[l].transpose(1, 0, 2).reshape(D_MODEL, MOE_EXPERTS))
        w_route = w_route.at[:, MOE_EXPERTS:MOE_EXPERTS + MOE_GROUPS].set(moe_w_group[l])
        wb, wo = w_branch[l].astype(BF16), w_out[l].astype(BF16)
        wg, wu, wd = moe_w_gate[l].astype(BF16), moe_w_up[l].astype(BF16), moe_w_down[l].astype(BF16)
        nfin = norm_final[None]

        def channel_mix(tokens, ys, mods, rows_per_mod, tile, final):
            s1, c1, gt1, s2, c2, gt2 = mods
            t1 = merge_branches(tokens, nm, s1, c1, gt1, ys, w_gate, wb, wo, rows_per_mod, tile)
            comb = moe_router(t1, nf, s2, c2, w_route, rows_per_mod, tile)
            moe_tile = min(MOE_TILE, rows_per_mod)
            return moe_experts(t1, nf, s2, c2, gt2, comb, wg, wu, wd, nfin, final, rows_per_mod, moe_tile)

        xt = channel_mix(xt, ys_x, (sh1, sc1, g1, sh2, sc2, g2), L, tm, l == depth - 1)
        if ctx_out:
            ct = channel_mix(ct, ys_c, (csh1, csc1, cg1, csh2, csc2, cg2), b_ * lc, tmc, False)
    return xt.reshape(b_, L, D_MODEL)
```

```python
import functools
import math

import numpy as np
import jax
import jax.numpy as jnp
from jax import lax
from jax.experimental import pallas as pl
from jax.experimental.pallas import tpu as pltpu

F32 = jnp.float32
BF16 = jnp.bfloat16
HI = lax.Precision.HIGHEST

D_MODEL = 1024
W_BR = 256
N_BRANCH = 4
N_MOD = 6
NORM_EPS = 1e-6
GRID_W = 64

HY_BANDS = 8
HY_HID = 64
HY_TARGET = 1e-2
HY_FAST_PCT = 0.3
HY_SLOW_PCT = 1.5
HY_BLOCK = 512

S5_GROUP = 16
S5_GROUPS = 16
S5_STATE = 64
S5_LANES = S5_GROUPS * S5_STATE
S5_CHUNK = 256

RW_HEAD = 64
RW_HEADS = 4
RW_W_RANK = 32
RW_A_RANK = 32
RW_G_RANK = 64
RW_GN_EPS = 64e-5
RW_SUB = 64
RW_STEP = 512

M2_HEADDIM = 64
M2_HEADS = 4
M2_GROUPS = 2
M2_STATE = 128
M2_CHUNK = 256

MOE_GROUPS = 4
MOE_PER_GROUP = 4
MOE_EXPERTS = 16
MOE_FF = 512
MOE_GROUP_LANE = MOE_EXPERTS
MOE_TILE = 1024
MOE_SUB = 128

HY_COLS = 3 * W_BR
S5_COLS = W_BR
RW_COLS = 3 * W_BR + 2 * RW_W_RANK + 2 * RW_A_RANK + RW_G_RANK
M2_XBC = W_BR + 2 * M2_GROUPS * M2_STATE
M2_COLS = W_BR + M2_XBC + 2 * M2_HEADS
RW_PAD = 1024
M2_PAD = 1152
LANE = 128
SUBLANE = 8

VMEM_LIMIT = 56 * 1024 * 1024


def _dot(a, b, prec=None):
    return jnp.dot(a, b, preferred_element_type=F32, precision=prec)


def _dot_nt(a, b, prec=None):
    return lax.dot_general(a, b, (((1,), (1,)), ((), ())), preferred_element_type=F32, precision=prec)


def _dot_tn(a, b, prec=None):
    return lax.dot_general(a, b, (((0,), (0,)), ((), ())), preferred_element_type=F32, precision=prec)


def _bdot(a, b):
    return jnp.dot(a.astype(BF16), b.astype(BF16), preferred_element_type=F32)


def _sigmoid(x):
    return 1.0 / (1.0 + jnp.exp(-x))


def _silu(x):
    return x * _sigmoid(x)


def _params(sem):
    return pltpu.CompilerParams(dimension_semantics=sem, vmem_limit_bytes=VMEM_LIMIT)


def _rms(x):
    return x * lax.rsqrt(jnp.mean(x * x, axis=-1, keepdims=True) + NORM_EPS)


def _modulate(x, g, shift, scale):
    return _rms(x) * g * (1.0 + scale) + shift


def _iota(shape, axis):
    return lax.broadcasted_iota(jnp.int32, shape, axis)


def _shift_rows(cur, prev_row, next_row):
    t = cur.shape[0]
    row = _iota(cur.shape, 0)
    xm = jnp.where(row == 0, prev_row, pltpu.roll(cur, 1, 0))
    xp = jnp.where(row == t - 1, next_row, pltpu.roll(cur, t - 1, 0))
    return xm, xp


def _halo_specs(t, n_blocks, width, col_block, chunk_of):
    per = t // SUBLANE
    last = n_blocks * per - 1

    def prev_map(b, s):
        return (b, jnp.maximum(chunk_of(s) * per - 1, 0), col_block)

    def next_map(b, s):
        return (b, jnp.minimum((chunk_of(s) + 1) * per, last), col_block)

    return (pl.BlockSpec((1, SUBLANE, width), prev_map), pl.BlockSpec((1, SUBLANE, width), next_map))


def _halo_rows(prev_ref, next_ref, chunk, n_chunks):
    prev_row = jnp.where(chunk == 0, 0.0, prev_ref[0, SUBLANE - 1:SUBLANE, :])
    next_row = jnp.where(chunk == n_chunks - 1, 0.0, next_ref[0, 0:1, :])
    return prev_row, next_row


def _mod_kernel(c_ref, w_ref, b_ref, o_ref):
    w_hi, w_lo = _split2(w_ref[0])
    o_ref[0] = _dot_split(_silu(c_ref[...]), w_hi, w_lo) + b_ref[0]


def adaln_mod(cvec, mod_w, mod_b):
    depth = mod_w.shape[0]
    return pl.pallas_call(
        _mod_kernel,
        name="adaln_mod",
        grid=(depth, N_MOD),
        in_specs=[pl.BlockSpec((SUBLANE, D_MODEL), lambda l, j: (0, 0)),
                  pl.BlockSpec((1, D_MODEL, D_MODEL), lambda l, j: (l, 0, j)),
                  pl.BlockSpec((1, 1, D_MODEL), lambda l, j: (l, 0, j))],
        out_specs=pl.BlockSpec((1, SUBLANE, D_MODEL), lambda l, j: (l, 0, j)),
        out_shape=jax.ShapeDtypeStruct((depth, SUBLANE, N_MOD * D_MODEL), F32),
        compiler_params=_params(("parallel", "parallel")),
    )(cvec, mod_w, mod_b.reshape(depth, 1, N_MOD * D_MODEL))


def _inproj_kernel(x_ref, g_ref, sh_ref, sc_ref, w_ref, *out_refs):
    h = _modulate(x_ref[...], g_ref[...], sh_ref[0], sc_ref[0]).astype(BF16)
    o = 0
    for ref in out_refs:
        n = ref.shape[-1]
        ref[...] = _dot(h, w_ref[:, o:o + n])
        o += n


def in_projection(xt, g, shift, scale, w_mix, rows_per_mod, tm):
    n = xt.shape[0]
    tiles_per_mod = rows_per_mod // tm
    widths = (HY_COLS, S5_COLS, RW_PAD, W_BR, M2_XBC, M2_PAD - W_BR - M2_XBC)
    mod_spec = pl.BlockSpec((1, 1, D_MODEL), lambda i: (i // tiles_per_mod, 0, 0))
    return pl.pallas_call(
        _inproj_kernel,
        name="in_projection",
        grid=(n // tm,),
        in_specs=[pl.BlockSpec((tm, D_MODEL), lambda i: (i, 0)),
                  pl.BlockSpec((1, D_MODEL), lambda i: (0, 0)),
                  mod_spec, mod_spec,
                  pl.BlockSpec(w_mix.shape, lambda i: (0, 0))],
        out_specs=[pl.BlockSpec((tm, w), lambda i: (i, 0)) for w in widths],
        out_shape=[jax.ShapeDtypeStruct((n, w), F32) for w in widths],
        compiler_params=_params(("parallel",)),
    )(xt, g, shift, scale, w_mix)


def _dft_mats(t):
    k = np.arange(t, dtype=np.float64)[:, None]
    n = np.arange(2 * t, dtype=np.float64)[None, :]
    ang = np.pi * (2.0 * k + 1.0) * n / (2.0 * t)
    fwd = np.concatenate([np.cos(ang), -np.sin(ang)], axis=0)
    inv = fwd[:, :t].T / t
    return fwd, inv


def _split_const(m):
    hi = jnp.asarray(m, F32).astype(BF16)
    lo = (jnp.asarray(m, F32) - hi.astype(F32)).astype(BF16)
    return jnp.stack([hi, lo])


def _dot_split_lhs(c_ref, b):
    b_hi, b_lo = _split2(b)
    return _dot(c_ref[0], b_hi) + _dot(c_ref[0], b_lo) + _dot(c_ref[1], b_hi)


def _hyfilt_kernel(w1_ref, b1_ref, w2_ref, b2_ref, w3_ref, fr_ref, o_ref):
    L = o_ref.shape[0]
    shape = (L, LANE)
    pos = _iota(shape, 0).astype(F32)
    lane = _iota(shape, 1)
    t = pos / (L - 1)
    w = (2.0 * math.pi / L) * pos
    band = jnp.where(lane >= 1 + HY_BANDS, lane - 1 - HY_BANDS, lane - 1).astype(F32)
    f = 1e-4 + band * ((HY_BANDS - 1 - 1e-4) / (HY_BANDS - 1))
    arg = f * w
    feats = jnp.where(lane == 0, t,
                      jnp.where(lane <= HY_BANDS, jnp.cos(arg),
                                jnp.where(lane <= 2 * HY_BANDS, -jnp.sin(arg), 0.0)))
    fr = fr_ref[...]
    h = jnp.sin(fr * (_dot(feats, w1_ref[...], HI) + b1_ref[...]))
    h = jnp.sin(fr * (_dot(h, w2_ref[...], HI) + b2_ref[...]))
    h = _dot(h, w3_ref[...], HI)
    ch = _iota((L, W_BR), 1).astype(F32)
    lo = math.log(HY_TARGET) / HY_SLOW_PCT
    hi = math.log(HY_TARGET) / HY_FAST_PCT
    deltas = jnp.abs(lo + ch * ((hi - lo) / (W_BR - 1)))
    decay = jnp.exp(-(t[:, 0:1]) * deltas)
    o_ref[:, 0:W_BR] = h[:, 0:W_BR] * decay
    o_ref[:, W_BR:2 * W_BR] = h[:, W_BR:2 * W_BR] * decay


def hyena_filter(L, f_w1, f_b1, f_w2, f_b2, f_w3, f_freq):
    pad = lambda a, r, c: jnp.zeros((r, c), F32).at[:a.shape[0], :a.shape[1]].set(a)
    args = (pad(f_w1, LANE, LANE), pad(f_b1[None], 1, LANE), pad(f_w2, LANE, LANE), pad(f_b2[None], 1, LANE),
            pad(f_w3, LANE, 2 * W_BR), pad(f_freq[None], 1, LANE))
    return pl.pallas_call(
        _hyfilt_kernel,
        name="hyena_filter",
        out_shape=jax.ShapeDtypeStruct((L, 2 * W_BR), F32),
        compiler_params=_params(None),
    )(*args)


def _ghat_kernel(f_ref, g_ref, o_ref):
    o_ref[0] = _dot_split_lhs(f_ref, g_ref[0])


def hyena_filter_spectra(filt, L, t):
    nb = L // t
    table = jnp.concatenate([filt[:, :W_BR], filt[:, W_BR:], jnp.zeros((1, W_BR), F32)], axis=0)
    m = np.arange(2 * t)[None, :]
    d = np.arange(-(nb - 1), nb)[:, None]
    off = d * t + np.where(m < t, m, m - 2 * t)
    idx = np.where(m == t, 2 * L, np.where(off >= 0, off, L - off)).astype(np.int32)
    sign = np.where(m == t, 0.0, np.where(m < t, 1.0, -1.0)).astype(np.float32) * np.ones_like(off, np.float32)
    g = table[idx] * sign[:, :, None]
    fwd, _ = _dft_mats(t)
    return pl.pallas_call(
        _ghat_kernel,
        name="hyena_filter_spectra",
        grid=(2 * nb - 1,),
        in_specs=[pl.BlockSpec((2, 2 * t, 2 * t), lambda d: (0, 0, 0)),
                  pl.BlockSpec((1, 2 * t, W_BR), lambda d: (d, 0, 0))],
        out_specs=pl.BlockSpec((1, 2 * t, W_BR), lambda d: (d, 0, 0)),
        out_shape=jax.ShapeDtypeStruct((2 * nb - 1, 2 * t, W_BR), F32),
        compiler_params=_params(("parallel",)),
    )(_split_const(fwd), g)


def _hypre_kernel(nb, cur_ref, prev_ref, next_ref, cw_ref, cb_ref, f_ref, x0_ref, u_ref, uh_ref):
    j = pl.program_id(1)
    cur = cur_ref[0]
    prev_row, next_row = _halo_rows(prev_ref, next_ref, j, nb)
    xm, xp = _shift_rows(cur, prev_row, next_row)
    pc = cb_ref[...] + xm * cw_ref[0:1, :] + cur * cw_ref[1:2, :] + xp * cw_ref[2:3, :]
    u = pc[:, W_BR:2 * W_BR] * pc[:, 2 * W_BR:3 * W_BR]
    x0_ref[0] = pc[:, 0:W_BR]
    u_ref[0] = u
    uh_ref[0, 0] = _dot_split_lhs(f_ref, u)


def _hymain_kernel(nb, t, g_ref, uh_ref, x0_ref, u_ref, bias_ref, inv_ref, o_ref):
    i = pl.program_id(2)
    acc_re = jnp.zeros((t, LANE), F32)
    acc_im = jnp.zeros((t, LANE), F32)
    for j in range(nb):
        g = g_ref[i - j + nb - 1]
        gr, gi = g[0:t], g[t:2 * t]
        ur, ui = uh_ref[0, j, 0:t, :], uh_ref[0, j, t:2 * t, :]
        acc_re += gr * ur - gi * ui
        acc_im += gr * ui + gi * ur
    y = _dot_split_lhs(inv_ref, jnp.concatenate([acc_re, acc_im], axis=0))
    o_ref[0] = x0_ref[0] * (y + bias_ref[...] * u_ref[0])


def hyena_mixer(hy, ghat, conv_w, conv_b, h_bias):
    b_, L, _ = hy.shape
    t = min(HY_BLOCK, L)
    nb = L // t
    fwd, inv = _dft_mats(t)
    prev_spec, next_spec = _halo_specs(t, nb, HY_COLS, 0, lambda s: s)
    x0, u, uh = pl.pallas_call(
        functools.partial(_hypre_kernel, nb),
        name="hyena_conv_dft",
        grid=(b_, nb),
        in_specs=[pl.BlockSpec((1, t, HY_COLS), lambda b, j: (b, j, 0)), prev_spec, next_spec,
                  pl.BlockSpec((3, HY_COLS), lambda b, j: (0, 0)),
                  pl.BlockSpec((1, HY_COLS), lambda b, j: (0, 0)),
                  pl.BlockSpec((2, 2 * t, t), lambda b, j: (0, 0, 0))],
        out_specs=[pl.BlockSpec((1, t, W_BR), lambda b, j: (b, j, 0)),
                   pl.BlockSpec((1, t, W_BR), lambda b, j: (b, j, 0)),
                   pl.BlockSpec((1, 1, 2 * t, W_BR), lambda b, j: (b, j, 0, 0))],
        out_shape=[jax.ShapeDtypeStruct((b_, L, W_BR), F32), jax.ShapeDtypeStruct((b_, L, W_BR), F32),
                   jax.ShapeDtypeStruct((b_, nb, 2 * t, W_BR), F32)],
        compiler_params=_params(("parallel", "parallel")),
    )(hy, hy, hy, conv_w, conv_b[None], _split_const(fwd[:, :t]))
    ncb = W_BR // LANE
    return pl.pallas_call(
        functools.partial(_hymain_kernel, nb, t),
        name="hyena_longconv",
        grid=(ncb, b_, nb),
        in_specs=[pl.BlockSpec((2 * nb - 1, 2 * t, LANE), lambda c, b, i: (0, 0, c)),
                  pl.BlockSpec((1, nb, 2 * t, LANE), lambda c, b, i: (b, 0, 0, c)),
                  pl.BlockSpec((1, t, LANE), lambda c, b, i: (b, i, c)),
                  pl.BlockSpec((1, t, LANE), lambda c, b, i: (b, i, c)),
                  pl.BlockSpec((1, LANE), lambda c, b, i: (0, c)),
                  pl.BlockSpec((2, t, 2 * t), lambda c, b, i: (0, 0, 0))],
        out_specs=pl.BlockSpec((1, t, LANE), lambda c, b, i: (b, i, c)),
        out_shape=jax.ShapeDtypeStruct((b_, L, W_BR), F32),
        compiler_params=_params(("parallel", "parallel", "parallel")),
    )(ghat, uh, x0, u, h_bias[None], _split_const(inv))


def _s5_kernel(nc, u_ref, h0_ref, bb_ref, ap_ref, cb_ref, dsk_ref, wg_ref, y_ref, hf_ref, yacc, state):
    s = pl.program_id(1)
    t = u_ref.shape[1]
    n = S5_LANES

    @pl.when(s == 0)
    def _():
        hf_ref[...] = jnp.zeros_like(hf_ref)

    def phase(d):
        chunk = s if d == 0 else 2 * nc - 1 - s
        first = (s == 0) if d == 0 else (s == nc)
        last = (s == nc - 1) if d == 0 else (s == 2 * nc - 1)

        @pl.when(first)
        def _():
            state[0:1, :] = h0_ref[0, d:d + 1, :]

        u = u_ref[0]
        x = _dot(u.astype(BF16), bb_ref[d])
        row = _iota((t, n), 0)
        st = state[0:1, :]
        pw = (lambda p: p - 1) if d == 0 else (lambda p: SUBLANE - p)
        a_re, a_im = ap_ref[d, pw(1):pw(1) + 1, 0:n], ap_ref[d, pw(1):pw(1) + 1, n:2 * n]
        c_re = a_re * st[:, 0:n] - a_im * st[:, n:2 * n]
        c_im = a_re * st[:, n:2 * n] + a_im * st[:, 0:n]
        entry = 0 if d == 0 else t - 1
        x_re = x[:, 0:n] + jnp.where(row == entry, c_re, 0.0)
        x_im = x[:, n:2 * n] + jnp.where(row == entry, c_im, 0.0)
        n_grp = t // SUBLANE
        x_re, x_im = x_re.reshape(n_grp, SUBLANE, n), x_im.reshape(n_grp, SUBLANE, n)
        sub = _iota((SUBLANE, n), 0)
        for sh in (1, 2, 4):
            valid = (sub >= sh) if d == 0 else (sub < SUBLANE - sh)
            a_re = jnp.where(valid, ap_ref[d, pw(sh):pw(sh) + 1, 0:n], 0.0)[None]
            a_im = jnp.where(valid, ap_ref[d, pw(sh):pw(sh) + 1, n:2 * n], 0.0)[None]
            rot = sh if d == 0 else SUBLANE - sh
            s_re, s_im = pltpu.roll(x_re, rot, 1), pltpu.roll(x_im, rot, 1)
            x_re, x_im = x_re + a_re * s_re - a_im * s_im, x_im + a_re * s_im + a_im * s_re
        p_re, p_im = ap_ref[d, :, 0:n], ap_ref[d, :, n:2 * n]
        out_re, out_im = [None] * n_grp, [None] * n_grp
        c_re = c_im = None
        for g in (range(n_grp) if d == 0 else range(n_grp - 1, -1, -1)):
            g_re, g_im = x_re[g], x_im[g]
            if c_re is not None:
                g_re, g_im = g_re + p_re * c_re - p_im * c_im, g_im + p_re * c_im + p_im * c_re
            out_re[g], out_im[g] = g_re, g_im
            edge = SUBLANE - 1 if d == 0 else 0
            c_re, c_im = g_re[edge:edge + 1], g_im[edge:edge + 1]
        x_re, x_im = jnp.concatenate(out_re, axis=0), jnp.concatenate(out_im, axis=0)
        ex = t - 1 if d == 0 else 0
        state[0:1, 0:n] = x_re[ex:ex + 1, :]
        state[0:1, n:2 * n] = x_im[ex:ex + 1, :]
        yd = _dot(x_re.astype(BF16), cb_ref[d, 0:n, :]) + _dot(x_im.astype(BF16), cb_ref[d, n:2 * n, :])
        rows = pl.ds(pl.multiple_of(chunk * t, t), t)
        if d == 0:
            yacc[rows, :] = dsk_ref[...] * u + yd
        else:
            y = yacc[rows, :] + yd
            y = 0.5 * y * (1.0 + jnp.tanh(math.sqrt(2.0 / math.pi) * (y + 0.044715 * y * y * y)))
            y_ref[0] = y * _sigmoid(_bdot(y, wg_ref[...]))

        @pl.when(last)
        def _():
            hf_ref[0, d:d + 1, :] = state[0:1, :]

    @pl.when(s < nc)
    def _():
        phase(0)

    @pl.when(s >= nc)
    def _():
        phase(1)


def _two_pass_chunk(nc):
    return lambda s: jnp.where(s < nc, s, 2 * nc - 1 - s)


def _two_pass_out(nc):
    return lambda s: jnp.where(s < nc, nc - 1, 2 * nc - 1 - s)


def s5_prepare(lam_re, lam_im, log_step, b_re, b_im, c_re, c_im):
    step = jnp.exp(log_step)[:, :, None]
    mag = jnp.exp(lam_re * step)
    lb_re, lb_im = mag * jnp.cos(lam_im * step), mag * jnp.sin(lam_im * step)
    den = lam_re * lam_re + lam_im * lam_im
    q_re = ((lb_re - 1.0) * lam_re + lb_im * lam_im) / den
    q_im = (lb_im * lam_re - (lb_re - 1.0) * lam_im) / den
    bb_re = q_re[..., None] * b_re - q_im[..., None] * b_im
    bb_im = q_re[..., None] * b_im + q_im[..., None] * b_re
    eye = jnp.eye(S5_GROUPS, dtype=F32)

    def blockdiag_in(m):
        return jnp.einsum('dgni,gh->dgihn', m, eye).reshape(2, W_BR, S5_LANES)

    def blockdiag_out(m):
        return jnp.einsum('dgin,gh->dgnhi', m, eye).reshape(2, S5_LANES, W_BR)

    bblk = jnp.concatenate([blockdiag_in(bb_re), blockdiag_in(bb_im)], axis=-1)
    cblk = jnp.concatenate([blockdiag_out(c_re), -blockdiag_out(c_im)], axis=1)
    j = np.arange(SUBLANE, dtype=np.float32)
    pw = jnp.asarray(np.stack([j + 1.0, SUBLANE - j]))[:, :, None]
    arg_re = (lam_re * step).reshape(2, 1, S5_LANES) * pw
    arg_im = (lam_im * step).reshape(2, 1, S5_LANES) * pw
    apow = jnp.concatenate([jnp.exp(arg_re) * jnp.cos(arg_im), jnp.exp(arg_re) * jnp.sin(arg_im)], axis=-1)
    return bblk.astype(BF16), apow, cblk.astype(BF16)


def s5_mixer(u, h0, bblk, apow, cblk, d_skip, w_glu):
    b_, L, _ = u.shape
    t = min(S5_CHUNK, L)
    nc = L // t
    chunk_of, out_of = _two_pass_chunk(nc), _two_pass_out(nc)
    const = lambda a: pl.BlockSpec(a.shape, lambda b, s: (0,) * a.ndim)
    dsk = d_skip[None]
    wg = w_glu.astype(BF16)
    return pl.pallas_call(
        functools.partial(_s5_kernel, nc),
        name="s5_mixer",
        grid=(b_, 2 * nc),
        in_specs=[pl.BlockSpec((1, t, W_BR), lambda b, s: (b, chunk_of(s), 0)),
                  pl.BlockSpec((1, SUBLANE, 2 * S5_LANES), lambda b, s: (b, 0, 0)),
                  const(bblk), const(apow), const(cblk), const(dsk), const(wg)],
        out_specs=[pl.BlockSpec((1, t, W_BR), lambda b, s: (b, out_of(s), 0)),
                   pl.BlockSpec((1, SUBLANE, 2 * S5_LANES), lambda b, s: (b, 0, 0))],
        out_shape=[jax.ShapeDtypeStruct((b_, L, W_BR), F32),
                   jax.ShapeDtypeStruct((b_, SUBLANE, 2 * S5_LANES), F32)],
        scratch_shapes=[pltpu.VMEM((L, W_BR), F32), pltpu.VMEM((SUBLANE, 2 * S5_LANES), F32)],
        compiler_params=_params(("parallel", "arbitrary")),
    )(u, h0, bblk, apow, cblk, dsk, wg)


def _head_mask(h, shape, axis):
    lane = _iota(shape, axis)
    return (lane >= h * RW_HEAD) & (lane < (h + 1) * RW_HEAD)


def _split3(x):
    hi = x.astype(BF16)
    r1 = x - hi.astype(F32)
    mid = r1.astype(BF16)
    return hi, mid, (r1 - mid.astype(F32)).astype(BF16)


def _stack_heads(x):
    xb = x.astype(BF16)
    zero = jnp.zeros_like(xb)
    return jnp.concatenate([jnp.where(_head_mask(h, xb.shape, 1), xb, zero) for h in range(RW_HEADS)], axis=0)


def _split2(x):
    hi = x.astype(BF16)
    return hi, (x - hi.astype(F32)).astype(BF16)


def _dot_exact_rhs(a, b):
    hi, lo = _split2(a)
    return _dot(hi, b) + _dot(lo, b)


def _dot_split(a, b_hi, b_lo):
    hi, lo = _split2(a)
    return _dot(hi, b_hi) + _dot(hi, b_lo) + _dot(lo, b_hi)


RW_INV_BASE = 8


def _cat_to_blockdiag(x):
    t, n = x.shape
    xb = x.astype(BF16)
    tiled = jnp.concatenate([xb] * (n // t), axis=0)
    same = (_iota((n, n), 0) // t) == (_iota((n, n), 1) // t)
    return jnp.where(same, tiled, jnp.zeros_like(tiled))


def _tri_inverse(nmat):
    t = nmat[0].shape[0]
    row, col = _iota(nmat[0].shape, 0), _iota(nmat[0].shape, 1) % t
    same = lambda s: (row // s) == (col // s)
    mul = lambda x, y: _dot(x.astype(BF16), _cat_to_blockdiag(y))
    base = [jnp.where(same(RW_INV_BASE), x, 0.0) for x in nmat]
    eye = jnp.where(row == col, 1.0, 0.0)
    m = [eye - x for x in base]
    p = base
    for _ in range(RW_INV_BASE.bit_length() - 2):
        p = [mul(x, x) for x in p]
        m = [x + mul(x, y) for x, y in zip(m, p)]
    s = 2 * RW_INV_BASE
    while s <= t:
        off = same(s) & jnp.logical_not(same(s // 2))
        mc = [mul(x, jnp.where(off, y, 0.0)) for x, y in zip(m, nmat)]
        m = [x - mul(y, x) for x, y in zip(m, mc)]
        s *= 2
    return m


def _wkv_local(d, r, logw, kd, v, kk, a):
    t = r[0].shape[0]
    bf = lambda x: x.astype(BF16)
    each = lambda f, *ls: [f(*xs) for xs in zip(*ls)]
    ri, ci = _iota((t, t), 0), _iota((t, t), 1)
    tri = ((ci <= ri) if d == 0 else (ci >= ri)).astype(BF16)
    row, col = _iota((t, RW_HEADS * t), 0), _iota((t, RW_HEADS * t), 1) % t
    incl = (col <= row) if d == 0 else (col >= row)
    strict = (col < row) if d == 0 else (col > row)
    dn = W_BR
    diag = _iota((dn, dn), 0) == _iota((dn, dn), 1)

    cs = each(lambda x: sum(_dot(tri, part) for part in _split3(x)), logw)
    g_tot = each(lambda x: jnp.exp(jnp.sum(x, axis=0, keepdims=True)), logw)
    g_inv = each(lambda c: jnp.exp(-c), cs)
    bt = each(lambda k_, a_, g: k_ * a_ * g, kk, a, g_inv)
    kt = each(lambda k_, g: k_ * g, kd, g_inv)
    qa = each(lambda k_, c, w: k_ * jnp.exp(c - w), kk, cs, logw)
    rh = each(lambda r_, c: r_ * jnp.exp(c), r, cs)
    bts, kts, vs = each(_stack_heads, bt), each(_stack_heads, kt), each(_stack_heads, v)
    bhs = each(lambda x, g: _stack_heads(x * g), bt, g_tot)
    khs = each(lambda x, g: _stack_heads(x * g), kt, g_tot)
    a_ab = each(lambda q, b: jnp.where(strict, _dot_nt(bf(q), b), 0.0), qa, bts)
    a_ak = each(lambda q, k_: bf(jnp.where(strict, _dot_nt(bf(q), k_), 0.0)), qa, kts)
    a_rb = each(lambda q, b: bf(jnp.where(incl, _dot_nt(bf(q), b), 0.0)), rh, bts)
    a_rk = each(lambda q, k_: bf(jnp.where(incl, _dot_nt(bf(q), k_), 0.0)), rh, kts)
    m = each(bf, _tri_inverse(a_ab))
    x1 = each(lambda a_, v_: _stack_heads(_dot(a_, v_)), a_ak, vs)
    ul = each(lambda m_, x: _stack_heads(-_dot(m_, x)), m, x1)
    w0 = each(lambda m_, q: _stack_heads(_dot(m_, _stack_heads(q))), m, qa)
    yloc = each(lambda ak, v_, ab, u: _dot(ak, v_) + _dot(ab, u), a_rk, vs, a_rb, ul)
    yh = each(lambda r_, ab, w: r_ - _dot(ab, w), rh, a_rb, w0)
    p_bd = each(lambda g, b, w: jnp.where(diag, g, 0.0) - _dot_tn(b, w), g_tot, bhs, w0)
    q_bd = each(lambda b, u, k_, v_: _dot_tn(b, u) + _dot_tn(k_, v_), bhs, ul, khs, vs)
    return list(zip(yloc, yh, p_bd, q_bd))


def _rwkv_kernel(ng, cur_ref, prev_ref, next_ref, h0_ref, mu_ref, w0_ref, a0_ref, lrh_ref, lrl_ref,
                 kk_ref, ka_ref, rk_ref, lnw_ref, lnb_ref, y_ref, hf_ref, yacc, hst):
    s = pl.program_id(1)
    t = cur_ref.shape[1]
    n = W_BR
    ri, ci = _iota((n, n), 0), _iota((n, n), 1)
    head_sum = ((ri // RW_HEAD) == (ci // RW_HEAD)).astype(BF16)

    def phase(d):
        chunk = s if d == 0 else 2 * ng - 1 - s
        first = (s == 0) if d == 0 else (s == ng)
        last = (s == ng - 1) if d == 0 else (s == 2 * ng - 1)

        @pl.when(first)
        def _():
            hst[...] = h0_ref[0, d]

        cur = cur_ref[0]
        prev_row, next_row = _halo_rows(prev_ref, next_ref, chunk, ng)
        xm, xp = _shift_rows(cur, prev_row, next_row)
        p = cur + (0.5 * (xm + xp) - cur) * mu_ref[...]
        r, k, v, lr = p[:, 0:n], p[:, n:2 * n], p[:, 2 * n:3 * n], p[:, 3 * n:4 * n]
        kk = k * kk_ref[...]
        kk = kk * lax.rsqrt(jnp.maximum(_dot_exact_rhs(kk * kk, head_sum), 1e-24))
        lane = _iota(lr.shape, 1)
        feats = jnp.where(lane < 2 * RW_W_RANK, jnp.tanh(lr),
                          jnp.where(lane < 2 * (RW_W_RANK + RW_A_RANK), lr, _sigmoid(lr)))
        cols = (2 if d == 0 else 3) * n
        low = _dot_split(feats, lrh_ref[d, :, 0:cols], lrl_ref[d, :, 0:cols])
        w = w0_ref[d:d + 1, :] + low[:, 0:n]
        logw = -math.exp(-0.5) * _sigmoid(w)
        a = _sigmoid(a0_ref[d:d + 1, :] + low[:, n:2 * n])
        kd = k * (1.0 + (a - 1.0) * ka_ref[...])
        nsub = t // RW_SUB
        subs = lambda x: [x[q * RW_SUB:(q + 1) * RW_SUB] for q in range(nsub)]
        local = _wkv_local(d, subs(r), subs(logw), subs(kd), subs(v), subs(kk), subs(a))
        h = hst[...]
        ys = [None] * nsub
        for q in (range(nsub) if d == 0 else range(nsub - 1, -1, -1)):
            yloc, yh, p_bd, q_bd = local[q]
            hb = h.astype(BF16)
            ys[q] = yloc + _dot(yh.astype(BF16), hb)
            h = _dot(p_bd.astype(BF16), hb) + q_bd
        hst[...] = h
        yd = jnp.concatenate(ys, axis=0)
        rows = pl.ds(pl.multiple_of(chunk * t, t), t)
        if d == 0:
            yacc[rows, :] = yd
        else:
            y = yacc[rows, :] + yd
            mean = _dot_exact_rhs(y, head_sum) * (1.0 / RW_HEAD)
            yc = y - mean
            var = _dot_exact_rhs(yc * yc, head_sum) * (1.0 / RW_HEAD)
            y = yc * lax.rsqrt(var + RW_GN_EPS) * lnw_ref[...] + lnb_ref[...]
            bonus = _dot_exact_rhs(r * k * rk_ref[...], head_sum) * v
            y_ref[0] = (y + bonus) * low[:, 2 * n:3 * n]

        @pl.when(last)
        def _():
            hf_ref[0, d] = hst[...]

    @pl.when(s < ng)
    def _():
        phase(0)

    @pl.when(s >= ng)
    def _():
        phase(1)


def rwkv_prepare(mu, w0, w2, a0, a2, g2, k_k, k_a, r_k, ln_w, ln_b):
    n = W_BR
    mu_p = jnp.zeros((1, RW_PAD), F32).at[0, :RW_COLS].set(mu)
    w2_p = jnp.zeros((2, n, n), F32)
    a2_p = jnp.zeros((2, n, n), F32)
    for d in range(2):
        w2_p = w2_p.at[d, d * RW_W_RANK:(d + 1) * RW_W_RANK].set(w2[d])
        o = 2 * RW_W_RANK + d * RW_A_RANK
        a2_p = a2_p.at[d, o:o + RW_A_RANK].set(a2[d])
    o = 2 * RW_W_RANK + 2 * RW_A_RANK
    g2_p = jnp.zeros((n, n), F32).at[o:o + RW_G_RANK].set(g2)
    low = jnp.concatenate([w2_p, a2_p, jnp.broadcast_to(g2_p, (2, n, n))], axis=-1)
    low_hi = low.astype(BF16)
    low_lo = (low - low_hi.astype(F32)).astype(BF16)
    return (mu_p, w0, a0, low_hi, low_lo, k_k[None], k_a[None], r_k.reshape(1, n), ln_w[None], ln_b[None])


def rwkv_mixer(rw, h0, prm):
    b_, L, _ = rw.shape
    t = min(RW_STEP, L)
    ng = L // t
    chunk_of, out_of = _two_pass_chunk(ng), _two_pass_out(ng)
    prev_spec, next_spec = _halo_specs(t, ng, RW_PAD, 0, chunk_of)
    const = lambda a: pl.BlockSpec(a.shape, lambda b, s: (0,) * a.ndim)
    st_spec = pl.BlockSpec((1, 2, W_BR, W_BR), lambda b, s: (b, 0, 0, 0))
    return pl.pallas_call(
        functools.partial(_rwkv_kernel, ng),
        name="rwkv_mixer",
        grid=(b_, 2 * ng),
        in_specs=[pl.BlockSpec((1, t, RW_PAD), lambda b, s: (b, chunk_of(s), 0)), prev_spec, next_spec, st_spec]
                 + [const(a) for a in prm],
        out_specs=[pl.BlockSpec((1, t, W_BR), lambda b, s: (b, out_of(s), 0)), st_spec],
        out_shape=[jax.ShapeDtypeStruct((b_, L, W_BR), F32), jax.ShapeDtypeStruct((b_, 2, W_BR, W_BR), F32)],
        scratch_shapes=[pltpu.VMEM((L, W_BR), F32), pltpu.VMEM((W_BR, W_BR), F32)],
        compiler_params=_params(("parallel", "arbitrary")),
    )(rw, rw, rw, h0, *prm)


def _m2_kernel(nc, z_ref, cur_ref, prev_ref, next_ref, dt_ref, h0_ref, cw_ref, cb_ref, ex_ref, a_ref, dtb_ref,
               dsk_ref, nw_ref, y_ref, hf_ref, yacc, hst):
    s = pl.program_id(1)
    t = cur_ref.shape[1]
    n = W_BR
    ns = M2_STATE
    ri, ci = _iota((t, t), 0), _iota((t, t), 1)

    def phase(d):
        chunk = s if d == 0 else 2 * nc - 1 - s
        first = (s == 0) if d == 0 else (s == nc)
        last = (s == nc - 1) if d == 0 else (s == 2 * nc - 1)

        @pl.when(first)
        def _():
            hst[...] = h0_ref[0, d]

        cur = cur_ref[0]
        prev_row, next_row = _halo_rows(prev_ref, next_ref, chunk, nc)
        xm, xp = _shift_rows(cur, prev_row, next_row)
        xbc = _silu(cb_ref[...] + xm * cw_ref[0:1, :] + cur * cw_ref[1:2, :] + xp * cw_ref[2:3, :])
        xs, bm, cm = xbc[:, 0:n], xbc[:, n:2 * n], xbc[:, 2 * n:3 * n]
        raw = sum(_dot(part, ex_ref[d]) for part in _split3(dt_ref[0])) + dtb_ref[d:d + 1, :]
        dtd = jnp.maximum(raw, 0.0) + jnp.log(1.0 + jnp.exp(-jnp.abs(raw)))
        da = dtd * a_ref[d:d + 1, :]
        incl = (ci <= ri) if d == 0 else (ci >= ri)
        tri = incl.astype(BF16)
        da_parts = _split3(da)
        acs = sum(_dot(tri, part) for part in da_parts)
        tot = jnp.sum(da, axis=0, keepdims=True)
        xdt = xs * dtd
        xdt_b = xdt.astype(BF16)
        bm_b, cm_b = bm.astype(BF16), cm.astype(BF16)
        sel = jnp.where(_iota((SUBLANE, n), 1) // M2_HEADDIM == _iota((SUBLANE, n), 0), 1.0 / M2_HEADDIM,
                        0.0).astype(BF16)
        acs_t = sum(_dot_nt(sel, part) for part in _split3(acs))
        ydiag = jnp.zeros((t, n), F32)
        for g in range(M2_GROUPS):
            cb = _dot_nt(cm_b[:, g * ns:(g + 1) * ns], bm_b[:, g * ns:(g + 1) * ns])
            for h in range(g * (M2_HEADS // M2_GROUPS), (g + 1) * (M2_HEADS // M2_GROUPS)):
                seg = acs[:, h * M2_HEADDIM:h * M2_HEADDIM + 1] - acs_t[h:h + 1, :]
                scores = cb * jnp.exp(jnp.where(incl, seg, -jnp.inf))
                ydiag = jnp.where(_head_mask(h, (t, n), 1), _dot(scores.astype(BF16), xdt_b), ydiag)
        h_in = hst[...]
        h_b = h_in.astype(BF16)
        lane = _iota((t, n), 1)
        yoff = jnp.where(lane < n // M2_GROUPS, _dot(cm_b[:, 0:ns], h_b), _dot(cm_b[:, ns:2 * ns], h_b))
        yd = ydiag + jnp.exp(acs) * yoff
        xdec = (xdt * jnp.exp(tot - acs)).astype(BF16)
        lane_s = _iota((ns, n), 1)
        new = jnp.where(lane_s < n // M2_GROUPS, _dot_tn(bm_b[:, 0:ns], xdec), _dot_tn(bm_b[:, ns:2 * ns], xdec))
        hst[...] = h_in * jnp.exp(tot) + new
        rows = pl.ds(pl.multiple_of(chunk * t, t), t)
        if d == 0:
            yacc[rows, :] = yd + dsk_ref[...] * xs
        else:
            y = (yacc[rows, :] + yd) * _silu(z_ref[0])
            y_ref[0] = _rms(y) * nw_ref[...]

        @pl.when(last)
        def _():
            hf_ref[0, d] = hst[...]

    @pl.when(s < nc)
    def _():
        phase(0)

    @pl.when(s >= nc)
    def _():
        phase(1)


def mamba_prepare(conv_w, conv_b, a_log, dt_bias, d_skip, norm_w):
    n = W_BR
    rep = lambda v: jnp.repeat(v, M2_HEADDIM, axis=-1)
    expand = jnp.zeros((2, LANE, n), F32)
    for d in range(2):
        for h in range(M2_HEADS):
            expand = expand.at[d, d * M2_HEADS + h, h * M2_HEADDIM:(h + 1) * M2_HEADDIM].set(1.0)
    return (conv_w, conv_b[None], expand.astype(BF16), rep(-jnp.exp(a_log)), rep(dt_bias), rep(d_skip)[None],
            norm_w[None])


def mamba_mixer(z, xbc, dt, h0, prm):
    b_, L, _ = z.shape
    t = min(M2_CHUNK, L)
    nc = L // t
    chunk_of, out_of = _two_pass_chunk(nc), _two_pass_out(nc)
    prev_spec, next_spec = _halo_specs(t, nc, M2_XBC, 0, chunk_of)
    const = lambda a: pl.BlockSpec(a.shape, lambda b, s: (0,) * a.ndim)
    st_spec = pl.BlockSpec((1, 2, M2_STATE, W_BR), lambda b, s: (b, 0, 0, 0))
    seq = lambda w: pl.BlockSpec((1, t, w), lambda b, s: (b, chunk_of(s), 0))
    return pl.pallas_call(
        functools.partial(_m2_kernel, nc),
        name="mamba_mixer",
        grid=(b_, 2 * nc),
        in_specs=[seq(W_BR), seq(M2_XBC), prev_spec, next_spec, seq(LANE), st_spec] + [const(a) for a in prm],
        out_specs=[pl.BlockSpec((1, t, W_BR), lambda b, s: (b, out_of(s), 0)), st_spec],
        out_shape=[jax.ShapeDtypeStruct((b_, L, W_BR), F32),
                   jax.ShapeDtypeStruct((b_, 2, M2_STATE, W_BR), F32)],
        scratch_shapes=[pltpu.VMEM((L, W_BR), F32), pltpu.VMEM((M2_STATE, W_BR), F32)],
        compiler_params=_params(("parallel", "arbitrary")),
    )(z, xbc, xbc, xbc, dt, h0, *prm)


def _merge_kernel(x_ref, g_ref, sh_ref, sc_ref, gt_ref, y0_ref, y1_ref, y2_ref, y3_ref, wg_ref, wb_ref, wo_ref,
                  o_ref):
    x = x_ref[...]
    h = _modulate(x, g_ref[...], sh_ref[0], sc_ref[0]).astype(BF16)
    acc = jnp.zeros(x.shape, F32)
    for i, y_ref in enumerate((y0_ref, y1_ref, y2_ref, y3_ref)):
        gate = _sigmoid(_dot(h, wg_ref[i]))
        acc += gate * _dot(y_ref[...].astype(BF16), wb_ref[i])
    o_ref[...] = x + gt_ref[0] * _dot(acc.astype(BF16), wo_ref[...])


def merge_branches(xt, g, shift, scale, gate, ys, w_gate, w_branch, w_out, rows_per_mod, tm):
    n = xt.shape[0]
    tiles_per_mod = rows_per_mod // tm
    mod_spec = pl.BlockSpec((1, 1, D_MODEL), lambda i: (i // tiles_per_mod, 0, 0))
    const = lambda a: pl.BlockSpec(a.shape, lambda i: (0,) * a.ndim)
    tok = lambda w: pl.BlockSpec((tm, w), lambda i: (i, 0))
    return pl.pallas_call(
        _merge_kernel,
        name="merge_branches",
        grid=(n // tm,),
        in_specs=[tok(D_MODEL), const(g), mod_spec, mod_spec, mod_spec] + [tok(W_BR)] * N_BRANCH
                 + [const(w_gate), const(w_branch), const(w_out)],
        out_specs=tok(D_MODEL),
        out_shape=jax.ShapeDtypeStruct((n, D_MODEL), F32),
        compiler_params=_params(("parallel",)),
    )(xt, g, shift, scale, gate, *ys, w_gate, w_branch, w_out)


def _router_kernel(x_ref, g_ref, sh_ref, sc_ref, wr_ref, comb_ref):
    u = _modulate(x_ref[...], g_ref[...], sh_ref[0], sc_ref[0])
    logits = _dot_split(u, wr_ref[0], wr_ref[1])
    lane = _iota(logits.shape, 1).astype(F32)
    neg = -jnp.inf
    is_grp = (lane >= MOE_EXPERTS) & (lane < MOE_EXPERTS + MOE_GROUPS)
    gl = jnp.where(is_grp, logits, neg)
    gmax = jnp.max(gl, axis=-1, keepdims=True)
    grp_p = 1.0 / jnp.sum(jnp.exp(gl - gmax), axis=-1, keepdims=True)
    grp_idx = jnp.min(jnp.where(gl == gmax, lane, 4.0 * LANE), axis=-1, keepdims=True) - MOE_EXPERTS
    in_grp = (lane >= grp_idx * MOE_PER_GROUP) & (lane < (grp_idx + 1) * MOE_PER_GROUP)
    el = jnp.where(in_grp, logits, neg)
    v1 = jnp.max(el, axis=-1, keepdims=True)
    i1 = jnp.min(jnp.where(el == v1, lane, 4.0 * LANE), axis=-1, keepdims=True)
    el2 = jnp.where(lane == i1, neg, el)
    v2 = jnp.max(el2, axis=-1, keepdims=True)
    i2 = jnp.min(jnp.where(el2 == v2, lane, 4.0 * LANE), axis=-1, keepdims=True)
    e2 = jnp.exp(v2 - v1)
    w1 = grp_p / (1.0 + e2)
    comb = jnp.where(lane == i1, w1, jnp.where(lane == i2, w1 * e2, 0.0))
    comb_ref[...] = jnp.where(lane == MOE_GROUP_LANE, grp_idx, comb)


def moe_router(xt, g, shift, scale, w_route, rows_per_mod, tm):
    n = xt.shape[0]
    tiles_per_mod = rows_per_mod // tm
    mod_spec = pl.BlockSpec((1, 1, D_MODEL), lambda i: (i // tiles_per_mod, 0, 0))
    return pl.pallas_call(
        _router_kernel,
        name="moe_router",
        grid=(n // tm,),
        in_specs=[pl.BlockSpec((tm, D_MODEL), lambda i: (i, 0)), pl.BlockSpec((1, D_MODEL), lambda i: (0, 0)),
                  mod_spec, mod_spec, pl.BlockSpec((2, D_MODEL, LANE), lambda i: (0, 0, 0))],
        out_specs=pl.BlockSpec((tm, LANE), lambda i: (i, 0)),
        out_shape=jax.ShapeDtypeStruct((n, LANE), F32),
        compiler_params=_params(("parallel",)),
    )(xt, g, shift, scale, w_route)


def _moe_kernel(final_norm, meta_ref, x_ref, g_ref, sh_ref, sc_ref, gt_ref, comb_ref, wg_ref, wu_ref, wd_ref, nf_ref,
                o_ref, pt_sc, xs_sc, cs_sc, acc):
    i = pl.program_id(0)
    e = pl.program_id(1)
    tm = x_ref.shape[0]

    @pl.when(e == 0)
    def _():
        u = _modulate(x_ref[...], g_ref[...], sh_ref[0], sc_ref[0]).astype(BF16)
        comb = comb_ref[...]
        lane = _iota((tm, LANE), 1)
        member = comb[:, MOE_GROUP_LANE:MOE_GROUP_LANE + 1] == lane.astype(F32)
        tri = jnp.where(_iota((tm, tm), 1) <= _iota((tm, tm), 0), 1.0, 0.0).astype(BF16)
        rank = _dot(tri, jnp.where(member, 1.0, 0.0).astype(BF16))
        offs = jnp.zeros((tm, LANE), jnp.int32)
        for grp in range(MOE_GROUPS):
            offs = jnp.where(lane == grp, meta_ref[i, grp], offs)
        posmat = jnp.where(member, offs.astype(F32) + rank - 1.0, 0.0)
        pos_col = jnp.sum(posmat, axis=-1, keepdims=True)
        ones = jnp.ones((SUBLANE, LANE), BF16)
        pos_row = sum(_dot_nt(ones, part) for part in _split3(posmat))[0:1, :]
        pt_sc[...] = jnp.where(pos_col == _iota((tm, tm), 1).astype(F32), 1.0, 0.0).astype(BF16)
        p = jnp.where(pos_row == _iota((tm, tm), 0).astype(F32), 1.0, 0.0).astype(BF16)
        xs_sc[...] = _dot(p, u).astype(BF16)
        cs_sc[...] = sum(_dot(p, part) for part in _split3(comb))
        acc[...] = jnp.zeros_like(acc)

    grp = e // MOE_PER_GROUP
    start = meta_ref[i, grp]
    count = meta_ref[i, MOE_GROUPS + grp]
    lo = start // MOE_SUB
    hi = jnp.where(count > 0, (start + count + MOE_SUB - 1) // MOE_SUB, lo)

    def body(j, carry):
        rows = pl.ds(pl.multiple_of(j * MOE_SUB, MOE_SUB), MOE_SUB)
        xs = xs_sc[rows, :]
        h = _silu(_dot(xs, wg_ref[0])) * _dot(xs, wu_ref[0])
        cs = cs_sc[rows, :]
        w = jnp.sum(jnp.where(_iota(cs.shape, 1) == e, cs, 0.0), axis=-1, keepdims=True)
        acc[rows, :] += w * _dot(h.astype(BF16), wd_ref[0])
        return carry

    lax.fori_loop(lo, hi, body, 0)

    @pl.when(e == MOE_EXPERTS - 1)
    def _():
        y = x_ref[...] + gt_ref[0] * _dot_exact_rhs_t(pt_sc[...], acc[...])
        o_ref[...] = _rms(y) * nf_ref[...] if final_norm else y


def _dot_exact_rhs_t(sel, a):
    hi, lo = _split2(a)
    return _dot(sel, hi) + _dot(sel, lo)


def moe_experts(xt, g, shift, scale, gate, comb, w_gate, w_up, w_down, norm_final, final_norm, rows_per_mod, tm):
    n = xt.shape[0]
    tiles_per_mod = rows_per_mod // tm
    gid = comb[:, MOE_GROUP_LANE].astype(jnp.int32).reshape(n // tm, tm)
    counts = jnp.sum(gid[:, :, None] == jnp.arange(MOE_GROUPS, dtype=jnp.int32), axis=1, dtype=jnp.int32)
    meta = jnp.concatenate([jnp.cumsum(counts, axis=1) - counts, counts], axis=1)
    mod_spec = pl.BlockSpec((1, 1, D_MODEL), lambda i, e, m: (i // tiles_per_mod, 0, 0))
    return pl.pallas_call(
        functools.partial(_moe_kernel, final_norm),
        name="moe_experts",
        grid_spec=pltpu.PrefetchScalarGridSpec(
            num_scalar_prefetch=1,
            grid=(n // tm, MOE_EXPERTS),
            in_specs=[pl.BlockSpec((tm, D_MODEL), lambda i, e, m: (i, 0)),
                      pl.BlockSpec((1, D_MODEL), lambda i, e, m: (0, 0)), mod_spec, mod_spec, mod_spec,
                      pl.BlockSpec((tm, LANE), lambda i, e, m: (i, 0)),
                      pl.BlockSpec((1, D_MODEL, MOE_FF), lambda i, e, m: (e, 0, 0)),
                      pl.BlockSpec((1, D_MODEL, MOE_FF), lambda i, e, m: (e, 0, 0)),
                      pl.BlockSpec((1, MOE_FF, D_MODEL), lambda i, e, m: (e, 0, 0)),
                      pl.BlockSpec((1, D_MODEL), lambda i, e, m: (0, 0))],
            out_specs=pl.BlockSpec((tm, D_MODEL), lambda i, e, m: (i, 0)),
            scratch_shapes=[pltpu.VMEM((tm, tm), BF16), pltpu.VMEM((tm, D_MODEL), BF16),
                            pltpu.VMEM((tm, LANE), F32), pltpu.VMEM((tm, D_MODEL), F32)]),
        out_shape=jax.ShapeDtypeStruct((n, D_MODEL), F32),
        compiler_params=_params(("parallel", "arbitrary")),
    )(meta, xt, g, shift, scale, gate, comb, w_gate, w_up, w_down, norm_final)


def _mix_weights(w_in):
    o = 0
    parts = []
    for cols, padded in ((HY_COLS, HY_COLS), (S5_COLS, S5_COLS), (RW_COLS, RW_PAD), (M2_COLS, M2_PAD)):
        parts.append(jnp.pad(w_in[:, o:o + cols], ((0, 0), (0, padded - cols))))
        o += cols
    w_gate = w_in[:, o:].reshape(D_MODEL, N_BRANCH, D_MODEL).transpose(1, 0, 2)
    return jnp.concatenate(parts, axis=1).astype(BF16), w_gate.astype(BF16)


def kernel(x, c, ctx, c_ctx, mod_w, mod_b, norm_mix, norm_ffn, w_in, hy_conv_w, hy_conv_b, hy_f_w1, hy_f_b1, hy_f_w2, hy_f_b2, hy_f_w3, hy_f_freq, hy_bias, s5_lam_re, s5_lam_im, s5_log_step, s5_b_re, s5_b_im, s5_c_re, s5_c_im, s5_d, s5_w_glu, rw_mu, rw_w0, rw_w2, rw_a0, rw_a2, rw_g2, rw_k_k, rw_k_a, rw_r_k, rw_ln_w, rw_ln_b, m2_conv_w, m2_conv_b, m2_a_log, m2_dt_bias, m2_d, m2_norm_w, w_branch, w_out, moe_w_group, moe_w_expert, moe_w_gate, moe_w_up, moe_w_down, norm_final):
    b_, L, _ = x.shape
    lc = ctx.shape[1]
    depth = mod_w.shape[0]
    tm = 512
    tmc = min(tm, lc)

    cvec = jnp.zeros((SUBLANE, D_MODEL), F32).at[:b_].set(c).at[b_].set(c_ctx)
    mod = adaln_mod(cvec, mod_w, mod_b)

    xt = x.reshape(b_ * L, D_MODEL)
    ct = ctx.reshape(b_ * lc, D_MODEL)
    for l in range(depth):
        ctx_out = l < depth - 1
        mx = mod[l, :b_].reshape(b_, 1, N_MOD, D_MODEL)
        mc = mod[l, b_:b_ + 1].reshape(1, 1, N_MOD, D_MODEL)
        sh1, sc1, g1, sh2, sc2, g2 = (mx[:, :, i] for i in range(N_MOD))
        csh1, csc1, cg1, csh2, csc2, cg2 = (mc[:, :, i] for i in range(N_MOD))
        w_mix, w_gate = _mix_weights(w_in[l])
        nm, nf = norm_mix[l][None], norm_ffn[l][None]

        s5_prm = s5_prepare(s5_lam_re[l], s5_lam_im[l], s5_log_step[l], s5_b_re[l], s5_b_im[l], s5_c_re[l],
                            s5_c_im[l])
        rw_prm = rwkv_prepare(rw_mu[l], rw_w0[l], rw_w2[l], rw_a0[l], rw_a2[l], rw_g2[l], rw_k_k[l], rw_k_a[l],
                              rw_r_k[l], rw_ln_w[l], rw_ln_b[l])
        m2_prm = mamba_prepare(m2_conv_w[l], m2_conv_b[l], m2_a_log[l], m2_dt_bias[l], m2_d[l], m2_norm_w[l])
        hy_f = (hy_f_w1[l], hy_f_b1[l], hy_f_w2[l], hy_f_b2[l], hy_f_w3[l], hy_f_freq[l])

        def mixers(tokens, n_tok, shift, scale, rows_per_mod, tile, states, want_hyena):
            hy, s5, rw, m2z, m2x, m2dt = in_projection(tokens, nm, shift, scale, w_mix, rows_per_mod, tile)
            seq = lambda a: a.reshape(b_, n_tok, a.shape[-1])
            y_hy = None
            if want_hyena:
                t = min(HY_BLOCK, n_tok)
                ghat = hyena_filter_spectra(hyena_filter(n_tok, *hy_f), n_tok, t)
                y_hy = hyena_mixer(seq(hy), ghat, hy_conv_w[l], hy_conv_b[l], hy_bias[l])
            y_s5, s5_h = s5_mixer(seq(s5), states[0], *s5_prm, s5_d[l], s5_w_glu[l])
            y_rw, rw_h = rwkv_mixer(seq(rw), states[1], rw_prm)
            y_m2, m2_h = mamba_mixer(seq(m2z), seq(m2x), seq(m2dt), states[2], m2_prm)
            flat = lambda a: None if a is None else a.reshape(b_ * n_tok, W_BR)
            return [flat(y_hy), flat(y_s5), flat(y_rw), flat(y_m2)], (s5_h, rw_h, m2_h)

        zero_states = (jnp.zeros((b_, SUBLANE, 2 * S5_LANES), F32), jnp.zeros((b_, 2, W_BR, W_BR), F32),
                       jnp.zeros((b_, 2, M2_STATE, W_BR), F32))
        ys_c, ctx_states = mixers(ct, lc, csh1, csc1, b_ * lc, tmc, zero_states, ctx_out)
        ys_x, _ = mixers(xt, L, sh1, sc1, L, tm, ctx_states, True)

        w_route = jnp.zeros((D_MODEL, LANE), F32)
        w_route = w_route.at[:, :MOE_EXPERTS].set(moe_w_expert[l].transpose(1, 0, 2).reshape(D_MODEL, MOE_EXPERTS))
        w_route = w_route.at[:, MOE_EXPERTS:MOE_EXPERTS + MOE_GROUPS].set(moe_w_group[l])
        w_route = jnp.stack(_split2(w_route))
        wb, wo = w_branch[l].astype(BF16), w_out[l].astype(BF16)
        wg, wu, wd = moe_w_gate[l].astype(BF16), moe_w_up[l].astype(BF16), moe_w_down[l].astype(BF16)
        nfin = norm_final[None]

        def channel_mix(tokens, ys, mods, rows_per_mod, tile, final):
            s1, c1, gt1, s2, c2, gt2 = mods
            t1 = merge_branches(tokens, nm, s1, c1, gt1, ys, w_gate, wb, wo, rows_per_mod, tile)
            comb = moe_router(t1, nf, s2, c2, w_route, rows_per_mod, tile)
            moe_tile = min(MOE_TILE, rows_per_mod)
            return moe_experts(t1, nf, s2, c2, gt2, comb, wg, wu, wd, nfin, final, rows_per_mod, moe_tile)

        xt = channel_mix(xt, ys_x, (sh1, sc1, g1, sh2, sc2, g2), L, tm, l == depth - 1)
        if ctx_out:
            ct = channel_mix(ct, ys_c, (csh1, csc1, cg1, csh2, csc2, cg2), b_ * lc, tmc, False)
    return xt.reshape(b_, L, D_MODEL)
```

```python
import functools
import math

import numpy as np
import jax
import jax.numpy as jnp
from jax import lax
from jax.experimental import pallas as pl
from jax.experimental.pallas import tpu as pltpu

F32 = jnp.float32
BF16 = jnp.bfloat16
HI = lax.Precision.HIGHEST

D_MODEL = 1024
W_BR = 256
N_BRANCH = 4
N_MOD = 6
NORM_EPS = 1e-6
GRID_W = 64

HY_BANDS = 8
HY_HID = 64
HY_TARGET = 1e-2
HY_FAST_PCT = 0.3
HY_SLOW_PCT = 1.5
HY_BLOCK = 512

S5_GROUP = 16
S5_GROUPS = 16
S5_STATE = 64
S5_LANES = S5_GROUPS * S5_STATE
S5_CHUNK = 256

RW_HEAD = 64
RW_HEADS = 4
RW_W_RANK = 32
RW_A_RANK = 32
RW_G_RANK = 64
RW_GN_EPS = 64e-5
RW_SUB = 64
RW_STEP = 512

M2_HEADDIM = 64
M2_HEADS = 4
M2_GROUPS = 2
M2_STATE = 128
M2_CHUNK = 256

MOE_GROUPS = 4
MOE_PER_GROUP = 4
MOE_EXPERTS = 16
MOE_FF = 512
MOE_GROUP_LANE = MOE_EXPERTS
MOE_TILE = 1024
MOE_SUB = 128

HY_COLS = 3 * W_BR
S5_COLS = W_BR
RW_COLS = 3 * W_BR + 2 * RW_W_RANK + 2 * RW_A_RANK + RW_G_RANK
M2_XBC = W_BR + 2 * M2_GROUPS * M2_STATE
M2_COLS = W_BR + M2_XBC + 2 * M2_HEADS
RW_PAD = 1024
M2_PAD = 1152
LANE = 128
SUBLANE = 8

VMEM_LIMIT = 56 * 1024 * 1024


def _dot(a, b, prec=None):
    return jnp.dot(a, b, preferred_element_type=F32, precision=prec)


def _dot_nt(a, b, prec=None):
    return lax.dot_general(a, b, (((1,), (1,)), ((), ())), preferred_element_type=F32, precision=prec)


def _dot_tn(a, b, prec=None):
    return lax.dot_general(a, b, (((0,), (0,)), ((), ())), preferred_element_type=F32, precision=prec)


def _bdot(a, b):
    return jnp.dot(a.astype(BF16), b.astype(BF16), preferred_element_type=F32)


def _sigmoid(x):
    return 1.0 / (1.0 + jnp.exp(-x))


def _silu(x):
    return x * _sigmoid(x)


def _params(sem):
    return pltpu.CompilerParams(dimension_semantics=sem, vmem_limit_bytes=VMEM_LIMIT)


def _rms(x):
    return x * lax.rsqrt(jnp.mean(x * x, axis=-1, keepdims=True) + NORM_EPS)


def _modulate(x, g, shift, scale):
    return _rms(x) * g * (1.0 + scale) + shift


def _iota(shape, axis):
    return lax.broadcasted_iota(jnp.int32, shape, axis)


def _shift_rows(cur, prev_row, next_row):
    t = cur.shape[0]
    row = _iota(cur.shape, 0)
    xm = jnp.where(row == 0, prev_row, pltpu.roll(cur, 1, 0))
    xp = jnp.where(row == t - 1, next_row, pltpu.roll(cur, t - 1, 0))
    return xm, xp


def _halo_specs(t, n_blocks, width, col_block, chunk_of):
    per = t // SUBLANE
    last = n_blocks * per - 1

    def prev_map(b, s):
        return (b, jnp.maximum(chunk_of(s) * per - 1, 0), col_block)

    def next_map(b, s):
        return (b, jnp.minimum((chunk_of(s) + 1) * per, last), col_block)

    return (pl.BlockSpec((1, SUBLANE, width), prev_map), pl.BlockSpec((1, SUBLANE, width), next_map))


def _halo_rows(prev_ref, next_ref, chunk, n_chunks):
    prev_row = jnp.where(chunk == 0, 0.0, prev_ref[0, SUBLANE - 1:SUBLANE, :])
    next_row = jnp.where(chunk == n_chunks - 1, 0.0, next_ref[0, 0:1, :])
    return prev_row, next_row


def _mod_kernel(c_ref, w_ref, b_ref, o_ref):
    w_hi, w_lo = _split2(w_ref[0])
    o_ref[0] = _dot_split(_silu(c_ref[...]), w_hi, w_lo) + b_ref[0]


def adaln_mod(cvec, mod_w, mod_b):
    depth = mod_w.shape[0]
    return pl.pallas_call(
        _mod_kernel,
        name="adaln_mod",
        grid=(depth, N_MOD),
        in_specs=[pl.BlockSpec((SUBLANE, D_MODEL), lambda l, j: (0, 0)),
                  pl.BlockSpec((1, D_MODEL, D_MODEL), lambda l, j: (l, 0, j)),
                  pl.BlockSpec((1, 1, D_MODEL), lambda l, j: (l, 0, j))],
        out_specs=pl.BlockSpec((1, SUBLANE, D_MODEL), lambda l, j: (l, 0, j)),
        out_shape=jax.ShapeDtypeStruct((depth, SUBLANE, N_MOD * D_MODEL), F32),
        compiler_params=_params(("parallel", "parallel")),
    )(cvec, mod_w, mod_b.reshape(depth, 1, N_MOD * D_MODEL))


def _inproj_kernel(x_ref, g_ref, sh_ref, sc_ref, w_ref, *out_refs):
    h = _modulate(x_ref[...], g_ref[...], sh_ref[0], sc_ref[0]).astype(BF16)
    o = 0
    for ref in out_refs:
        n = ref.shape[-1]
        ref[...] = _dot(h, w_ref[:, o:o + n])
        o += n


def in_projection(xt, g, shift, scale, w_mix, rows_per_mod, tm):
    n = xt.shape[0]
    tiles_per_mod = rows_per_mod // tm
    widths = (HY_COLS, S5_COLS, RW_PAD, W_BR, M2_XBC, M2_PAD - W_BR - M2_XBC)
    mod_spec = pl.BlockSpec((1, 1, D_MODEL), lambda i: (i // tiles_per_mod, 0, 0))
    return pl.pallas_call(
        _inproj_kernel,
        name="in_projection",
        grid=(n // tm,),
        in_specs=[pl.BlockSpec((tm, D_MODEL), lambda i: (i, 0)),
                  pl.BlockSpec((1, D_MODEL), lambda i: (0, 0)),
                  mod_spec, mod_spec,
                  pl.BlockSpec(w_mix.shape, lambda i: (0, 0))],
        out_specs=[pl.BlockSpec((tm, w), lambda i: (i, 0)) for w in widths],
        out_shape=[jax.ShapeDtypeStruct((n, w), F32) for w in widths],
        compiler_params=_params(("parallel",)),
    )(xt, g, shift, scale, w_mix)


def _dft_mats(t):
    k = np.arange(t, dtype=np.float64)[:, None]
    n = np.arange(2 * t, dtype=np.float64)[None, :]
    ang = np.pi * (2.0 * k + 1.0) * n / (2.0 * t)
    fwd = np.concatenate([np.cos(ang), -np.sin(ang)], axis=0)
    inv = fwd[:, :t].T / t
    return fwd, inv


def _split_const(m):
    hi = jnp.asarray(m, F32).astype(BF16)
    lo = (jnp.asarray(m, F32) - hi.astype(F32)).astype(BF16)
    return jnp.stack([hi, lo])


def _dot_split_lhs(c_ref, b):
    b_hi, b_lo = _split2(b)
    return _dot(c_ref[0], b_hi) + _dot(c_ref[0], b_lo) + _dot(c_ref[1], b_hi)


def _hyfilt_kernel(w1_ref, b1_ref, w2_ref, b2_ref, w3_ref, fr_ref, o_ref):
    L = o_ref.shape[0]
    shape = (L, LANE)
    pos = _iota(shape, 0).astype(F32)
    lane = _iota(shape, 1)
    t = pos / (L - 1)
    w = (2.0 * math.pi / L) * pos
    band = jnp.where(lane >= 1 + HY_BANDS, lane - 1 - HY_BANDS, lane - 1).astype(F32)
    f = 1e-4 + band * ((HY_BANDS - 1 - 1e-4) / (HY_BANDS - 1))
    arg = f * w
    feats = jnp.where(lane == 0, t,
                      jnp.where(lane <= HY_BANDS, jnp.cos(arg),
                                jnp.where(lane <= 2 * HY_BANDS, -jnp.sin(arg), 0.0)))
    fr = fr_ref[...]
    h = jnp.sin(fr * (_dot(feats, w1_ref[...], HI) + b1_ref[...]))
    h = jnp.sin(fr * (_dot(h, w2_ref[...], HI) + b2_ref[...]))
    h = _dot(h, w3_ref[...], HI)
    ch = _iota((L, W_BR), 1).astype(F32)
    lo = math.log(HY_TARGET) / HY_SLOW_PCT
    hi = math.log(HY_TARGET) / HY_FAST_PCT
    deltas = jnp.abs(lo + ch * ((hi - lo) / (W_BR - 1)))
    decay = jnp.exp(-(t[:, 0:1]) * deltas)
    o_ref[:, 0:W_BR] = h[:, 0:W_BR] * decay
    o_ref[:, W_BR:2 * W_BR] = h[:, W_BR:2 * W_BR] * decay


def hyena_filter(L, f_w1, f_b1, f_w2, f_b2, f_w3, f_freq):
    pad = lambda a, r, c: jnp.zeros((r, c), F32).at[:a.shape[0], :a.shape[1]].set(a)
    args = (pad(f_w1, LANE, LANE), pad(f_b1[None], 1, LANE), pad(f_w2, LANE, LANE), pad(f_b2[None], 1, LANE),
            pad(f_w3, LANE, 2 * W_BR), pad(f_freq[None], 1, LANE))
    return pl.pallas_call(
        _hyfilt_kernel,
        name="hyena_filter",
        out_shape=jax.ShapeDtypeStruct((L, 2 * W_BR), F32),
        compiler_params=_params(None),
    )(*args)


def _ghat_kernel(f_ref, g_ref, o_ref):
    o_ref[0] = _dot_split_lhs(f_ref, g_ref[0])


def hyena_filter_spectra(filt, L, t):
    nb = L // t
    table = jnp.concatenate([filt[:, :W_BR], filt[:, W_BR:], jnp.zeros((1, W_BR), F32)], axis=0)
    m = np.arange(2 * t)[None, :]
    d = np.arange(-(nb - 1), nb)[:, None]
    off = d * t + np.where(m < t, m, m - 2 * t)
    idx = np.where(m == t, 2 * L, np.where(off >= 0, off, L - off)).astype(np.int32)
    sign = np.where(m == t, 0.0, np.where(m < t, 1.0, -1.0)).astype(np.float32) * np.ones_like(off, np.float32)
    g = table[idx] * sign[:, :, None]
    fwd, _ = _dft_mats(t)
    return pl.pallas_call(
        _ghat_kernel,
        name="hyena_filter_spectra",
        grid=(2 * nb - 1,),
        in_specs=[pl.BlockSpec((2, 2 * t, 2 * t), lambda d: (0, 0, 0)),
                  pl.BlockSpec((1, 2 * t, W_BR), lambda d: (d, 0, 0))],
        out_specs=pl.BlockSpec((1, 2 * t, W_BR), lambda d: (d, 0, 0)),
        out_shape=jax.ShapeDtypeStruct((2 * nb - 1, 2 * t, W_BR), F32),
        compiler_params=_params(("parallel",)),
    )(_split_const(fwd), g)


def _hypre_kernel(nb, cur_ref, prev_ref, next_ref, cw_ref, cb_ref, f_ref, x0_ref, u_ref, uh_ref):
    j = pl.program_id(1)
    cur = cur_ref[0]
    prev_row, next_row = _halo_rows(prev_ref, next_ref, j, nb)
    xm, xp = _shift_rows(cur, prev_row, next_row)
    pc = cb_ref[...] + xm * cw_ref[0:1, :] + cur * cw_ref[1:2, :] + xp * cw_ref[2:3, :]
    u = pc[:, W_BR:2 * W_BR] * pc[:, 2 * W_BR:3 * W_BR]
    x0_ref[0] = pc[:, 0:W_BR]
    u_ref[0] = u
    uh_ref[0, 0] = _dot_split_lhs(f_ref, u)


def _hymain_kernel(nb, t, g_ref, uh_ref, x0_ref, u_ref, bias_ref, inv_ref, o_ref):
    i = pl.program_id(2)
    acc_re = jnp.zeros((t, LANE), F32)
    acc_im = jnp.zeros((t, LANE), F32)
    for j in range(nb):
        g = g_ref[i - j + nb - 1]
        gr, gi = g[0:t], g[t:2 * t]
        ur, ui = uh_ref[0, j, 0:t, :], uh_ref[0, j, t:2 * t, :]
        acc_re += gr * ur - gi * ui
        acc_im += gr * ui + gi * ur
    y = _dot_split_lhs(inv_ref, jnp.concatenate([acc_re, acc_im], axis=0))
    o_ref[0] = x0_ref[0] * (y + bias_ref[...] * u_ref[0])


def hyena_mixer(hy, ghat, conv_w, conv_b, h_bias):
    b_, L, _ = hy.shape
    t = min(HY_BLOCK, L)
    nb = L // t
    fwd, inv = _dft_mats(t)
    prev_spec, next_spec = _halo_specs(t, nb, HY_COLS, 0, lambda s: s)
    x0, u, uh = pl.pallas_call(
        functools.partial(_hypre_kernel, nb),
        name="hyena_conv_dft",
        grid=(b_, nb),
        in_specs=[pl.BlockSpec((1, t, HY_COLS), lambda b, j: (b, j, 0)), prev_spec, next_spec,
                  pl.BlockSpec((3, HY_COLS), lambda b, j: (0, 0)),
                  pl.BlockSpec((1, HY_COLS), lambda b, j: (0, 0)),
                  pl.BlockSpec((2, 2 * t, t), lambda b, j: (0, 0, 0))],
        out_specs=[pl.BlockSpec((1, t, W_BR), lambda b, j: (b, j, 0)),
                   pl.BlockSpec((1, t, W_BR), lambda b, j: (b, j, 0)),
                   pl.BlockSpec((1, 1, 2 * t, W_BR), lambda b, j: (b, j, 0, 0))],
        out_shape=[jax.ShapeDtypeStruct((b_, L, W_BR), F32), jax.ShapeDtypeStruct((b_, L, W_BR), F32),
                   jax.ShapeDtypeStruct((b_, nb, 2 * t, W_BR), F32)],
        compiler_params=_params(("parallel", "parallel")),
    )(hy, hy, hy, conv_w, conv_b[None], _split_const(fwd[:, :t]))
    ncb = W_BR // LANE
    return pl.pallas_call(
        functools.partial(_hymain_kernel, nb, t),
        name="hyena_longconv",
        grid=(ncb, b_, nb),
        in_specs=[pl.BlockSpec((2 * nb - 1, 2 * t, LANE), lambda c, b, i: (0, 0, c)),
                  pl.BlockSpec((1, nb, 2 * t, LANE), lambda c, b, i: (b, 0, 0, c)),
                  pl.BlockSpec((1, t, LANE), lambda c, b, i: (b, i, c)),
                  pl.BlockSpec((1, t, LANE), lambda c, b, i: (b, i, c)),
                  pl.BlockSpec((1, LANE), lambda c, b, i: (0, c)),
                  pl.BlockSpec((2, t, 2 * t), lambda c, b, i: (0, 0, 0))],
        out_specs=pl.BlockSpec((1, t, LANE), lambda c, b, i: (b, i, c)),
        out_shape=jax.ShapeDtypeStruct((b_, L, W_BR), F32),
        compiler_params=_params(("parallel", "parallel", "parallel")),
    )(ghat, uh, x0, u, h_bias[None], _split_const(inv))


def _s5_kernel(nc, u_ref, h0_ref, bb_ref, ap_ref, cb_ref, dsk_ref, wg_ref, y_ref, hf_ref, yacc, state):
    s = pl.program_id(1)
    t = u_ref.shape[1]
    n = S5_LANES

    @pl.when(s == 0)
    def _():
        hf_ref[...] = jnp.zeros_like(hf_ref)

    def phase(d):
        chunk = s if d == 0 else 2 * nc - 1 - s
        first = (s == 0) if d == 0 else (s == nc)
        last = (s == nc - 1) if d == 0 else (s == 2 * nc - 1)

        @pl.when(first)
        def _():
            state[0:1, :] = h0_ref[0, d:d + 1, :]

        u = u_ref[0]
        x = _dot(u.astype(BF16), bb_ref[d])
        row = _iota((t, n), 0)
        st = state[0:1, :]
        pw = (lambda p: p - 1) if d == 0 else (lambda p: SUBLANE - p)
        a_re, a_im = ap_ref[d, pw(1):pw(1) + 1, 0:n], ap_ref[d, pw(1):pw(1) + 1, n:2 * n]
        c_re = a_re * st[:, 0:n] - a_im * st[:, n:2 * n]
        c_im = a_re * st[:, n:2 * n] + a_im * st[:, 0:n]
        entry = 0 if d == 0 else t - 1
        x_re = x[:, 0:n] + jnp.where(row == entry, c_re, 0.0)
        x_im = x[:, n:2 * n] + jnp.where(row == entry, c_im, 0.0)
        n_grp = t // SUBLANE
        x_re, x_im = x_re.reshape(n_grp, SUBLANE, n), x_im.reshape(n_grp, SUBLANE, n)
        sub = _iota((SUBLANE, n), 0)
        for sh in (1, 2, 4):
            valid = (sub >= sh) if d == 0 else (sub < SUBLANE - sh)
            a_re = jnp.where(valid, ap_ref[d, pw(sh):pw(sh) + 1, 0:n], 0.0)[None]
            a_im = jnp.where(valid, ap_ref[d, pw(sh):pw(sh) + 1, n:2 * n], 0.0)[None]
            rot = sh if d == 0 else SUBLANE - sh
            s_re, s_im = pltpu.roll(x_re, rot, 1), pltpu.roll(x_im, rot, 1)
            x_re, x_im = x_re + a_re * s_re - a_im * s_im, x_im + a_re * s_im + a_im * s_re
        p_re, p_im = ap_ref[d, :, 0:n], ap_ref[d, :, n:2 * n]
        out_re, out_im = [None] * n_grp, [None] * n_grp
        c_re = c_im = None
        for g in (range(n_grp) if d == 0 else range(n_grp - 1, -1, -1)):
            g_re, g_im = x_re[g], x_im[g]
            if c_re is not None:
                g_re, g_im = g_re + p_re * c_re - p_im * c_im, g_im + p_re * c_im + p_im * c_re
            out_re[g], out_im[g] = g_re, g_im
            edge = SUBLANE - 1 if d == 0 else 0
            c_re, c_im = g_re[edge:edge + 1], g_im[edge:edge + 1]
        x_re, x_im = jnp.concatenate(out_re, axis=0), jnp.concatenate(out_im, axis=0)
        ex = t - 1 if d == 0 else 0
        state[0:1, 0:n] = x_re[ex:ex + 1, :]
        state[0:1, n:2 * n] = x_im[ex:ex + 1, :]
        yd = _dot(x_re.astype(BF16), cb_ref[d, 0:n, :]) + _dot(x_im.astype(BF16), cb_ref[d, n:2 * n, :])
        rows = pl.ds(pl.multiple_of(chunk * t, t), t)
        if d == 0:
            yacc[rows, :] = dsk_ref[...] * u + yd
        else:
            y = yacc[rows, :] + yd
            y = 0.5 * y * (1.0 + jnp.tanh(math.sqrt(2.0 / math.pi) * (y + 0.044715 * y * y * y)))
            y_ref[0] = y * _sigmoid(_bdot(y, wg_ref[...]))

        @pl.when(last)
        def _():
            hf_ref[0, d:d + 1, :] = state[0:1, :]

    @pl.when(s < nc)
    def _():
        phase(0)

    @pl.when(s >= nc)
    def _():
        phase(1)


def _two_pass_chunk(nc):
    return lambda s: jnp.where(s < nc, s, 2 * nc - 1 - s)


def _two_pass_out(nc):
    return lambda s: jnp.where(s < nc, nc - 1, 2 * nc - 1 - s)


def s5_prepare(lam_re, lam_im, log_step, b_re, b_im, c_re, c_im):
    step = jnp.exp(log_step)[:, :, None]
    mag = jnp.exp(lam_re * step)
    lb_re, lb_im = mag * jnp.cos(lam_im * step), mag * jnp.sin(lam_im * step)
    den = lam_re * lam_re + lam_im * lam_im
    q_re = ((lb_re - 1.0) * lam_re + lb_im * lam_im) / den
    q_im = (lb_im * lam_re - (lb_re - 1.0) * lam_im) / den
    bb_re = q_re[..., None] * b_re - q_im[..., None] * b_im
    bb_im = q_re[..., None] * b_im + q_im[..., None] * b_re
    eye = jnp.eye(S5_GROUPS, dtype=F32)

    def blockdiag_in(m):
        return jnp.einsum('dgni,gh->dgihn', m, eye).reshape(2, W_BR, S5_LANES)

    def blockdiag_out(m):
        return jnp.einsum('dgin,gh->dgnhi', m, eye).reshape(2, S5_LANES, W_BR)

    bblk = jnp.concatenate([blockdiag_in(bb_re), blockdiag_in(bb_im)], axis=-1)
    cblk = jnp.concatenate([blockdiag_out(c_re), -blockdiag_out(c_im)], axis=1)
    j = np.arange(SUBLANE, dtype=np.float32)
    pw = jnp.asarray(np.stack([j + 1.0, SUBLANE - j]))[:, :, None]
    arg_re = (lam_re * step).reshape(2, 1, S5_LANES) * pw
    arg_im = (lam_im * step).reshape(2, 1, S5_LANES) * pw
    apow = jnp.concatenate([jnp.exp(arg_re) * jnp.cos(arg_im), jnp.exp(arg_re) * jnp.sin(arg_im)], axis=-1)
    return bblk.astype(BF16), apow, cblk.astype(BF16)


def s5_mixer(u, h0, bblk, apow, cblk, d_skip, w_glu):
    b_, L, _ = u.shape
    t = min(S5_CHUNK, L)
    nc = L // t
    chunk_of, out_of = _two_pass_chunk(nc), _two_pass_out(nc)
    const = lambda a: pl.BlockSpec(a.shape, lambda b, s: (0,) * a.ndim)
    dsk = d_skip[None]
    wg = w_glu.astype(BF16)
    return pl.pallas_call(
        functools.partial(_s5_kernel, nc),
        name="s5_mixer",
        grid=(b_, 2 * nc),
        in_specs=[pl.BlockSpec((1, t, W_BR), lambda b, s: (b, chunk_of(s), 0)),
                  pl.BlockSpec((1, SUBLANE, 2 * S5_LANES), lambda b, s: (b, 0, 0)),
                  const(bblk), const(apow), const(cblk), const(dsk), const(wg)],
        out_specs=[pl.BlockSpec((1, t, W_BR), lambda b, s: (b, out_of(s), 0)),
                   pl.BlockSpec((1, SUBLANE, 2 * S5_LANES), lambda b, s: (b, 0, 0))],
        out_shape=[jax.ShapeDtypeStruct((b_, L, W_BR), F32),
                   jax.ShapeDtypeStruct((b_, SUBLANE, 2 * S5_LANES), F32)],
        scratch_shapes=[pltpu.VMEM((L, W_BR), F32), pltpu.VMEM((SUBLANE, 2 * S5_LANES), F32)],
        compiler_params=_params(("parallel", "arbitrary")),
    )(u, h0, bblk, apow, cblk, dsk, wg)


def _head_mask(h, shape, axis):
    lane = _iota(shape, axis)
    return (lane >= h * RW_HEAD) & (lane < (h + 1) * RW_HEAD)


def _split3(x):
    hi = x.astype(BF16)
    r1 = x - hi.astype(F32)
    mid = r1.astype(BF16)
    return hi, mid, (r1 - mid.astype(F32)).astype(BF16)


def _stack_heads(x):
    xb = x.astype(BF16)
    zero = jnp.zeros_like(xb)
    return jnp.concatenate([jnp.where(_head_mask(h, xb.shape, 1), xb, zero) for h in range(RW_HEADS)], axis=0)


def _split2(x):
    hi = x.astype(BF16)
    return hi, (x - hi.astype(F32)).astype(BF16)


def _dot_exact_rhs(a, b):
    hi, lo = _split2(a)
    return _dot(hi, b) + _dot(lo, b)


def _dot_split(a, b_hi, b_lo):
    hi, lo = _split2(a)
    return _dot(hi, b_hi) + _dot(hi, b_lo) + _dot(lo, b_hi)


RW_INV_BASE = 8


def _cat_to_blockdiag(x):
    t, n = x.shape
    xb = x.astype(BF16)
    tiled = jnp.concatenate([xb] * (n // t), axis=0)
    same = (_iota((n, n), 0) // t) == (_iota((n, n), 1) // t)
    return jnp.where(same, tiled, jnp.zeros_like(tiled))


def _tri_inverse(nmat):
    t = nmat[0].shape[0]
    row, col = _iota(nmat[0].shape, 0), _iota(nmat[0].shape, 1) % t
    same = lambda s: (row // s) == (col // s)
    mul = lambda x, y: _dot(x.astype(BF16), _cat_to_blockdiag(y))
    base = [jnp.where(same(RW_INV_BASE), x, 0.0) for x in nmat]
    eye = jnp.where(row == col, 1.0, 0.0)
    m = [eye - x for x in base]
    p = base
    for _ in range(RW_INV_BASE.bit_length() - 2):
        p = [mul(x, x) for x in p]
        m = [x + mul(x, y) for x, y in zip(m, p)]
    s = 2 * RW_INV_BASE
    while s <= t:
        off = same(s) & jnp.logical_not(same(s // 2))
        mc = [mul(x, jnp.where(off, y, 0.0)) for x, y in zip(m, nmat)]
        m = [x - mul(y, x) for x, y in zip(m, mc)]
        s *= 2
    return m


def _wkv_local(d, r, logw, kd, v, kk, a):
    t = r[0].shape[0]
    bf = lambda x: x.astype(BF16)
    each = lambda f, *ls: [f(*xs) for xs in zip(*ls)]
    ri, ci = _iota((t, t), 0), _iota((t, t), 1)
    tri = ((ci <= ri) if d == 0 else (ci >= ri)).astype(BF16)
    row, col = _iota((t, RW_HEADS * t), 0), _iota((t, RW_HEADS * t), 1) % t
    incl = (col <= row) if d == 0 else (col >= row)
    strict = (col < row) if d == 0 else (col > row)
    dn = W_BR
    diag = _iota((dn, dn), 0) == _iota((dn, dn), 1)

    cs = each(lambda x: sum(_dot(tri, part) for part in _split3(x)), logw)
    g_tot = each(lambda x: jnp.exp(jnp.sum(x, axis=0, keepdims=True)), logw)
    g_inv = each(lambda c: jnp.exp(-c), cs)
    bt = each(lambda k_, a_, g: k_ * a_ * g, kk, a, g_inv)
    kt = each(lambda k_, g: k_ * g, kd, g_inv)
    qa = each(lambda k_, c, w: k_ * jnp.exp(c - w), kk, cs, logw)
    rh = each(lambda r_, c: r_ * jnp.exp(c), r, cs)
    bts, kts, vs = each(_stack_heads, bt), each(_stack_heads, kt), each(_stack_heads, v)
    bhs = each(lambda x, g: _stack_heads(x * g), bt, g_tot)
    khs = each(lambda x, g: _stack_heads(x * g), kt, g_tot)
    a_ab = each(lambda q, b: jnp.where(strict, _dot_nt(bf(q), b), 0.0), qa, bts)
    a_ak = each(lambda q, k_: bf(jnp.where(strict, _dot_nt(bf(q), k_), 0.0)), qa, kts)
    a_rb = each(lambda q, b: bf(jnp.where(incl, _dot_nt(bf(q), b), 0.0)), rh, bts)
    a_rk = each(lambda q, k_: bf(jnp.where(incl, _dot_nt(bf(q), k_), 0.0)), rh, kts)
    m = each(bf, _tri_inverse(a_ab))
    x1 = each(lambda a_, v_: _stack_heads(_dot(a_, v_)), a_ak, vs)
    ul = each(lambda m_, x: _stack_heads(-_dot(m_, x)), m, x1)
    w0 = each(lambda m_, q: _stack_heads(_dot(m_, _stack_heads(q))), m, qa)
    yloc = each(lambda ak, v_, ab, u: _dot(ak, v_) + _dot(ab, u), a_rk, vs, a_rb, ul)
    yh = each(lambda r_, ab, w: r_ - _dot(ab, w), rh, a_rb, w0)
    p_bd = each(lambda g, b, w: jnp.where(diag, g, 0.0) - _dot_tn(b, w), g_tot, bhs, w0)
    q_bd = each(lambda b, u, k_, v_: _dot_tn(b, u) + _dot_tn(k_, v_), bhs, ul, khs, vs)
    return list(zip(yloc, yh, p_bd, q_bd))


def _rwkv_kernel(ng, cur_ref, prev_ref, next_ref, h0_ref, mu_ref, w0_ref, a0_ref, lrh_ref, lrl_ref,
                 kk_ref, ka_ref, rk_ref, lnw_ref, lnb_ref, y_ref, hf_ref, yacc, hst):
    s = pl.program_id(1)
    t = cur_ref.shape[1]
    n = W_BR
    ri, ci = _iota((n, n), 0), _iota((n, n), 1)
    head_sum = ((ri // RW_HEAD) == (ci // RW_HEAD)).astype(BF16)

    def phase(d):
        chunk = s if d == 0 else 2 * ng - 1 - s
        first = (s == 0) if d == 0 else (s == ng)
        last = (s == ng - 1) if d == 0 else (s == 2 * ng - 1)

        @pl.when(first)
        def _():
            hst[...] = h0_ref[0, d]

        cur = cur_ref[0]
        prev_row, next_row = _halo_rows(prev_ref, next_ref, chunk, ng)
        xm, xp = _shift_rows(cur, prev_row, next_row)
        p = cur + (0.5 * (xm + xp) - cur) * mu_ref[...]
        r, k, v, lr = p[:, 0:n], p[:, n:2 * n], p[:, 2 * n:3 * n], p[:, 3 * n:4 * n]
        kk = k * kk_ref[...]
        kk = kk * lax.rsqrt(jnp.maximum(_dot_exact_rhs(kk * kk, head_sum), 1e-24))
        lane = _iota(lr.shape, 1)
        feats = jnp.where(lane < 2 * RW_W_RANK, jnp.tanh(lr),
                          jnp.where(lane < 2 * (RW_W_RANK + RW_A_RANK), lr, _sigmoid(lr)))
        cols = (2 if d == 0 else 3) * n
        low = _dot_split(feats, lrh_ref[d, :, 0:cols], lrl_ref[d, :, 0:cols])
        w = w0_ref[d:d + 1, :] + low[:, 0:n]
        logw = -math.exp(-0.5) * _sigmoid(w)
        a = _sigmoid(a0_ref[d:d + 1, :] + low[:, n:2 * n])
        kd = k * (1.0 + (a - 1.0) * ka_ref[...])
        nsub = t // RW_SUB
        subs = lambda x: [x[q * RW_SUB:(q + 1) * RW_SUB] for q in range(nsub)]
        local = _wkv_local(d, subs(r), subs(logw), subs(kd), subs(v), subs(kk), subs(a))
        h = hst[...]
        ys = [None] * nsub
        for q in (range(nsub) if d == 0 else range(nsub - 1, -1, -1)):
            yloc, yh, p_bd, q_bd = local[q]
            hb = h.astype(BF16)
            ys[q] = yloc + _dot(yh.astype(BF16), hb)
            h = _dot(p_bd.astype(BF16), hb) + q_bd
        hst[...] = h
        yd = jnp.concatenate(ys, axis=0)
        rows = pl.ds(pl.multiple_of(chunk * t, t), t)
        if d == 0:
            yacc[rows, :] = yd
        else:
            y = yacc[rows, :] + yd
            mean = _dot_exact_rhs(y, head_sum) * (1.0 / RW_HEAD)
            yc = y - mean
            var = _dot_exact_rhs(yc * yc, head_sum) * (1.0 / RW_HEAD)
            y = yc * lax.rsqrt(var + RW_GN_EPS) * lnw_ref[...] + lnb_ref[...]
            bonus = _dot_exact_rhs(r * k * rk_ref[...], head_sum) * v
            y_ref[0] = (y + bonus) * low[:, 2 * n:3 * n]

        @pl.when(last)
        def _():
            hf_ref[0, d] = hst[...]

    @pl.when(s < ng)
    def _():
        phase(0)

    @pl.when(s >= ng)
    def _():
        phase(1)


def rwkv_prepare(mu, w0, w2, a0, a2, g2, k_k, k_a, r_k, ln_w, ln_b):
    n = W_BR
    mu_p = jnp.zeros((1, RW_PAD), F32).at[0, :RW_COLS].set(mu)
    w2_p = jnp.zeros((2, n, n), F32)
    a2_p = jnp.zeros((2, n, n), F32)
    for d in range(2):
        w2_p = w2_p.at[d, d * RW_W_RANK:(d + 1) * RW_W_RANK].set(w2[d])
        o = 2 * RW_W_RANK + d * RW_A_RANK
        a2_p = a2_p.at[d, o:o + RW_A_RANK].set(a2[d])
    o = 2 * RW_W_RANK + 2 * RW_A_RANK
    g2_p = jnp.zeros((n, n), F32).at[o:o + RW_G_RANK].set(g2)
    low = jnp.concatenate([w2_p, a2_p, jnp.broadcast_to(g2_p, (2, n, n))], axis=-1)
    low_hi = low.astype(BF16)
    low_lo = (low - low_hi.astype(F32)).astype(BF16)
    return (mu_p, w0, a0, low_hi, low_lo, k_k[None], k_a[None], r_k.reshape(1, n), ln_w[None], ln_b[None])


def rwkv_mixer(rw, h0, prm):
    b_, L, _ = rw.shape
    t = min(RW_STEP, L)
    ng = L // t
    chunk_of, out_of = _two_pass_chunk(ng), _two_pass_out(ng)
    prev_spec, next_spec = _halo_specs(t, ng, RW_PAD, 0, chunk_of)
    const = lambda a: pl.BlockSpec(a.shape, lambda b, s: (0,) * a.ndim)
    st_spec = pl.BlockSpec((1, 2, W_BR, W_BR), lambda b, s: (b, 0, 0, 0))
    return pl.pallas_call(
        functools.partial(_rwkv_kernel, ng),
        name="rwkv_mixer",
        grid=(b_, 2 * ng),
        in_specs=[pl.BlockSpec((1, t, RW_PAD), lambda b, s: (b, chunk_of(s), 0)), prev_spec, next_spec, st_spec]
                 + [const(a) for a in prm],
        out_specs=[pl.BlockSpec((1, t, W_BR), lambda b, s: (b, out_of(s), 0)), st_spec],
        out_shape=[jax.ShapeDtypeStruct((b_, L, W_BR), F32), jax.ShapeDtypeStruct((b_, 2, W_BR, W_BR), F32)],
        scratch_shapes=[pltpu.VMEM((L, W_BR), F32), pltpu.VMEM((W_BR, W_BR), F32)],
        compiler_params=_params(("parallel", "arbitrary")),
    )(rw, rw, rw, h0, *prm)


def _m2_kernel(nc, z_ref, cur_ref, prev_ref, next_ref, dt_ref, h0_ref, cw_ref, cb_ref, ex_ref, a_ref, dtb_ref,
               dsk_ref, nw_ref, y_ref, hf_ref, yacc, hst):
    s = pl.program_id(1)
    t = cur_ref.shape[1]
    n = W_BR
    ns = M2_STATE
    ri, ci = _iota((t, t), 0), _iota((t, t), 1)

    def phase(d):
        chunk = s if d == 0 else 2 * nc - 1 - s
        first = (s == 0) if d == 0 else (s == nc)
        last = (s == nc - 1) if d == 0 else (s == 2 * nc - 1)

        @pl.when(first)
        def _():
            hst[...] = h0_ref[0, d]

        cur = cur_ref[0]
        prev_row, next_row = _halo_rows(prev_ref, next_ref, chunk, nc)
        xm, xp = _shift_rows(cur, prev_row, next_row)
        xbc = _silu(cb_ref[...] + xm * cw_ref[0:1, :] + cur * cw_ref[1:2, :] + xp * cw_ref[2:3, :])
        xs, bm, cm = xbc[:, 0:n], xbc[:, n:2 * n], xbc[:, 2 * n:3 * n]
        raw = sum(_dot(part, ex_ref[d]) for part in _split3(dt_ref[0])) + dtb_ref[d:d + 1, :]
        dtd = jnp.maximum(raw, 0.0) + jnp.log(1.0 + jnp.exp(-jnp.abs(raw)))
        da = dtd * a_ref[d:d + 1, :]
        incl = (ci <= ri) if d == 0 else (ci >= ri)
        tri = incl.astype(BF16)
        da_parts = _split3(da)
        acs = sum(_dot(tri, part) for part in da_parts)
        tot = jnp.sum(da, axis=0, keepdims=True)
        xdt = xs * dtd
        xdt_b = xdt.astype(BF16)
        bm_b, cm_b = bm.astype(BF16), cm.astype(BF16)
        sel = jnp.where(_iota((SUBLANE, n), 1) // M2_HEADDIM == _iota((SUBLANE, n), 0), 1.0 / M2_HEADDIM,
                        0.0).astype(BF16)
        acs_t = sum(_dot_nt(sel, part) for part in _split3(acs))
        ydiag = jnp.zeros((t, n), F32)
        for g in range(M2_GROUPS):
            cb = _dot_nt(cm_b[:, g * ns:(g + 1) * ns], bm_b[:, g * ns:(g + 1) * ns])
            for h in range(g * (M2_HEADS // M2_GROUPS), (g + 1) * (M2_HEADS // M2_GROUPS)):
                seg = acs[:, h * M2_HEADDIM:h * M2_HEADDIM + 1] - acs_t[h:h + 1, :]
                scores = cb * jnp.exp(jnp.where(incl, seg, -jnp.inf))
                ydiag = jnp.where(_head_mask(h, (t, n), 1), _dot(scores.astype(BF16), xdt_b), ydiag)
        h_in = hst[...]
        h_b = h_in.astype(BF16)
        lane = _iota((t, n), 1)
        yoff = jnp.where(lane < n // M2_GROUPS, _dot(cm_b[:, 0:ns], h_b), _dot(cm_b[:, ns:2 * ns], h_b))
        yd = ydiag + jnp.exp(acs) * yoff
        xdec = (xdt * jnp.exp(tot - acs)).astype(BF16)
        lane_s = _iota((ns, n), 1)
        new = jnp.where(lane_s < n // M2_GROUPS, _dot_tn(bm_b[:, 0:ns], xdec), _dot_tn(bm_b[:, ns:2 * ns], xdec))
        hst[...] = h_in * jnp.exp(tot) + new
        rows = pl.ds(pl.multiple_of(chunk * t, t), t)
        if d == 0:
            yacc[rows, :] = yd + dsk_ref[...] * xs
        else:
            y = (yacc[rows, :] + yd) * _silu(z_ref[0])
            y_ref[0] = _rms(y) * nw_ref[...]

        @pl.when(last)
        def _():
            hf_ref[0, d] = hst[...]

    @pl.when(s < nc)
    def _():
        phase(0)

    @pl.when(s >= nc)
    def _():
        phase(1)


def mamba_prepare(conv_w, conv_b, a_log, dt_bias, d_skip, norm_w):
    n = W_BR
    rep = lambda v: jnp.repeat(v, M2_HEADDIM, axis=-1)
    expand = jnp.zeros((2, LANE, n), F32)
    for d in range(2):
        for h in range(M2_HEADS):
            expand = expand.at[d, d * M2_HEADS + h, h * M2_HEADDIM:(h + 1) * M2_HEADDIM].set(1.0)
    return (conv_w, conv_b[None], expand.astype(BF16), rep(-jnp.exp(a_log)), rep(dt_bias), rep(d_skip)[None],
            norm_w[None])


def mamba_mixer(z, xbc, dt, h0, prm):
    b_, L, _ = z.shape
    t = min(M2_CHUNK, L)
    nc = L // t
    chunk_of, out_of = _two_pass_chunk(nc), _two_pass_out(nc)
    prev_spec, next_spec = _halo_specs(t, nc, M2_XBC, 0, chunk_of)
    const = lambda a: pl.BlockSpec(a.shape, lambda b, s: (0,) * a.ndim)
    st_spec = pl.BlockSpec((1, 2, M2_STATE, W_BR), lambda b, s: (b, 0, 0, 0))
    seq = lambda w: pl.BlockSpec((1, t, w), lambda b, s: (b, chunk_of(s), 0))
    return pl.pallas_call(
        functools.partial(_m2_kernel, nc),
        name="mamba_mixer",
        grid=(b_, 2 * nc),
        in_specs=[seq(W_BR), seq(M2_XBC), prev_spec, next_spec, seq(LANE), st_spec] + [const(a) for a in prm],
        out_specs=[pl.BlockSpec((1, t, W_BR), lambda b, s: (b, out_of(s), 0)), st_spec],
        out_shape=[jax.ShapeDtypeStruct((b_, L, W_BR), F32),
                   jax.ShapeDtypeStruct((b_, 2, M2_STATE, W_BR), F32)],
        scratch_shapes=[pltpu.VMEM((L, W_BR), F32), pltpu.VMEM((M2_STATE, W_BR), F32)],
        compiler_params=_params(("parallel", "arbitrary")),
    )(z, xbc, xbc, xbc, dt, h0, *prm)


def _merge_kernel(x_ref, g_ref, sh_ref, sc_ref, gt_ref, y0_ref, y1_ref, y2_ref, y3_ref, wg_ref, wb_ref, wo_ref,
                  o_ref):
    x = x_ref[...]
    h = _modulate(x, g_ref[...], sh_ref[0], sc_ref[0]).astype(BF16)
    acc = jnp.zeros(x.shape, F32)
    for i, y_ref in enumerate((y0_ref, y1_ref, y2_ref, y3_ref)):
        gate = _sigmoid(_dot(h, wg_ref[i]))
        acc += gate * _dot(y_ref[...].astype(BF16), wb_ref[i])
    o_ref[...] = x + gt_ref[0] * _dot(acc.astype(BF16), wo_ref[...])


def merge_branches(xt, g, shift, scale, gate, ys, w_gate, w_branch, w_out, rows_per_mod, tm):
    n = xt.shape[0]
    tiles_per_mod = rows_per_mod // tm
    mod_spec = pl.BlockSpec((1, 1, D_MODEL), lambda i: (i // tiles_per_mod, 0, 0))
    const = lambda a: pl.BlockSpec(a.shape, lambda i: (0,) * a.ndim)
    tok = lambda w: pl.BlockSpec((tm, w), lambda i: (i, 0))
    return pl.pallas_call(
        _merge_kernel,
        name="merge_branches",
        grid=(n // tm,),
        in_specs=[tok(D_MODEL), const(g), mod_spec, mod_spec, mod_spec] + [tok(W_BR)] * N_BRANCH
                 + [const(w_gate), const(w_branch), const(w_out)],
        out_specs=tok(D_MODEL),
        out_shape=jax.ShapeDtypeStruct((n, D_MODEL), F32),
        compiler_params=_params(("parallel",)),
    )(xt, g, shift, scale, gate, *ys, w_gate, w_branch, w_out)


def _router_kernel(x_ref, g_ref, sh_ref, sc_ref, wr_ref, comb_ref):
    u = _modulate(x_ref[...], g_ref[...], sh_ref[0], sc_ref[0])
    logits = _dot_split(u, wr_ref[0], wr_ref[1])
    lane = _iota(logits.shape, 1).astype(F32)
    neg = -jnp.inf
    is_grp = (lane >= MOE_EXPERTS) & (lane < MOE_EXPERTS + MOE_GROUPS)
    gl = jnp.where(is_grp, logits, neg)
    gmax = jnp.max(gl, axis=-1, keepdims=True)
    grp_p = 1.0 / jnp.sum(jnp.exp(gl - gmax), axis=-1, keepdims=True)
    grp_idx = jnp.min(jnp.where(gl == gmax, lane, 4.0 * LANE), axis=-1, keepdims=True) - MOE_EXPERTS
    in_grp = (lane >= grp_idx * MOE_PER_GROUP) & (lane < (grp_idx + 1) * MOE_PER_GROUP)
    el = jnp.where(in_grp, logits, neg)
    v1 = jnp.max(el, axis=-1, keepdims=True)
    i1 = jnp.min(jnp.where(el == v1, lane, 4.0 * LANE), axis=-1, keepdims=True)
    el2 = jnp.where(lane == i1, neg, el)
    v2 = jnp.max(el2, axis=-1, keepdims=True)
    i2 = jnp.min(jnp.where(el2 == v2, lane, 4.0 * LANE), axis=-1, keepdims=True)
    e2 = jnp.exp(v2 - v1)
    w1 = grp_p / (1.0 + e2)
    comb = jnp.where(lane == i1, w1, jnp.where(lane == i2, w1 * e2, 0.0))
    comb_ref[...] = jnp.where(lane == MOE_GROUP_LANE, grp_idx, comb)


def moe_router(xt, g, shift, scale, w_route, rows_per_mod, tm):
    n = xt.shape[0]
    tiles_per_mod = rows_per_mod // tm
    mod_spec = pl.BlockSpec((1, 1, D_MODEL), lambda i: (i // tiles_per_mod, 0, 0))
    return pl.pallas_call(
        _router_kernel,
        name="moe_router",
        grid=(n // tm,),
        in_specs=[pl.BlockSpec((tm, D_MODEL), lambda i: (i, 0)), pl.BlockSpec((1, D_MODEL), lambda i: (0, 0)),
                  mod_spec, mod_spec, pl.BlockSpec((2, D_MODEL, LANE), lambda i: (0, 0, 0))],
        out_specs=pl.BlockSpec((tm, LANE), lambda i: (i, 0)),
        out_shape=jax.ShapeDtypeStruct((n, LANE), F32),
        compiler_params=_params(("parallel",)),
    )(xt, g, shift, scale, w_route)


def _moe_kernel(final_norm, meta_ref, x_ref, g_ref, sh_ref, sc_ref, gt_ref, comb_ref, wg_ref, wu_ref, wd_ref, nf_ref,
                o_ref, pt_sc, xs_sc, cs_sc, acc):
    i = pl.program_id(0)
    e = pl.program_id(1)
    tm = x_ref.shape[0]

    @pl.when(e == 0)
    def _():
        u = _modulate(x_ref[...], g_ref[...], sh_ref[0], sc_ref[0]).astype(BF16)
        comb = comb_ref[...]
        lane = _iota((tm, LANE), 1)
        member = comb[:, MOE_GROUP_LANE:MOE_GROUP_LANE + 1] == lane.astype(F32)
        tri = jnp.where(_iota((tm, tm), 1) <= _iota((tm, tm), 0), 1.0, 0.0).astype(BF16)
        rank = _dot(tri, jnp.where(member, 1.0, 0.0).astype(BF16))
        offs = jnp.zeros((tm, LANE), jnp.int32)
        for grp in range(MOE_GROUPS):
            offs = jnp.where(lane == grp, meta_ref[i, grp], offs)
        posmat = jnp.where(member, offs.astype(F32) + rank - 1.0, 0.0)
        pos_col = jnp.sum(posmat, axis=-1, keepdims=True)
        ones = jnp.ones((SUBLANE, LANE), BF16)
        pos_row = sum(_dot_nt(ones, part) for part in _split3(posmat))[0:1, :]
        pt_sc[...] = jnp.where(pos_col == _iota((tm, tm), 1).astype(F32), 1.0, 0.0).astype(BF16)
        p = jnp.where(pos_row == _iota((tm, tm), 0).astype(F32), 1.0, 0.0).astype(BF16)
        xs_sc[...] = _dot(p, u).astype(BF16)
        cs_sc[...] = sum(_dot(p, part) for part in _split3(comb))
        acc[...] = jnp.zeros_like(acc)

    grp = e // MOE_PER_GROUP
    start = meta_ref[i, grp]
    count = meta_ref[i, MOE_GROUPS + grp]
    lo = start // MOE_SUB
    hi = jnp.where(count > 0, (start + count + MOE_SUB - 1) // MOE_SUB, lo)

    w_gate, w_up, w_down = wg_ref[0, 0].astype(BF16), wu_ref[0, 0].astype(BF16), wd_ref[0, 0].astype(BF16)

    def body(j, carry):
        rows = pl.ds(pl.multiple_of(j * MOE_SUB, MOE_SUB), MOE_SUB)
        xs = xs_sc[rows, :]
        h = _silu(_dot(xs, w_gate)) * _dot(xs, w_up)
        cs = cs_sc[rows, :]
        w = jnp.sum(jnp.where(_iota(cs.shape, 1) == e, cs, 0.0), axis=-1, keepdims=True)
        acc[rows, :] += w * _dot(h.astype(BF16), w_down)
        return carry

    lax.fori_loop(lo, hi, body, 0)

    @pl.when(e == MOE_EXPERTS - 1)
    def _():
        y = x_ref[...] + gt_ref[0] * _dot(pt_sc[...], acc[...].astype(BF16))
        o_ref[...] = _rms(y) * nf_ref[...] if final_norm else y


def moe_experts(xt, g, shift, scale, gate, comb, layer, w_gate, w_up, w_down, norm_final, final_norm, rows_per_mod,
                tm):
    n = xt.shape[0]
    tiles_per_mod = rows_per_mod // tm
    gid = comb[:, MOE_GROUP_LANE].astype(jnp.int32).reshape(n // tm, tm)
    counts = jnp.sum(gid[:, :, None] == jnp.arange(MOE_GROUPS, dtype=jnp.int32), axis=1, dtype=jnp.int32)
    meta = jnp.concatenate([jnp.cumsum(counts, axis=1) - counts, counts], axis=1)
    mod_spec = pl.BlockSpec((1, 1, D_MODEL), lambda i, e, m: (i // tiles_per_mod, 0, 0))
    return pl.pallas_call(
        functools.partial(_moe_kernel, final_norm),
        name="moe_experts",
        grid_spec=pltpu.PrefetchScalarGridSpec(
            num_scalar_prefetch=1,
            grid=(n // tm, MOE_EXPERTS),
            in_specs=[pl.BlockSpec((tm, D_MODEL), lambda i, e, m: (i, 0)),
                      pl.BlockSpec((1, D_MODEL), lambda i, e, m: (0, 0)), mod_spec, mod_spec, mod_spec,
                      pl.BlockSpec((tm, LANE), lambda i, e, m: (i, 0)),
                      pl.BlockSpec((1, 1, D_MODEL, MOE_FF), lambda i, e, m: (layer, e, 0, 0)),
                      pl.BlockSpec((1, 1, D_MODEL, MOE_FF), lambda i, e, m: (layer, e, 0, 0)),
                      pl.BlockSpec((1, 1, MOE_FF, D_MODEL), lambda i, e, m: (layer, e, 0, 0)),
                      pl.BlockSpec((1, D_MODEL), lambda i, e, m: (0, 0))],
            out_specs=pl.BlockSpec((tm, D_MODEL), lambda i, e, m: (i, 0)),
            scratch_shapes=[pltpu.VMEM((tm, tm), BF16), pltpu.VMEM((tm, D_MODEL), BF16),
                            pltpu.VMEM((tm, LANE), F32), pltpu.VMEM((tm, D_MODEL), F32)]),
        out_shape=jax.ShapeDtypeStruct((n, D_MODEL), F32),
        compiler_params=_params(("parallel", "arbitrary")),
    )(meta, xt, g, shift, scale, gate, comb, w_gate, w_up, w_down, norm_final)


def _mix_weights(w_in):
    o = 0
    parts = []
    for cols, padded in ((HY_COLS, HY_COLS), (S5_COLS, S5_COLS), (RW_COLS, RW_PAD), (M2_COLS, M2_PAD)):
        parts.append(jnp.pad(w_in[:, o:o + cols], ((0, 0), (0, padded - cols))))
        o += cols
    w_gate = w_in[:, o:].reshape(D_MODEL, N_BRANCH, D_MODEL).transpose(1, 0, 2)
    return jnp.concatenate(parts, axis=1).astype(BF16), w_gate.astype(BF16)


def kernel(x, c, ctx, c_ctx, mod_w, mod_b, norm_mix, norm_ffn, w_in, hy_conv_w, hy_conv_b, hy_f_w1, hy_f_b1, hy_f_w2, hy_f_b2, hy_f_w3, hy_f_freq, hy_bias, s5_lam_re, s5_lam_im, s5_log_step, s5_b_re, s5_b_im, s5_c_re, s5_c_im, s5_d, s5_w_glu, rw_mu, rw_w0, rw_w2, rw_a0, rw_a2, rw_g2, rw_k_k, rw_k_a, rw_r_k, rw_ln_w, rw_ln_b, m2_conv_w, m2_conv_b, m2_a_log, m2_dt_bias, m2_d, m2_norm_w, w_branch, w_out, moe_w_group, moe_w_expert, moe_w_gate, moe_w_up, moe_w_down, norm_final):
    b_, L, _ = x.shape
    lc = ctx.shape[1]
    depth = mod_w.shape[0]
    tm = 512
    tmc = min(tm, lc)

    cvec = jnp.zeros((SUBLANE, D_MODEL), F32).at[:b_].set(c).at[b_].set(c_ctx)
    mod = adaln_mod(cvec, mod_w, mod_b)

    xt = x.reshape(b_ * L, D_MODEL)
    ct = ctx.reshape(b_ * lc, D_MODEL)
    for l in range(depth):
        ctx_out = l < depth - 1
        mx = mod[l, :b_].reshape(b_, 1, N_MOD, D_MODEL)
        mc = mod[l, b_:b_ + 1].reshape(1, 1, N_MOD, D_MODEL)
        sh1, sc1, g1, sh2, sc2, g2 = (mx[:, :, i] for i in range(N_MOD))
        csh1, csc1, cg1, csh2, csc2, cg2 = (mc[:, :, i] for i in range(N_MOD))
        w_mix, w_gate = _mix_weights(w_in[l])
        nm, nf = norm_mix[l][None], norm_ffn[l][None]

        s5_prm = s5_prepare(s5_lam_re[l], s5_lam_im[l], s5_log_step[l], s5_b_re[l], s5_b_im[l], s5_c_re[l],
                            s5_c_im[l])
        rw_prm = rwkv_prepare(rw_mu[l], rw_w0[l], rw_w2[l], rw_a0[l], rw_a2[l], rw_g2[l], rw_k_k[l], rw_k_a[l],
                              rw_r_k[l], rw_ln_w[l], rw_ln_b[l])
        m2_prm = mamba_prepare(m2_conv_w[l], m2_conv_b[l], m2_a_log[l], m2_dt_bias[l], m2_d[l], m2_norm_w[l])
        hy_f = (hy_f_w1[l], hy_f_b1[l], hy_f_w2[l], hy_f_b2[l], hy_f_w3[l], hy_f_freq[l])

        def mixers(tokens, n_tok, shift, scale, rows_per_mod, tile, states, want_hyena):
            hy, s5, rw, m2z, m2x, m2dt = in_projection(tokens, nm, shift, scale, w_mix, rows_per_mod, tile)
            seq = lambda a: a.reshape(b_, n_tok, a.shape[-1])
            y_hy = None
            if want_hyena:
                t = min(HY_BLOCK, n_tok)
                ghat = hyena_filter_spectra(hyena_filter(n_tok, *hy_f), n_tok, t)
                y_hy = hyena_mixer(seq(hy), ghat, hy_conv_w[l], hy_conv_b[l], hy_bias[l])
            y_s5, s5_h = s5_mixer(seq(s5), states[0], *s5_prm, s5_d[l], s5_w_glu[l])
            y_rw, rw_h = rwkv_mixer(seq(rw), states[1], rw_prm)
            y_m2, m2_h = mamba_mixer(seq(m2z), seq(m2x), seq(m2dt), states[2], m2_prm)
            flat = lambda a: None if a is None else a.reshape(b_ * n_tok, W_BR)
            return [flat(y_hy), flat(y_s5), flat(y_rw), flat(y_m2)], (s5_h, rw_h, m2_h)

        zero_states = (jnp.zeros((b_, SUBLANE, 2 * S5_LANES), F32), jnp.zeros((b_, 2, W_BR, W_BR), F32),
                       jnp.zeros((b_, 2, M2_STATE, W_BR), F32))
        ys_c, ctx_states = mixers(ct, lc, csh1, csc1, b_ * lc, tmc, zero_states, ctx_out)
        ys_x, _ = mixers(xt, L, sh1, sc1, L, tm, ctx_states, True)

        w_route = jnp.zeros((D_MODEL, LANE), F32)
        w_route = w_route.at[:, :MOE_EXPERTS].set(moe_w_expert[l].transpose(1, 0, 2).reshape(D_MODEL, MOE_EXPERTS))
        w_route = w_route.at[:, MOE_EXPERTS:MOE_EXPERTS + MOE_GROUPS].set(moe_w_group[l])
        w_route = jnp.stack(_split2(w_route))
        wb, wo = w_branch[l].astype(BF16), w_out[l].astype(BF16)
        nfin = norm_final[None]

        def channel_mix(tokens, ys, mods, rows_per_mod, tile, final):
            s1, c1, gt1, s2, c2, gt2 = mods
            t1 = merge_branches(tokens, nm, s1, c1, gt1, ys, w_gate, wb, wo, rows_per_mod, tile)
            comb = moe_router(t1, nf, s2, c2, w_route, rows_per_mod, tile)
            moe_tile = min(MOE_TILE, rows_per_mod)
            return moe_experts(t1, nf, s2, c2, gt2, comb, l, moe_w_gate, moe_w_up, moe_w_down, nfin, final,
                               rows_per_mod, moe_tile)

        xt = channel_mix(xt, ys_x, (sh1, sc1, g1, sh2, sc2, g2), L, tm, l == depth - 1)
        if ctx_out:
            ct = channel_mix(ct, ys_c, (csh1, csc1, cg1, csh2, csc2, cg2), b_ * lc, tmc, False)
    return xt.reshape(b_, L, D_MODEL)
```

```python
import functools
import math

import numpy as np
import jax
import jax.numpy as jnp
from jax import lax
from jax.experimental import pallas as pl
from jax.experimental.pallas import tpu as pltpu

F32 = jnp.float32
BF16 = jnp.bfloat16
HI = lax.Precision.HIGHEST

D_MODEL = 1024
W_BR = 256
N_BRANCH = 4
N_MOD = 6
NORM_EPS = 1e-6
GRID_W = 64

HY_BANDS = 8
HY_HID = 64
HY_TARGET = 1e-2
HY_FAST_PCT = 0.3
HY_SLOW_PCT = 1.5
HY_BLOCK = 512

S5_GROUP = 16
S5_GROUPS = 16
S5_STATE = 64
S5_LANES = S5_GROUPS * S5_STATE
S5_CHUNK = 512

RW_HEAD = 64
RW_HEADS = 4
RW_W_RANK = 32
RW_A_RANK = 32
RW_G_RANK = 64
RW_GN_EPS = 64e-5
RW_SUB = 64
RW_STEP = 512

M2_HEADDIM = 64
M2_HEADS = 4
M2_GROUPS = 2
M2_STATE = 128
M2_CHUNK = 256

MOE_GROUPS = 4
MOE_PER_GROUP = 4
MOE_EXPERTS = 16
MOE_FF = 512
MOE_GROUP_LANE = MOE_EXPERTS
MOE_TILE = 1024
MOE_SUB = 128

HY_COLS = 3 * W_BR
S5_COLS = W_BR
RW_COLS = 3 * W_BR + 2 * RW_W_RANK + 2 * RW_A_RANK + RW_G_RANK
M2_XBC = W_BR + 2 * M2_GROUPS * M2_STATE
M2_COLS = W_BR + M2_XBC + 2 * M2_HEADS
RW_PAD = 1024
M2_PAD = 1152
LANE = 128
SUBLANE = 8

VMEM_LIMIT = 56 * 1024 * 1024


def _dot(a, b, prec=None):
    return jnp.dot(a, b, preferred_element_type=F32, precision=prec)


def _dot_nt(a, b, prec=None):
    return lax.dot_general(a, b, (((1,), (1,)), ((), ())), preferred_element_type=F32, precision=prec)


def _dot_tn(a, b, prec=None):
    return lax.dot_general(a, b, (((0,), (0,)), ((), ())), preferred_element_type=F32, precision=prec)


def _bdot(a, b):
    return jnp.dot(a.astype(BF16), b.astype(BF16), preferred_element_type=F32)


def _sigmoid(x):
    return 1.0 / (1.0 + jnp.exp(-x))


def _silu(x):
    return x * _sigmoid(x)


def _params(sem):
    return pltpu.CompilerParams(dimension_semantics=sem, vmem_limit_bytes=VMEM_LIMIT)


def _rms(x):
    return x * lax.rsqrt(jnp.mean(x * x, axis=-1, keepdims=True) + NORM_EPS)


def _modulate(x, g, shift, scale):
    return _rms(x) * g * (1.0 + scale) + shift


def _iota(shape, axis):
    return lax.broadcasted_iota(jnp.int32, shape, axis)


def _shift_rows(cur, prev_row, next_row):
    t = cur.shape[0]
    row = _iota(cur.shape, 0)
    xm = jnp.where(row == 0, prev_row, pltpu.roll(cur, 1, 0))
    xp = jnp.where(row == t - 1, next_row, pltpu.roll(cur, t - 1, 0))
    return xm, xp


def _halo_specs(t, n_blocks, width, col_block, chunk_of):
    per = t // SUBLANE
    last = n_blocks * per - 1

    def prev_map(b, s):
        return (b, jnp.maximum(chunk_of(s) * per - 1, 0), col_block)

    def next_map(b, s):
        return (b, jnp.minimum((chunk_of(s) + 1) * per, last), col_block)

    return (pl.BlockSpec((1, SUBLANE, width), prev_map), pl.BlockSpec((1, SUBLANE, width), next_map))


def _halo_rows(prev_ref, next_ref, chunk, n_chunks):
    prev_row = jnp.where(chunk == 0, 0.0, prev_ref[0, SUBLANE - 1:SUBLANE, :])
    next_row = jnp.where(chunk == n_chunks - 1, 0.0, next_ref[0, 0:1, :])
    return prev_row, next_row


def _mod_kernel(c_ref, w_ref, b_ref, o_ref):
    w_hi, w_lo = _split2(w_ref[0])
    o_ref[0] = _dot_split(_silu(c_ref[...]), w_hi, w_lo) + b_ref[0]


def adaln_mod(cvec, mod_w, mod_b):
    depth = mod_w.shape[0]
    return pl.pallas_call(
        _mod_kernel,
        name="adaln_mod",
        grid=(depth, N_MOD),
        in_specs=[pl.BlockSpec((SUBLANE, D_MODEL), lambda l, j: (0, 0)),
                  pl.BlockSpec((1, D_MODEL, D_MODEL), lambda l, j: (l, 0, j)),
                  pl.BlockSpec((1, 1, D_MODEL), lambda l, j: (l, 0, j))],
        out_specs=pl.BlockSpec((1, SUBLANE, D_MODEL), lambda l, j: (l, 0, j)),
        out_shape=jax.ShapeDtypeStruct((depth, SUBLANE, N_MOD * D_MODEL), F32),
        compiler_params=_params(("parallel", "parallel")),
    )(cvec, mod_w, mod_b.reshape(depth, 1, N_MOD * D_MODEL))


def _inproj_kernel(x_ref, g_ref, sh_ref, sc_ref, w_ref, *out_refs):
    h = _modulate(x_ref[...], g_ref[...], sh_ref[0], sc_ref[0]).astype(BF16)
    o = 0
    for ref in out_refs:
        n = ref.shape[-1]
        ref[...] = _dot(h, w_ref[:, o:o + n])
        o += n


def in_projection(xt, g, shift, scale, w_mix, rows_per_mod, tm):
    n = xt.shape[0]
    tiles_per_mod = rows_per_mod // tm
    widths = (HY_COLS, S5_COLS, RW_PAD, W_BR, M2_XBC, M2_PAD - W_BR - M2_XBC)
    mod_spec = pl.BlockSpec((1, 1, D_MODEL), lambda i: (i // tiles_per_mod, 0, 0))
    return pl.pallas_call(
        _inproj_kernel,
        name="in_projection",
        grid=(n // tm,),
        in_specs=[pl.BlockSpec((tm, D_MODEL), lambda i: (i, 0)),
                  pl.BlockSpec((1, D_MODEL), lambda i: (0, 0)),
                  mod_spec, mod_spec,
                  pl.BlockSpec(w_mix.shape, lambda i: (0, 0))],
        out_specs=[pl.BlockSpec((tm, w), lambda i: (i, 0)) for w in widths],
        out_shape=[jax.ShapeDtypeStruct((n, w), F32) for w in widths],
        compiler_params=_params(("parallel",)),
    )(xt, g, shift, scale, w_mix)


def _dft_mats(t):
    k = np.arange(t, dtype=np.float64)[:, None]
    n = np.arange(2 * t, dtype=np.float64)[None, :]
    ang = np.pi * (2.0 * k + 1.0) * n / (2.0 * t)
    fwd = np.concatenate([np.cos(ang), -np.sin(ang)], axis=0)
    inv = fwd[:, :t].T / t
    return fwd, inv


def _split_const(m):
    hi = jnp.asarray(m, F32).astype(BF16)
    lo = (jnp.asarray(m, F32) - hi.astype(F32)).astype(BF16)
    return jnp.stack([hi, lo])


def _dot_split_lhs(c_ref, b):
    b_hi, b_lo = _split2(b)
    return _dot(c_ref[0], b_hi) + _dot(c_ref[0], b_lo) + _dot(c_ref[1], b_hi)


def _hyfilt_kernel(w1_ref, b1_ref, w2_ref, b2_ref, w3_ref, fr_ref, o_ref):
    L = o_ref.shape[0]
    shape = (L, LANE)
    pos = _iota(shape, 0).astype(F32)
    lane = _iota(shape, 1)
    t = pos / (L - 1)
    w = (2.0 * math.pi / L) * pos
    band = jnp.where(lane >= 1 + HY_BANDS, lane - 1 - HY_BANDS, lane - 1).astype(F32)
    f = 1e-4 + band * ((HY_BANDS - 1 - 1e-4) / (HY_BANDS - 1))
    arg = f * w
    feats = jnp.where(lane == 0, t,
                      jnp.where(lane <= HY_BANDS, jnp.cos(arg),
                                jnp.where(lane <= 2 * HY_BANDS, -jnp.sin(arg), 0.0)))
    fr = fr_ref[...]
    h = jnp.sin(fr * (_dot(feats, w1_ref[...], HI) + b1_ref[...]))
    h = jnp.sin(fr * (_dot(h, w2_ref[...], HI) + b2_ref[...]))
    h = _dot(h, w3_ref[...], HI)
    ch = _iota((L, W_BR), 1).astype(F32)
    lo = math.log(HY_TARGET) / HY_SLOW_PCT
    hi = math.log(HY_TARGET) / HY_FAST_PCT
    deltas = jnp.abs(lo + ch * ((hi - lo) / (W_BR - 1)))
    decay = jnp.exp(-(t[:, 0:1]) * deltas)
    o_ref[:, 0:W_BR] = h[:, 0:W_BR] * decay
    o_ref[:, W_BR:2 * W_BR] = h[:, W_BR:2 * W_BR] * decay


def hyena_filter(L, f_w1, f_b1, f_w2, f_b2, f_w3, f_freq):
    pad = lambda a, r, c: jnp.zeros((r, c), F32).at[:a.shape[0], :a.shape[1]].set(a)
    args = (pad(f_w1, LANE, LANE), pad(f_b1[None], 1, LANE), pad(f_w2, LANE, LANE), pad(f_b2[None], 1, LANE),
            pad(f_w3, LANE, 2 * W_BR), pad(f_freq[None], 1, LANE))
    return pl.pallas_call(
        _hyfilt_kernel,
        name="hyena_filter",
        out_shape=jax.ShapeDtypeStruct((L, 2 * W_BR), F32),
        compiler_params=_params(None),
    )(*args)


def _ghat_kernel(f_ref, g_ref, o_ref):
    o_ref[0] = _dot_split_lhs(f_ref, g_ref[0])


def hyena_filter_spectra(filt, L, t):
    nb = L // t
    table = jnp.concatenate([filt[:, :W_BR], filt[:, W_BR:], jnp.zeros((1, W_BR), F32)], axis=0)
    m = np.arange(2 * t)[None, :]
    d = np.arange(-(nb - 1), nb)[:, None]
    off = d * t + np.where(m < t, m, m - 2 * t)
    idx = np.where(m == t, 2 * L, np.where(off >= 0, off, L - off)).astype(np.int32)
    sign = np.where(m == t, 0.0, np.where(m < t, 1.0, -1.0)).astype(np.float32) * np.ones_like(off, np.float32)
    g = table[idx] * sign[:, :, None]
    fwd, _ = _dft_mats(t)
    return pl.pallas_call(
        _ghat_kernel,
        name="hyena_filter_spectra",
        grid=(2 * nb - 1,),
        in_specs=[pl.BlockSpec((2, 2 * t, 2 * t), lambda d: (0, 0, 0)),
                  pl.BlockSpec((1, 2 * t, W_BR), lambda d: (d, 0, 0))],
        out_specs=pl.BlockSpec((1, 2 * t, W_BR), lambda d: (d, 0, 0)),
        out_shape=jax.ShapeDtypeStruct((2 * nb - 1, 2 * t, W_BR), F32),
        compiler_params=_params(("parallel",)),
    )(_split_const(fwd), g)


def _hypre_kernel(nb, cur_ref, prev_ref, next_ref, cw_ref, cb_ref, f_ref, x0_ref, u_ref, uh_ref):
    j = pl.program_id(1)
    cur = cur_ref[0]
    prev_row, next_row = _halo_rows(prev_ref, next_ref, j, nb)
    xm, xp = _shift_rows(cur, prev_row, next_row)
    pc = cb_ref[...] + xm * cw_ref[0:1, :] + cur * cw_ref[1:2, :] + xp * cw_ref[2:3, :]
    u = pc[:, W_BR:2 * W_BR] * pc[:, 2 * W_BR:3 * W_BR]
    x0_ref[0] = pc[:, 0:W_BR]
    u_ref[0] = u
    uh_ref[0, 0] = _dot_split_lhs(f_ref, u)


def _hymain_kernel(nb, t, g_ref, uh_ref, x0_ref, u_ref, bias_ref, inv_ref, o_ref):
    i = pl.program_id(2)
    acc_re = jnp.zeros((t, LANE), F32)
    acc_im = jnp.zeros((t, LANE), F32)
    for j in range(nb):
        g = g_ref[i - j + nb - 1]
        gr, gi = g[0:t], g[t:2 * t]
        ur, ui = uh_ref[0, j, 0:t, :], uh_ref[0, j, t:2 * t, :]
        acc_re += gr * ur - gi * ui
        acc_im += gr * ui + gi * ur
    y = _dot_split_lhs(inv_ref, jnp.concatenate([acc_re, acc_im], axis=0))
    o_ref[0] = x0_ref[0] * (y + bias_ref[...] * u_ref[0])


def hyena_mixer(hy, ghat, conv_w, conv_b, h_bias):
    b_, L, _ = hy.shape
    t = min(HY_BLOCK, L)
    nb = L // t
    fwd, inv = _dft_mats(t)
    prev_spec, next_spec = _halo_specs(t, nb, HY_COLS, 0, lambda s: s)
    x0, u, uh = pl.pallas_call(
        functools.partial(_hypre_kernel, nb),
        name="hyena_conv_dft",
        grid=(b_, nb),
        in_specs=[pl.BlockSpec((1, t, HY_COLS), lambda b, j: (b, j, 0)), prev_spec, next_spec,
                  pl.BlockSpec((3, HY_COLS), lambda b, j: (0, 0)),
                  pl.BlockSpec((1, HY_COLS), lambda b, j: (0, 0)),
                  pl.BlockSpec((2, 2 * t, t), lambda b, j: (0, 0, 0))],
        out_specs=[pl.BlockSpec((1, t, W_BR), lambda b, j: (b, j, 0)),
                   pl.BlockSpec((1, t, W_BR), lambda b, j: (b, j, 0)),
                   pl.BlockSpec((1, 1, 2 * t, W_BR), lambda b, j: (b, j, 0, 0))],
        out_shape=[jax.ShapeDtypeStruct((b_, L, W_BR), F32), jax.ShapeDtypeStruct((b_, L, W_BR), F32),
                   jax.ShapeDtypeStruct((b_, nb, 2 * t, W_BR), F32)],
        compiler_params=_params(("parallel", "parallel")),
    )(hy, hy, hy, conv_w, conv_b[None], _split_const(fwd[:, :t]))
    ncb = W_BR // LANE
    return pl.pallas_call(
        functools.partial(_hymain_kernel, nb, t),
        name="hyena_longconv",
        grid=(ncb, b_, nb),
        in_specs=[pl.BlockSpec((2 * nb - 1, 2 * t, LANE), lambda c, b, i: (0, 0, c)),
                  pl.BlockSpec((1, nb, 2 * t, LANE), lambda c, b, i: (b, 0, 0, c)),
                  pl.BlockSpec((1, t, LANE), lambda c, b, i: (b, i, c)),
                  pl.BlockSpec((1, t, LANE), lambda c, b, i: (b, i, c)),
                  pl.BlockSpec((1, LANE), lambda c, b, i: (0, c)),
                  pl.BlockSpec((2, t, 2 * t), lambda c, b, i: (0, 0, 0))],
        out_specs=pl.BlockSpec((1, t, LANE), lambda c, b, i: (b, i, c)),
        out_shape=jax.ShapeDtypeStruct((b_, L, W_BR), F32),
        compiler_params=_params(("parallel", "parallel", "parallel")),
    )(ghat, uh, x0, u, h_bias[None], _split_const(inv))


def _s5_kernel(nc, u_ref, h0_ref, bb_ref, ap_ref, cb_ref, dsk_ref, wg_ref, y_ref, hf_ref, yacc, state):
    s = pl.program_id(1)
    t = u_ref.shape[1]
    n = S5_LANES

    @pl.when(s == 0)
    def _():
        hf_ref[...] = jnp.zeros_like(hf_ref)

    def phase(d):
        chunk = s if d == 0 else 2 * nc - 1 - s
        first = (s == 0) if d == 0 else (s == nc)
        last = (s == nc - 1) if d == 0 else (s == 2 * nc - 1)

        @pl.when(first)
        def _():
            state[0:1, :] = h0_ref[0, d:d + 1, :]

        u = u_ref[0]
        x = _dot(u.astype(BF16), bb_ref[d])
        row = _iota((t, n), 0)
        st = state[0:1, :]
        pw = (lambda p: p - 1) if d == 0 else (lambda p: SUBLANE - p)
        a_re, a_im = ap_ref[d, pw(1):pw(1) + 1, 0:n], ap_ref[d, pw(1):pw(1) + 1, n:2 * n]
        c_re = a_re * st[:, 0:n] - a_im * st[:, n:2 * n]
        c_im = a_re * st[:, n:2 * n] + a_im * st[:, 0:n]
        entry = 0 if d == 0 else t - 1
        x_re = x[:, 0:n] + jnp.where(row == entry, c_re, 0.0)
        x_im = x[:, n:2 * n] + jnp.where(row == entry, c_im, 0.0)
        n_grp = t // SUBLANE
        x_re, x_im = x_re.reshape(n_grp, SUBLANE, n), x_im.reshape(n_grp, SUBLANE, n)
        sub = _iota((SUBLANE, n), 0)
        for sh in (1, 2, 4):
            valid = (sub >= sh) if d == 0 else (sub < SUBLANE - sh)
            a_re = jnp.where(valid, ap_ref[d, pw(sh):pw(sh) + 1, 0:n], 0.0)[None]
            a_im = jnp.where(valid, ap_ref[d, pw(sh):pw(sh) + 1, n:2 * n], 0.0)[None]
            rot = sh if d == 0 else SUBLANE - sh
            s_re, s_im = pltpu.roll(x_re, rot, 1), pltpu.roll(x_im, rot, 1)
            x_re, x_im = x_re + a_re * s_re - a_im * s_im, x_im + a_re * s_im + a_im * s_re
        p_re, p_im = ap_ref[d, :, 0:n], ap_ref[d, :, n:2 * n]
        out_re, out_im = [None] * n_grp, [None] * n_grp
        c_re = c_im = None
        for g in (range(n_grp) if d == 0 else range(n_grp - 1, -1, -1)):
            g_re, g_im = x_re[g], x_im[g]
            if c_re is not None:
                g_re, g_im = g_re + p_re * c_re - p_im * c_im, g_im + p_re * c_im + p_im * c_re
            out_re[g], out_im[g] = g_re, g_im
            edge = SUBLANE - 1 if d == 0 else 0
            c_re, c_im = g_re[edge:edge + 1], g_im[edge:edge + 1]
        x_re, x_im = jnp.concatenate(out_re, axis=0), jnp.concatenate(out_im, axis=0)
        ex = t - 1 if d == 0 else 0
        state[0:1, 0:n] = x_re[ex:ex + 1, :]
        state[0:1, n:2 * n] = x_im[ex:ex + 1, :]
        yd = _dot(x_re.astype(BF16), cb_ref[d, 0:n, :]) + _dot(x_im.astype(BF16), cb_ref[d, n:2 * n, :])
        rows = pl.ds(pl.multiple_of(chunk * t, t), t)
        if d == 0:
            yacc[rows, :] = dsk_ref[...] * u + yd
        else:
            y = yacc[rows, :] + yd
            y = 0.5 * y * (1.0 + jnp.tanh(math.sqrt(2.0 / math.pi) * (y + 0.044715 * y * y * y)))
            y_ref[0] = y * _sigmoid(_bdot(y, wg_ref[...]))

        @pl.when(last)
        def _():
            hf_ref[0, d:d + 1, :] = state[0:1, :]

    @pl.when(s < nc)
    def _():
        phase(0)

    @pl.when(s >= nc)
    def _():
        phase(1)


def _two_pass_chunk(nc):
    return lambda s: jnp.where(s < nc, s, 2 * nc - 1 - s)


def _two_pass_out(nc):
    return lambda s: jnp.where(s < nc, nc - 1, 2 * nc - 1 - s)


def s5_prepare(lam_re, lam_im, log_step, b_re, b_im, c_re, c_im):
    step = jnp.exp(log_step)[:, :, None]
    mag = jnp.exp(lam_re * step)
    lb_re, lb_im = mag * jnp.cos(lam_im * step), mag * jnp.sin(lam_im * step)
    den = lam_re * lam_re + lam_im * lam_im
    q_re = ((lb_re - 1.0) * lam_re + lb_im * lam_im) / den
    q_im = (lb_im * lam_re - (lb_re - 1.0) * lam_im) / den
    bb_re = q_re[..., None] * b_re - q_im[..., None] * b_im
    bb_im = q_re[..., None] * b_im + q_im[..., None] * b_re
    eye = jnp.eye(S5_GROUPS, dtype=F32)

    def blockdiag_in(m):
        return jnp.einsum('dgni,gh->dgihn', m, eye).reshape(2, W_BR, S5_LANES)

    def blockdiag_out(m):
        return jnp.einsum('dgin,gh->dgnhi', m, eye).reshape(2, S5_LANES, W_BR)

    bblk = jnp.concatenate([blockdiag_in(bb_re), blockdiag_in(bb_im)], axis=-1)
    cblk = jnp.concatenate([blockdiag_out(c_re), -blockdiag_out(c_im)], axis=1)
    j = np.arange(SUBLANE, dtype=np.float32)
    pw = jnp.asarray(np.stack([j + 1.0, SUBLANE - j]))[:, :, None]
    arg_re = (lam_re * step).reshape(2, 1, S5_LANES) * pw
    arg_im = (lam_im * step).reshape(2, 1, S5_LANES) * pw
    apow = jnp.concatenate([jnp.exp(arg_re) * jnp.cos(arg_im), jnp.exp(arg_re) * jnp.sin(arg_im)], axis=-1)
    return bblk.astype(BF16), apow, cblk.astype(BF16)


def s5_mixer(u, h0, bblk, apow, cblk, d_skip, w_glu):
    b_, L, _ = u.shape
    t = min(S5_CHUNK, L)
    nc = L // t
    chunk_of, out_of = _two_pass_chunk(nc), _two_pass_out(nc)
    const = lambda a: pl.BlockSpec(a.shape, lambda b, s: (0,) * a.ndim)
    dsk = d_skip[None]
    wg = w_glu.astype(BF16)
    return pl.pallas_call(
        functools.partial(_s5_kernel, nc),
        name="s5_mixer",
        grid=(b_, 2 * nc),
        in_specs=[pl.BlockSpec((1, t, W_BR), lambda b, s: (b, chunk_of(s), 0)),
                  pl.BlockSpec((1, SUBLANE, 2 * S5_LANES), lambda b, s: (b, 0, 0)),
                  const(bblk), const(apow), const(cblk), const(dsk), const(wg)],
        out_specs=[pl.BlockSpec((1, t, W_BR), lambda b, s: (b, out_of(s), 0)),
                   pl.BlockSpec((1, SUBLANE, 2 * S5_LANES), lambda b, s: (b, 0, 0))],
        out_shape=[jax.ShapeDtypeStruct((b_, L, W_BR), F32),
                   jax.ShapeDtypeStruct((b_, SUBLANE, 2 * S5_LANES), F32)],
        scratch_shapes=[pltpu.VMEM((L, W_BR), F32), pltpu.VMEM((SUBLANE, 2 * S5_LANES), F32)],
        compiler_params=_params(("parallel", "arbitrary")),
    )(u, h0, bblk, apow, cblk, dsk, wg)


def _head_mask(h, shape, axis):
    lane = _iota(shape, axis)
    return (lane >= h * RW_HEAD) & (lane < (h + 1) * RW_HEAD)


def _split3(x):
    hi = x.astype(BF16)
    r1 = x - hi.astype(F32)
    mid = r1.astype(BF16)
    return hi, mid, (r1 - mid.astype(F32)).astype(BF16)


def _stack_heads(x):
    xb = x.astype(BF16)
    zero = jnp.zeros_like(xb)
    return jnp.concatenate([jnp.where(_head_mask(h, xb.shape, 1), xb, zero) for h in range(RW_HEADS)], axis=0)


def _split2(x):
    hi = x.astype(BF16)
    return hi, (x - hi.astype(F32)).astype(BF16)


def _dot_exact_rhs(a, b):
    hi, lo = _split2(a)
    return _dot(hi, b) + _dot(lo, b)


def _dot_split(a, b_hi, b_lo):
    hi, lo = _split2(a)
    return _dot(hi, b_hi) + _dot(hi, b_lo) + _dot(lo, b_hi)


RW_INV_BASE = 8


def _cat_to_blockdiag(x):
    t, n = x.shape
    xb = x.astype(BF16)
    tiled = jnp.concatenate([xb] * (n // t), axis=0)
    same = (_iota((n, n), 0) // t) == (_iota((n, n), 1) // t)
    return jnp.where(same, tiled, jnp.zeros_like(tiled))


def _tri_inverse(nmat):
    t = nmat[0].shape[0]
    row, col = _iota(nmat[0].shape, 0), _iota(nmat[0].shape, 1) % t
    same = lambda s: (row // s) == (col // s)
    mul = lambda x, y: _dot(x.astype(BF16), _cat_to_blockdiag(y))
    base = [jnp.where(same(RW_INV_BASE), x, 0.0) for x in nmat]
    eye = jnp.where(row == col, 1.0, 0.0)
    m = [eye - x for x in base]
    p = base
    for _ in range(RW_INV_BASE.bit_length() - 2):
        p = [mul(x, x) for x in p]
        m = [x + mul(x, y) for x, y in zip(m, p)]
    s = 2 * RW_INV_BASE
    while s <= t:
        off = same(s) & jnp.logical_not(same(s // 2))
        mc = [mul(x, jnp.where(off, y, 0.0)) for x, y in zip(m, nmat)]
        m = [x - mul(y, x) for x, y in zip(m, mc)]
        s *= 2
    return m


def _wkv_local(d, r, logw, kd, v, kk, a):
    t = r[0].shape[0]
    bf = lambda x: x.astype(BF16)
    each = lambda f, *ls: [f(*xs) for xs in zip(*ls)]
    ri, ci = _iota((t, t), 0), _iota((t, t), 1)
    tri = ((ci <= ri) if d == 0 else (ci >= ri)).astype(BF16)
    row, col = _iota((t, RW_HEADS * t), 0), _iota((t, RW_HEADS * t), 1) % t
    incl = (col <= row) if d == 0 else (col >= row)
    strict = (col < row) if d == 0 else (col > row)
    dn = W_BR
    diag = _iota((dn, dn), 0) == _iota((dn, dn), 1)

    cs = each(lambda x: sum(_dot(tri, part) for part in _split3(x)), logw)
    g_tot = each(lambda x: jnp.exp(jnp.sum(x, axis=0, keepdims=True)), logw)
    g_inv = each(lambda c: jnp.exp(-c), cs)
    bt = each(lambda k_, a_, g: k_ * a_ * g, kk, a, g_inv)
    kt = each(lambda k_, g: k_ * g, kd, g_inv)
    qa = each(lambda k_, c, w: k_ * jnp.exp(c - w), kk, cs, logw)
    rh = each(lambda r_, c: r_ * jnp.exp(c), r, cs)
    bts, kts, vs = each(_stack_heads, bt), each(_stack_heads, kt), each(_stack_heads, v)
    bhs = each(lambda x, g: _stack_heads(x * g), bt, g_tot)
    khs = each(lambda x, g: _stack_heads(x * g), kt, g_tot)
    a_ab = each(lambda q, b: jnp.where(strict, _dot_nt(bf(q), b), 0.0), qa, bts)
    a_ak = each(lambda q, k_: bf(jnp.where(strict, _dot_nt(bf(q), k_), 0.0)), qa, kts)
    a_rb = each(lambda q, b: bf(jnp.where(incl, _dot_nt(bf(q), b), 0.0)), rh, bts)
    a_rk = each(lambda q, k_: bf(jnp.where(incl, _dot_nt(bf(q), k_), 0.0)), rh, kts)
    m = each(bf, _tri_inverse(a_ab))
    x1 = each(lambda a_, v_: _stack_heads(_dot(a_, v_)), a_ak, vs)
    ul = each(lambda m_, x: _stack_heads(-_dot(m_, x)), m, x1)
    w0 = each(lambda m_, q: _stack_heads(_dot(m_, _stack_heads(q))), m, qa)
    yloc = each(lambda ak, v_, ab, u: _dot(ak, v_) + _dot(ab, u), a_rk, vs, a_rb, ul)
    yh = each(lambda r_, ab, w: r_ - _dot(ab, w), rh, a_rb, w0)
    p_bd = each(lambda g, b, w: jnp.where(diag, g, 0.0) - _dot_tn(b, w), g_tot, bhs, w0)
    q_bd = each(lambda b, u, k_, v_: _dot_tn(b, u) + _dot_tn(k_, v_), bhs, ul, khs, vs)
    return list(zip(yloc, yh, p_bd, q_bd))


def _rwkv_kernel(ng, cur_ref, prev_ref, next_ref, h0_ref, mu_ref, w0_ref, a0_ref, lrh_ref, lrl_ref,
                 kk_ref, ka_ref, rk_ref, lnw_ref, lnb_ref, y_ref, hf_ref, yacc, hst):
    s = pl.program_id(1)
    t = cur_ref.shape[1]
    n = W_BR
    ri, ci = _iota((n, n), 0), _iota((n, n), 1)
    head_sum = ((ri // RW_HEAD) == (ci // RW_HEAD)).astype(BF16)

    def phase(d):
        chunk = s if d == 0 else 2 * ng - 1 - s
        first = (s == 0) if d == 0 else (s == ng)
        last = (s == ng - 1) if d == 0 else (s == 2 * ng - 1)

        @pl.when(first)
        def _():
            hst[...] = h0_ref[0, d]

        cur = cur_ref[0]
        prev_row, next_row = _halo_rows(prev_ref, next_ref, chunk, ng)
        xm, xp = _shift_rows(cur, prev_row, next_row)
        p = cur + (0.5 * (xm + xp) - cur) * mu_ref[...]
        r, k, v, lr = p[:, 0:n], p[:, n:2 * n], p[:, 2 * n:3 * n], p[:, 3 * n:4 * n]
        kk = k * kk_ref[...]
        kk = kk * lax.rsqrt(jnp.maximum(_dot_exact_rhs(kk * kk, head_sum), 1e-24))
        lane = _iota(lr.shape, 1)
        feats = jnp.where(lane < 2 * RW_W_RANK, jnp.tanh(lr),
                          jnp.where(lane < 2 * (RW_W_RANK + RW_A_RANK), lr, _sigmoid(lr)))
        cols = (2 if d == 0 else 3) * n
        low = _dot_split(feats, lrh_ref[d, :, 0:cols], lrl_ref[d, :, 0:cols])
        w = w0_ref[d:d + 1, :] + low[:, 0:n]
        logw = -math.exp(-0.5) * _sigmoid(w)
        a = _sigmoid(a0_ref[d:d + 1, :] + low[:, n:2 * n])
        kd = k * (1.0 + (a - 1.0) * ka_ref[...])
        nsub = t // RW_SUB
        subs = lambda x: [x[q * RW_SUB:(q + 1) * RW_SUB] for q in range(nsub)]
        local = _wkv_local(d, subs(r), subs(logw), subs(kd), subs(v), subs(kk), subs(a))
        h = hst[...]
        ys = [None] * nsub
        for q in (range(nsub) if d == 0 else range(nsub - 1, -1, -1)):
            yloc, yh, p_bd, q_bd = local[q]
            hb = h.astype(BF16)
            ys[q] = yloc + _dot(yh.astype(BF16), hb)
            h = _dot(p_bd.astype(BF16), hb) + q_bd
        hst[...] = h
        yd = jnp.concatenate(ys, axis=0)
        rows = pl.ds(pl.multiple_of(chunk * t, t), t)
        if d == 0:
            yacc[rows, :] = yd
        else:
            y = yacc[rows, :] + yd
            mean = _dot_exact_rhs(y, head_sum) * (1.0 / RW_HEAD)
            yc = y - mean
            var = _dot_exact_rhs(yc * yc, head_sum) * (1.0 / RW_HEAD)
            y = yc * lax.rsqrt(var + RW_GN_EPS) * lnw_ref[...] + lnb_ref[...]
            bonus = _dot_exact_rhs(r * k * rk_ref[...], head_sum) * v
            y_ref[0] = (y + bonus) * low[:, 2 * n:3 * n]

        @pl.when(last)
        def _():
            hf_ref[0, d] = hst[...]

    @pl.when(s < ng)
    def _():
        phase(0)

    @pl.when(s >= ng)
    def _():
        phase(1)


def rwkv_prepare(mu, w0, w2, a0, a2, g2, k_k, k_a, r_k, ln_w, ln_b):
    n = W_BR
    mu_p = jnp.zeros((1, RW_PAD), F32).at[0, :RW_COLS].set(mu)
    w2_p = jnp.zeros((2, n, n), F32)
    a2_p = jnp.zeros((2, n, n), F32)
    for d in range(2):
        w2_p = w2_p.at[d, d * RW_W_RANK:(d + 1) * RW_W_RANK].set(w2[d])
        o = 2 * RW_W_RANK + d * RW_A_RANK
        a2_p = a2_p.at[d, o:o + RW_A_RANK].set(a2[d])
    o = 2 * RW_W_RANK + 2 * RW_A_RANK
    g2_p = jnp.zeros((n, n), F32).at[o:o + RW_G_RANK].set(g2)
    low = jnp.concatenate([w2_p, a2_p, jnp.broadcast_to(g2_p, (2, n, n))], axis=-1)
    low_hi = low.astype(BF16)
    low_lo = (low - low_hi.astype(F32)).astype(BF16)
    return (mu_p, w0, a0, low_hi, low_lo, k_k[None], k_a[None], r_k.reshape(1, n), ln_w[None], ln_b[None])


def rwkv_mixer(rw, h0, prm):
    b_, L, _ = rw.shape
    t = min(RW_STEP, L)
    ng = L // t
    chunk_of, out_of = _two_pass_chunk(ng), _two_pass_out(ng)
    prev_spec, next_spec = _halo_specs(t, ng, RW_PAD, 0, chunk_of)
    const = lambda a: pl.BlockSpec(a.shape, lambda b, s: (0,) * a.ndim)
    st_spec = pl.BlockSpec((1, 2, W_BR, W_BR), lambda b, s: (b, 0, 0, 0))
    return pl.pallas_call(
        functools.partial(_rwkv_kernel, ng),
        name="rwkv_mixer",
        grid=(b_, 2 * ng),
        in_specs=[pl.BlockSpec((1, t, RW_PAD), lambda b, s: (b, chunk_of(s), 0)), prev_spec, next_spec, st_spec]
                 + [const(a) for a in prm],
        out_specs=[pl.BlockSpec((1, t, W_BR), lambda b, s: (b, out_of(s), 0)), st_spec],
        out_shape=[jax.ShapeDtypeStruct((b_, L, W_BR), F32), jax.ShapeDtypeStruct((b_, 2, W_BR, W_BR), F32)],
        scratch_shapes=[pltpu.VMEM((L, W_BR), F32), pltpu.VMEM((W_BR, W_BR), F32)],
        compiler_params=_params(("parallel", "arbitrary")),
    )(rw, rw, rw, h0, *prm)


def _m2_kernel(nc, z_ref, cur_ref, prev_ref, next_ref, dt_ref, h0_ref, cw_ref, cb_ref, ex_ref, a_ref, dtb_ref,
               dsk_ref, nw_ref, y_ref, hf_ref, yacc, hst):
    s = pl.program_id(1)
    t = cur_ref.shape[1]
    n = W_BR
    ns = M2_STATE
    ri, ci = _iota((t, t), 0), _iota((t, t), 1)

    def phase(d):
        chunk = s if d == 0 else 2 * nc - 1 - s
        first = (s == 0) if d == 0 else (s == nc)
        last = (s == nc - 1) if d == 0 else (s == 2 * nc - 1)

        @pl.when(first)
        def _():
            hst[...] = h0_ref[0, d]

        cur = cur_ref[0]
        prev_row, next_row = _halo_rows(prev_ref, next_ref, chunk, nc)
        xm, xp = _shift_rows(cur, prev_row, next_row)
        xbc = _silu(cb_ref[...] + xm * cw_ref[0:1, :] + cur * cw_ref[1:2, :] + xp * cw_ref[2:3, :])
        xs, bm, cm = xbc[:, 0:n], xbc[:, n:2 * n], xbc[:, 2 * n:3 * n]
        raw = sum(_dot(part, ex_ref[d]) for part in _split3(dt_ref[0])) + dtb_ref[d:d + 1, :]
        dtd = jnp.maximum(raw, 0.0) + jnp.log(1.0 + jnp.exp(-jnp.abs(raw)))
        da = dtd * a_ref[d:d + 1, :]
        incl = (ci <= ri) if d == 0 else (ci >= ri)
        tri = incl.astype(BF16)
        da_parts = _split3(da)
        acs = sum(_dot(tri, part) for part in da_parts)
        tot = jnp.sum(da, axis=0, keepdims=True)
        xdt = xs * dtd
        xdt_b = xdt.astype(BF16)
        bm_b, cm_b = bm.astype(BF16), cm.astype(BF16)
        sel = jnp.where(_iota((SUBLANE, n), 1) // M2_HEADDIM == _iota((SUBLANE, n), 0), 1.0 / M2_HEADDIM,
                        0.0).astype(BF16)
        acs_t = sum(_dot_nt(sel, part) for part in _split3(acs))
        ydiag = jnp.zeros((t, n), F32)
        for g in range(M2_GROUPS):
            cb = _dot_nt(cm_b[:, g * ns:(g + 1) * ns], bm_b[:, g * ns:(g + 1) * ns])
            for h in range(g * (M2_HEADS // M2_GROUPS), (g + 1) * (M2_HEADS // M2_GROUPS)):
                seg = acs[:, h * M2_HEADDIM:h * M2_HEADDIM + 1] - acs_t[h:h + 1, :]
                scores = cb * jnp.exp(jnp.where(incl, seg, -jnp.inf))
                ydiag = jnp.where(_head_mask(h, (t, n), 1), _dot(scores.astype(BF16), xdt_b), ydiag)
        h_in = hst[...]
        h_b = h_in.astype(BF16)
        lane = _iota((t, n), 1)
        yoff = jnp.where(lane < n // M2_GROUPS, _dot(cm_b[:, 0:ns], h_b), _dot(cm_b[:, ns:2 * ns], h_b))
        yd = ydiag + jnp.exp(acs) * yoff
        xdec = (xdt * jnp.exp(tot - acs)).astype(BF16)
        lane_s = _iota((ns, n), 1)
        new = jnp.where(lane_s < n // M2_GROUPS, _dot_tn(bm_b[:, 0:ns], xdec), _dot_tn(bm_b[:, ns:2 * ns], xdec))
        hst[...] = h_in * jnp.exp(tot) + new
        rows = pl.ds(pl.multiple_of(chunk * t, t), t)
        if d == 0:
            yacc[rows, :] = yd + dsk_ref[...] * xs
        else:
            y = (yacc[rows, :] + yd) * _silu(z_ref[0])
            y_ref[0] = _rms(y) * nw_ref[...]

        @pl.when(last)
        def _():
            hf_ref[0, d] = hst[...]

    @pl.when(s < nc)
    def _():
        phase(0)

    @pl.when(s >= nc)
    def _():
        phase(1)


def mamba_prepare(conv_w, conv_b, a_log, dt_bias, d_skip, norm_w):
    n = W_BR
    rep = lambda v: jnp.repeat(v, M2_HEADDIM, axis=-1)
    expand = jnp.zeros((2, LANE, n), F32)
    for d in range(2):
        for h in range(M2_HEADS):
            expand = expand.at[d, d * M2_HEADS + h, h * M2_HEADDIM:(h + 1) * M2_HEADDIM].set(1.0)
    return (conv_w, conv_b[None], expand.astype(BF16), rep(-jnp.exp(a_log)), rep(dt_bias), rep(d_skip)[None],
            norm_w[None])


def mamba_mixer(z, xbc, dt, h0, prm):
    b_, L, _ = z.shape
    t = min(M2_CHUNK, L)
    nc = L // t
    chunk_of, out_of = _two_pass_chunk(nc), _two_pass_out(nc)
    prev_spec, next_spec = _halo_specs(t, nc, M2_XBC, 0, chunk_of)
    const = lambda a: pl.BlockSpec(a.shape, lambda b, s: (0,) * a.ndim)
    st_spec = pl.BlockSpec((1, 2, M2_STATE, W_BR), lambda b, s: (b, 0, 0, 0))
    seq = lambda w: pl.BlockSpec((1, t, w), lambda b, s: (b, chunk_of(s), 0))
    return pl.pallas_call(
        functools.partial(_m2_kernel, nc),
        name="mamba_mixer",
        grid=(b_, 2 * nc),
        in_specs=[seq(W_BR), seq(M2_XBC), prev_spec, next_spec, seq(LANE), st_spec] + [const(a) for a in prm],
        out_specs=[pl.BlockSpec((1, t, W_BR), lambda b, s: (b, out_of(s), 0)), st_spec],
        out_shape=[jax.ShapeDtypeStruct((b_, L, W_BR), F32),
                   jax.ShapeDtypeStruct((b_, 2, M2_STATE, W_BR), F32)],
        scratch_shapes=[pltpu.VMEM((L, W_BR), F32), pltpu.VMEM((M2_STATE, W_BR), F32)],
        compiler_params=_params(("parallel", "arbitrary")),
    )(z, xbc, xbc, xbc, dt, h0, *prm)


def _merge_kernel(x_ref, g_ref, sh_ref, sc_ref, gt_ref, y0_ref, y1_ref, y2_ref, y3_ref, wg_ref, wb_ref, wo_ref,
                  o_ref):
    x = x_ref[...]
    h = _modulate(x, g_ref[...], sh_ref[0], sc_ref[0]).astype(BF16)
    acc = jnp.zeros(x.shape, F32)
    for i, y_ref in enumerate((y0_ref, y1_ref, y2_ref, y3_ref)):
        gate = _sigmoid(_dot(h, wg_ref[i]))
        acc += gate * _dot(y_ref[...].astype(BF16), wb_ref[i])
    o_ref[...] = x + gt_ref[0] * _dot(acc.astype(BF16), wo_ref[...])


def merge_branches(xt, g, shift, scale, gate, ys, w_gate, w_branch, w_out, rows_per_mod, tm):
    n = xt.shape[0]
    tiles_per_mod = rows_per_mod // tm
    mod_spec = pl.BlockSpec((1, 1, D_MODEL), lambda i: (i // tiles_per_mod, 0, 0))
    const = lambda a: pl.BlockSpec(a.shape, lambda i: (0,) * a.ndim)
    tok = lambda w: pl.BlockSpec((tm, w), lambda i: (i, 0))
    return pl.pallas_call(
        _merge_kernel,
        name="merge_branches",
        grid=(n // tm,),
        in_specs=[tok(D_MODEL), const(g), mod_spec, mod_spec, mod_spec] + [tok(W_BR)] * N_BRANCH
                 + [const(w_gate), const(w_branch), const(w_out)],
        out_specs=tok(D_MODEL),
        out_shape=jax.ShapeDtypeStruct((n, D_MODEL), F32),
        compiler_params=_params(("parallel",)),
    )(xt, g, shift, scale, gate, *ys, w_gate, w_branch, w_out)


def _router_kernel(x_ref, g_ref, sh_ref, sc_ref, wr_ref, comb_ref):
    u = _modulate(x_ref[...], g_ref[...], sh_ref[0], sc_ref[0])
    logits = _dot_split(u, wr_ref[0], wr_ref[1])
    lane = _iota(logits.shape, 1).astype(F32)
    neg = -jnp.inf
    is_grp = (lane >= MOE_EXPERTS) & (lane < MOE_EXPERTS + MOE_GROUPS)
    gl = jnp.where(is_grp, logits, neg)
    gmax = jnp.max(gl, axis=-1, keepdims=True)
    grp_p = 1.0 / jnp.sum(jnp.exp(gl - gmax), axis=-1, keepdims=True)
    grp_idx = jnp.min(jnp.where(gl == gmax, lane, 4.0 * LANE), axis=-1, keepdims=True) - MOE_EXPERTS
    in_grp = (lane >= grp_idx * MOE_PER_GROUP) & (lane < (grp_idx + 1) * MOE_PER_GROUP)
    el = jnp.where(in_grp, logits, neg)
    v1 = jnp.max(el, axis=-1, keepdims=True)
    i1 = jnp.min(jnp.where(el == v1, lane, 4.0 * LANE), axis=-1, keepdims=True)
    el2 = jnp.where(lane == i1, neg, el)
    v2 = jnp.max(el2, axis=-1, keepdims=True)
    i2 = jnp.min(jnp.where(el2 == v2, lane, 4.0 * LANE), axis=-1, keepdims=True)
    e2 = jnp.exp(v2 - v1)
    w1 = grp_p / (1.0 + e2)
    comb = jnp.where(lane == i1, w1, jnp.where(lane == i2, w1 * e2, 0.0))
    comb_ref[...] = jnp.where(lane == MOE_GROUP_LANE, grp_idx, comb)


def moe_router(xt, g, shift, scale, w_route, rows_per_mod, tm):
    n = xt.shape[0]
    tiles_per_mod = rows_per_mod // tm
    mod_spec = pl.BlockSpec((1, 1, D_MODEL), lambda i: (i // tiles_per_mod, 0, 0))
    return pl.pallas_call(
        _router_kernel,
        name="moe_router",
        grid=(n // tm,),
        in_specs=[pl.BlockSpec((tm, D_MODEL), lambda i: (i, 0)), pl.BlockSpec((1, D_MODEL), lambda i: (0, 0)),
                  mod_spec, mod_spec, pl.BlockSpec((2, D_MODEL, LANE), lambda i: (0, 0, 0))],
        out_specs=pl.BlockSpec((tm, LANE), lambda i: (i, 0)),
        out_shape=jax.ShapeDtypeStruct((n, LANE), F32),
        compiler_params=_params(("parallel",)),
    )(xt, g, shift, scale, w_route)


def _moe_kernel(final_norm, meta_ref, x_ref, g_ref, sh_ref, sc_ref, gt_ref, comb_ref, wg_ref, wu_ref, wd_ref, nf_ref,
                o_ref, pt_sc, xs_sc, cs_sc, acc):
    i = pl.program_id(0)
    e = pl.program_id(1)
    tm = x_ref.shape[0]

    @pl.when(e == 0)
    def _():
        u = _modulate(x_ref[...], g_ref[...], sh_ref[0], sc_ref[0]).astype(BF16)
        comb = comb_ref[...]
        lane = _iota((tm, LANE), 1)
        member = comb[:, MOE_GROUP_LANE:MOE_GROUP_LANE + 1] == lane.astype(F32)
        tri = jnp.where(_iota((tm, tm), 1) <= _iota((tm, tm), 0), 1.0, 0.0).astype(BF16)
        rank = _dot(tri, jnp.where(member, 1.0, 0.0).astype(BF16))
        offs = jnp.zeros((tm, LANE), jnp.int32)
        for grp in range(MOE_GROUPS):
            offs = jnp.where(lane == grp, meta_ref[i, grp], offs)
        posmat = jnp.where(member, offs.astype(F32) + rank - 1.0, 0.0)
        pos_col = jnp.sum(posmat, axis=-1, keepdims=True)
        ones = jnp.ones((SUBLANE, LANE), BF16)
        pos_row = sum(_dot_nt(ones, part) for part in _split3(posmat))[0:1, :]
        pt_sc[...] = jnp.where(pos_col == _iota((tm, tm), 1).astype(F32), 1.0, 0.0).astype(BF16)
        p = jnp.where(pos_row == _iota((tm, tm), 0).astype(F32), 1.0, 0.0).astype(BF16)
        xs_sc[...] = _dot(p, u).astype(BF16)
        cs_sc[...] = sum(_dot(p, part) for part in _split3(comb))
        acc[...] = jnp.zeros_like(acc)

    grp = e // MOE_PER_GROUP
    start = meta_ref[i, grp]
    count = meta_ref[i, MOE_GROUPS + grp]
    lo = start // MOE_SUB
    hi = jnp.where(count > 0, (start + count + MOE_SUB - 1) // MOE_SUB, lo)

    def body(j, carry):
        rows = pl.ds(pl.multiple_of(j * MOE_SUB, MOE_SUB), MOE_SUB)
        xs = xs_sc[rows, :]
        h = _silu(_dot(xs, wg_ref[0, 0])) * _dot(xs, wu_ref[0, 0])
        cs = cs_sc[rows, :]
        w = jnp.sum(jnp.where(_iota(cs.shape, 1) == e, cs, 0.0), axis=-1, keepdims=True)
        acc[rows, :] += w * _dot(h.astype(BF16), wd_ref[0, 0])
        return carry

    lax.fori_loop(lo, hi, body, 0)

    @pl.when(e == MOE_EXPERTS - 1)
    def _():
        y = x_ref[...] + gt_ref[0] * _dot(pt_sc[...], acc[...].astype(BF16))
        o_ref[...] = _rms(y) * nf_ref[...] if final_norm else y


def moe_experts(xt, g, shift, scale, gate, comb, layer, w_gate, w_up, w_down, norm_final, final_norm, rows_per_mod,
                tm):
    n = xt.shape[0]
    tiles_per_mod = rows_per_mod // tm
    gid = comb[:, MOE_GROUP_LANE].astype(jnp.int32).reshape(n // tm, tm)
    counts = jnp.sum(gid[:, :, None] == jnp.arange(MOE_GROUPS, dtype=jnp.int32), axis=1, dtype=jnp.int32)
    meta = jnp.concatenate([jnp.cumsum(counts, axis=1) - counts, counts], axis=1)
    mod_spec = pl.BlockSpec((1, 1, D_MODEL), lambda i, e, m: (i // tiles_per_mod, 0, 0))
    return pl.pallas_call(
        functools.partial(_moe_kernel, final_norm),
        name="moe_experts",
        grid_spec=pltpu.PrefetchScalarGridSpec(
            num_scalar_prefetch=1,
            grid=(n // tm, MOE_EXPERTS),
            in_specs=[pl.BlockSpec((tm, D_MODEL), lambda i, e, m: (i, 0)),
                      pl.BlockSpec((1, D_MODEL), lambda i, e, m: (0, 0)), mod_spec, mod_spec, mod_spec,
                      pl.BlockSpec((tm, LANE), lambda i, e, m: (i, 0)),
                      pl.BlockSpec((1, 1, D_MODEL, MOE_FF), lambda i, e, m: (layer, e, 0, 0)),
                      pl.BlockSpec((1, 1, D_MODEL, MOE_FF), lambda i, e, m: (layer, e, 0, 0)),
                      pl.BlockSpec((1, 1, MOE_FF, D_MODEL), lambda i, e, m: (layer, e, 0, 0)),
                      pl.BlockSpec((1, D_MODEL), lambda i, e, m: (0, 0))],
            out_specs=pl.BlockSpec((tm, D_MODEL), lambda i, e, m: (i, 0)),
            scratch_shapes=[pltpu.VMEM((tm, tm), BF16), pltpu.VMEM((tm, D_MODEL), BF16),
                            pltpu.VMEM((tm, LANE), F32), pltpu.VMEM((tm, D_MODEL), F32)]),
        out_shape=jax.ShapeDtypeStruct((n, D_MODEL), F32),
        compiler_params=_params(("parallel", "arbitrary")),
    )(meta, xt, g, shift, scale, gate, comb, w_gate, w_up, w_down, norm_final)


def _mix_weights(w_in):
    o = 0
    parts = []
    for cols, padded in ((HY_COLS, HY_COLS), (S5_COLS, S5_COLS), (RW_COLS, RW_PAD), (M2_COLS, M2_PAD)):
        parts.append(jnp.pad(w_in[:, o:o + cols], ((0, 0), (0, padded - cols))))
        o += cols
    w_gate = w_in[:, o:].reshape(D_MODEL, N_BRANCH, D_MODEL).transpose(1, 0, 2)
    return jnp.concatenate(parts, axis=1).astype(BF16), w_gate.astype(BF16)


def kernel(x, c, ctx, c_ctx, mod_w, mod_b, norm_mix, norm_ffn, w_in, hy_conv_w, hy_conv_b, hy_f_w1, hy_f_b1, hy_f_w2, hy_f_b2, hy_f_w3, hy_f_freq, hy_bias, s5_lam_re, s5_lam_im, s5_log_step, s5_b_re, s5_b_im, s5_c_re, s5_c_im, s5_d, s5_w_glu, rw_mu, rw_w0, rw_w2, rw_a0, rw_a2, rw_g2, rw_k_k, rw_k_a, rw_r_k, rw_ln_w, rw_ln_b, m2_conv_w, m2_conv_b, m2_a_log, m2_dt_bias, m2_d, m2_norm_w, w_branch, w_out, moe_w_group, moe_w_expert, moe_w_gate, moe_w_up, moe_w_down, norm_final):
    b_, L, _ = x.shape
    lc = ctx.shape[1]
    depth = mod_w.shape[0]
    tm = 512
    tmc = min(tm, lc)

    cvec = jnp.zeros((SUBLANE, D_MODEL), F32).at[:b_].set(c).at[b_].set(c_ctx)
    mod = adaln_mod(cvec, mod_w, mod_b)

    xt = x.reshape(b_ * L, D_MODEL)
    ct = ctx.reshape(b_ * lc, D_MODEL)
    moe_wg, moe_wu, moe_wd = moe_w_gate.astype(BF16), moe_w_up.astype(BF16), moe_w_down.astype(BF16)
    for l in range(depth):
        ctx_out = l < depth - 1
        mx = mod[l, :b_].reshape(b_, 1, N_MOD, D_MODEL)
        mc = mod[l, b_:b_ + 1].reshape(1, 1, N_MOD, D_MODEL)
        sh1, sc1, g1, sh2, sc2, g2 = (mx[:, :, i] for i in range(N_MOD))
        csh1, csc1, cg1, csh2, csc2, cg2 = (mc[:, :, i] for i in range(N_MOD))
        w_mix, w_gate = _mix_weights(w_in[l])
        nm, nf = norm_mix[l][None], norm_ffn[l][None]

        s5_prm = s5_prepare(s5_lam_re[l], s5_lam_im[l], s5_log_step[l], s5_b_re[l], s5_b_im[l], s5_c_re[l],
                            s5_c_im[l])
        rw_prm = rwkv_prepare(rw_mu[l], rw_w0[l], rw_w2[l], rw_a0[l], rw_a2[l], rw_g2[l], rw_k_k[l], rw_k_a[l],
                              rw_r_k[l], rw_ln_w[l], rw_ln_b[l])
        m2_prm = mamba_prepare(m2_conv_w[l], m2_conv_b[l], m2_a_log[l], m2_dt_bias[l], m2_d[l], m2_norm_w[l])
        hy_f = (hy_f_w1[l], hy_f_b1[l], hy_f_w2[l], hy_f_b2[l], hy_f_w3[l], hy_f_freq[l])

        def mixers(tokens, n_tok, shift, scale, rows_per_mod, tile, states, want_hyena):
            hy, s5, rw, m2z, m2x, m2dt = in_projection(tokens, nm, shift, scale, w_mix, rows_per_mod, tile)
            seq = lambda a: a.reshape(b_, n_tok, a.shape[-1])
            y_hy = None
            if want_hyena:
                t = min(HY_BLOCK, n_tok)
                ghat = hyena_filter_spectra(hyena_filter(n_tok, *hy_f), n_tok, t)
                y_hy = hyena_mixer(seq(hy), ghat, hy_conv_w[l], hy_conv_b[l], hy_bias[l])
            y_s5, s5_h = s5_mixer(seq(s5), states[0], *s5_prm, s5_d[l], s5_w_glu[l])
            y_rw, rw_h = rwkv_mixer(seq(rw), states[1], rw_prm)
            y_m2, m2_h = mamba_mixer(seq(m2z), seq(m2x), seq(m2dt), states[2], m2_prm)
            flat = lambda a: None if a is None else a.reshape(b_ * n_tok, W_BR)
            return [flat(y_hy), flat(y_s5), flat(y_rw), flat(y_m2)], (s5_h, rw_h, m2_h)

        zero_states = (jnp.zeros((b_, SUBLANE, 2 * S5_LANES), F32), jnp.zeros((b_, 2, W_BR, W_BR), F32),
                       jnp.zeros((b_, 2, M2_STATE, W_BR), F32))
        ys_c, ctx_states = mixers(ct, lc, csh1, csc1, b_ * lc, tmc, zero_states, ctx_out)
        ys_x, _ = mixers(xt, L, sh1, sc1, L, tm, ctx_states, True)

        w_route = jnp.zeros((D_MODEL, LANE), F32)
        w_route = w_route.at[:, :MOE_EXPERTS].set(moe_w_expert[l].transpose(1, 0, 2).reshape(D_MODEL, MOE_EXPERTS))
        w_route = w_route.at[:, MOE_EXPERTS:MOE_EXPERTS + MOE_GROUPS].set(moe_w_group[l])
        w_route = jnp.stack(_split2(w_route))
        wb, wo = w_branch[l].astype(BF16), w_out[l].astype(BF16)
        nfin = norm_final[None]

        def channel_mix(tokens, ys, mods, rows_per_mod, tile, final):
            s1, c1, gt1, s2, c2, gt2 = mods
            t1 = merge_branches(tokens, nm, s1, c1, gt1, ys, w_gate, wb, wo, rows_per_mod, tile)
            comb = moe_router(t1, nf, s2, c2, w_route, rows_per_mod, tile)
            moe_tile = min(MOE_TILE, rows_per_mod)
            return moe_experts(t1, nf, s2, c2, gt2, comb, l, moe_wg, moe_wu, moe_wd, nfin, final, rows_per_mod,
                               moe_tile)

        xt = channel_mix(xt, ys_x, (sh1, sc1, g1, sh2, sc2, g2), L, tm, l == depth - 1)
        if ctx_out:
            ct = channel_mix(ct, ys_c, (csh1, csc1, cg1, csh2, csc2, cg2), b_ * lc, tmc, False)
    return xt.reshape(b_, L, D_MODEL)
```

```python
import functools
import math

import numpy as np
import jax
import jax.numpy as jnp
from jax import lax
from jax.experimental import pallas as pl
from jax.experimental.pallas import tpu as pltpu

F32 = jnp.float32
BF16 = jnp.bfloat16
HI = lax.Precision.HIGHEST

D_MODEL = 1024
W_BR = 256
N_BRANCH = 4
N_MOD = 6
NORM_EPS = 1e-6
GRID_W = 64

HY_BANDS = 8
HY_HID = 64
HY_TARGET = 1e-2
HY_FAST_PCT = 0.3
HY_SLOW_PCT = 1.5
HY_BLOCK = 512

S5_GROUP = 16
S5_GROUPS = 16
S5_STATE = 64
S5_LANES = S5_GROUPS * S5_STATE
S5_CHUNK = 512

RW_HEAD = 64
RW_HEADS = 4
RW_W_RANK = 32
RW_A_RANK = 32
RW_G_RANK = 64
RW_GN_EPS = 64e-5
RW_SUB = 64
RW_STEP = 512

M2_HEADDIM = 64
M2_HEADS = 4
M2_GROUPS = 2
M2_STATE = 128
M2_CHUNK = 256

MOE_GROUPS = 4
MOE_PER_GROUP = 4
MOE_EXPERTS = 16
MOE_FF = 512
MOE_GROUP_LANE = MOE_EXPERTS
MOE_TILE = 1024
MOE_SUB = 128

HY_COLS = 3 * W_BR
S5_COLS = W_BR
RW_COLS = 3 * W_BR + 2 * RW_W_RANK + 2 * RW_A_RANK + RW_G_RANK
M2_XBC = W_BR + 2 * M2_GROUPS * M2_STATE
M2_COLS = W_BR + M2_XBC + 2 * M2_HEADS
RW_PAD = 1024
M2_PAD = 1152
LANE = 128
SUBLANE = 8

VMEM_LIMIT = 56 * 1024 * 1024


def _dot(a, b, prec=None):
    return jnp.dot(a, b, preferred_element_type=F32, precision=prec)


def _dot_nt(a, b, prec=None):
    return lax.dot_general(a, b, (((1,), (1,)), ((), ())), preferred_element_type=F32, precision=prec)


def _dot_tn(a, b, prec=None):
    return lax.dot_general(a, b, (((0,), (0,)), ((), ())), preferred_element_type=F32, precision=prec)


def _bdot(a, b):
    return jnp.dot(a.astype(BF16), b.astype(BF16), preferred_element_type=F32)


def _sigmoid(x):
    return 1.0 / (1.0 + jnp.exp(-x))


def _silu(x):
    return x * _sigmoid(x)


def _params(sem):
    return pltpu.CompilerParams(dimension_semantics=sem, vmem_limit_bytes=VMEM_LIMIT)


def _rms(x):
    return x * lax.rsqrt(jnp.mean(x * x, axis=-1, keepdims=True) + NORM_EPS)


def _modulate(x, g, shift, scale):
    return _rms(x) * g * (1.0 + scale) + shift


def _iota(shape, axis):
    return lax.broadcasted_iota(jnp.int32, shape, axis)


def _shift_rows(cur, prev_row, next_row):
    t = cur.shape[0]
    row = _iota(cur.shape, 0)
    xm = jnp.where(row == 0, prev_row, pltpu.roll(cur, 1, 0))
    xp = jnp.where(row == t - 1, next_row, pltpu.roll(cur, t - 1, 0))
    return xm, xp


def _halo_specs(t, n_blocks, width, col_block, chunk_of):
    per = t // SUBLANE
    last = n_blocks * per - 1

    def prev_map(b, s):
        return (b, jnp.maximum(chunk_of(s) * per - 1, 0), col_block)

    def next_map(b, s):
        return (b, jnp.minimum((chunk_of(s) + 1) * per, last), col_block)

    return (pl.BlockSpec((1, SUBLANE, width), prev_map), pl.BlockSpec((1, SUBLANE, width), next_map))


def _halo_rows(prev_ref, next_ref, chunk, n_chunks):
    prev_row = jnp.where(chunk == 0, 0.0, prev_ref[0, SUBLANE - 1:SUBLANE, :])
    next_row = jnp.where(chunk == n_chunks - 1, 0.0, next_ref[0, 0:1, :])
    return prev_row, next_row


def _mod_kernel(c_ref, w_ref, b_ref, o_ref):
    w_hi, w_lo = _split2(w_ref[0])
    o_ref[0] = _dot_split(_silu(c_ref[...]), w_hi, w_lo) + b_ref[0]


def adaln_mod(cvec, mod_w, mod_b):
    depth = mod_w.shape[0]
    return pl.pallas_call(
        _mod_kernel,
        name="adaln_mod",
        grid=(depth, N_MOD),
        in_specs=[pl.BlockSpec((SUBLANE, D_MODEL), lambda l, j: (0, 0)),
                  pl.BlockSpec((1, D_MODEL, D_MODEL), lambda l, j: (l, 0, j)),
                  pl.BlockSpec((1, 1, D_MODEL), lambda l, j: (l, 0, j))],
        out_specs=pl.BlockSpec((1, SUBLANE, D_MODEL), lambda l, j: (l, 0, j)),
        out_shape=jax.ShapeDtypeStruct((depth, SUBLANE, N_MOD * D_MODEL), F32),
        compiler_params=_params(("parallel", "parallel")),
    )(cvec, mod_w, mod_b.reshape(depth, 1, N_MOD * D_MODEL))


def _inproj_kernel(x_ref, g_ref, sh_ref, sc_ref, w_ref, *out_refs):
    h = _modulate(x_ref[...], g_ref[...], sh_ref[0], sc_ref[0]).astype(BF16)
    o = 0
    for ref in out_refs:
        n = ref.shape[-1]
        ref[...] = _dot(h, w_ref[:, o:o + n])
        o += n


def in_projection(xt, g, shift, scale, w_mix, rows_per_mod, tm):
    n = xt.shape[0]
    tiles_per_mod = rows_per_mod // tm
    widths = (HY_COLS, S5_COLS, RW_PAD, W_BR, M2_XBC, M2_PAD - W_BR - M2_XBC)
    mod_spec = pl.BlockSpec((1, 1, D_MODEL), lambda i: (i // tiles_per_mod, 0, 0))
    return pl.pallas_call(
        _inproj_kernel,
        name="in_projection",
        grid=(n // tm,),
        in_specs=[pl.BlockSpec((tm, D_MODEL), lambda i: (i, 0)),
                  pl.BlockSpec((1, D_MODEL), lambda i: (0, 0)),
                  mod_spec, mod_spec,
                  pl.BlockSpec(w_mix.shape, lambda i: (0, 0))],
        out_specs=[pl.BlockSpec((tm, w), lambda i: (i, 0)) for w in widths],
        out_shape=[jax.ShapeDtypeStruct((n, w), F32) for w in widths],
        compiler_params=_params(("parallel",)),
    )(xt, g, shift, scale, w_mix)


def _dft_mats(t):
    k = np.arange(t, dtype=np.float64)[:, None]
    n = np.arange(2 * t, dtype=np.float64)[None, :]
    ang = np.pi * (2.0 * k + 1.0) * n / (2.0 * t)
    fwd = np.concatenate([np.cos(ang), -np.sin(ang)], axis=0)
    inv = fwd[:, :t].T / t
    return fwd, inv


def _split_const(m):
    hi = jnp.asarray(m, F32).astype(BF16)
    lo = (jnp.asarray(m, F32) - hi.astype(F32)).astype(BF16)
    return jnp.stack([hi, lo])


def _dot_split_lhs(c_ref, b):
    b_hi, b_lo = _split2(b)
    return _dot(c_ref[0], b_hi) + _dot(c_ref[0], b_lo) + _dot(c_ref[1], b_hi)


def _hyfilt_kernel(w1_ref, b1_ref, w2_ref, b2_ref, w3_ref, fr_ref, o_ref):
    L = o_ref.shape[0]
    half_rows = L // 2
    shape = (half_rows, LANE)
    lane = _iota(shape, 1)
    sub = lane % HY_HID
    pos = (_iota(shape, 0) + (lane // HY_HID) * half_rows).astype(F32)
    t = pos / (L - 1)
    w = (2.0 * math.pi / L) * pos
    band = jnp.where(sub >= 1 + HY_BANDS, sub - 1 - HY_BANDS, sub - 1).astype(F32)
    f = 1e-4 + band * ((HY_BANDS - 1 - 1e-4) / (HY_BANDS - 1))
    arg = f * w
    feats = jnp.where(sub == 0, t,
                      jnp.where(sub <= HY_BANDS, jnp.cos(arg),
                                jnp.where(sub <= 2 * HY_BANDS, -jnp.sin(arg), 0.0)))
    fr = fr_ref[...]
    h = jnp.sin(fr * (_dot(feats, w1_ref[...], HI) + b1_ref[...]))
    h = jnp.sin(fr * (_dot(h, w2_ref[...], HI) + b2_ref[...]))
    h = _dot(h, w3_ref[...], HI)
    ch = _iota((half_rows, W_BR), 1).astype(F32)
    lo = math.log(HY_TARGET) / HY_SLOW_PCT
    hi = math.log(HY_TARGET) / HY_FAST_PCT
    deltas = jnp.abs(lo + ch * ((hi - lo) / (W_BR - 1)))
    for part in range(2):
        decay = jnp.exp(-t[:, part * HY_HID:part * HY_HID + 1] * deltas)
        for side in range(2):
            col = (2 * part + side) * W_BR
            o_ref[part * half_rows:(part + 1) * half_rows, side * W_BR:(side + 1) * W_BR] = (
                h[:, col:col + W_BR] * decay)


def hyena_filter(L, f_w1, f_b1, f_w2, f_b2, f_w3, f_freq):
    def twice(a):
        r, c = a.shape
        out = jnp.zeros((2 * HY_HID, 2 * c), F32)
        return out.at[:r, :c].set(a).at[HY_HID:HY_HID + r, c:].set(a)

    row2 = lambda v: jnp.concatenate([v, v])[None]
    args = (twice(f_w1), row2(f_b1), twice(f_w2), row2(f_b2), twice(f_w3), row2(f_freq))
    return pl.pallas_call(
        _hyfilt_kernel,
        name="hyena_filter",
        out_shape=jax.ShapeDtypeStruct((L, 2 * W_BR), F32),
        compiler_params=_params(None),
    )(*args)


def _ghat_kernel(f_ref, g_ref, o_ref):
    o_ref[0] = _dot_split_lhs(f_ref, g_ref[0])


def hyena_filter_spectra(filt, L, t):
    nb = L // t
    table = jnp.concatenate([filt[:, :W_BR], filt[:, W_BR:], jnp.zeros((1, W_BR), F32)], axis=0)
    m = np.arange(2 * t)[None, :]
    d = np.arange(-(nb - 1), nb)[:, None]
    off = d * t + np.where(m < t, m, m - 2 * t)
    idx = np.where(m == t, 2 * L, np.where(off >= 0, off, L - off)).astype(np.int32)
    sign = np.where(m == t, 0.0, np.where(m < t, 1.0, -1.0)).astype(np.float32) * np.ones_like(off, np.float32)
    g = table[idx] * sign[:, :, None]
    fwd, _ = _dft_mats(t)
    return pl.pallas_call(
        _ghat_kernel,
        name="hyena_filter_spectra",
        grid=(2 * nb - 1,),
        in_specs=[pl.BlockSpec((2, 2 * t, 2 * t), lambda d: (0, 0, 0)),
                  pl.BlockSpec((1, 2 * t, W_BR), lambda d: (d, 0, 0))],
        out_specs=pl.BlockSpec((1, 2 * t, W_BR), lambda d: (d, 0, 0)),
        out_shape=jax.ShapeDtypeStruct((2 * nb - 1, 2 * t, W_BR), F32),
        compiler_params=_params(("parallel",)),
    )(_split_const(fwd), g)


def _hypre_kernel(nb, cur_ref, prev_ref, next_ref, cw_ref, cb_ref, f_ref, x0_ref, u_ref, uh_ref):
    j = pl.program_id(1)
    cur = cur_ref[0]
    prev_row, next_row = _halo_rows(prev_ref, next_ref, j, nb)
    xm, xp = _shift_rows(cur, prev_row, next_row)
    pc = cb_ref[...] + xm * cw_ref[0:1, :] + cur * cw_ref[1:2, :] + xp * cw_ref[2:3, :]
    u = pc[:, W_BR:2 * W_BR] * pc[:, 2 * W_BR:3 * W_BR]
    x0_ref[0] = pc[:, 0:W_BR]
    u_ref[0] = u
    uh_ref[0, 0] = _dot_split_lhs(f_ref, u)


def _hymain_kernel(nb, t, g_ref, uh_ref, x0_ref, u_ref, bias_ref, inv_ref, o_ref):
    i = pl.program_id(2)
    acc_re = jnp.zeros((t, LANE), F32)
    acc_im = jnp.zeros((t, LANE), F32)
    for j in range(nb):
        g = g_ref[i - j + nb - 1]
        gr, gi = g[0:t], g[t:2 * t]
        ur, ui = uh_ref[0, j, 0:t, :], uh_ref[0, j, t:2 * t, :]
        acc_re += gr * ur - gi * ui
        acc_im += gr * ui + gi * ur
    y = _dot_split_lhs(inv_ref, jnp.concatenate([acc_re, acc_im], axis=0))
    o_ref[0] = x0_ref[0] * (y + bias_ref[...] * u_ref[0])


def hyena_mixer(hy, ghat, conv_w, conv_b, h_bias):
    b_, L, _ = hy.shape
    t = min(HY_BLOCK, L)
    nb = L // t
    fwd, inv = _dft_mats(t)
    prev_spec, next_spec = _halo_specs(t, nb, HY_COLS, 0, lambda s: s)
    x0, u, uh = pl.pallas_call(
        functools.partial(_hypre_kernel, nb),
        name="hyena_conv_dft",
        grid=(b_, nb),
        in_specs=[pl.BlockSpec((1, t, HY_COLS), lambda b, j: (b, j, 0)), prev_spec, next_spec,
                  pl.BlockSpec((3, HY_COLS), lambda b, j: (0, 0)),
                  pl.BlockSpec((1, HY_COLS), lambda b, j: (0, 0)),
                  pl.BlockSpec((2, 2 * t, t), lambda b, j: (0, 0, 0))],
        out_specs=[pl.BlockSpec((1, t, W_BR), lambda b, j: (b, j, 0)),
                   pl.BlockSpec((1, t, W_BR), lambda b, j: (b, j, 0)),
                   pl.BlockSpec((1, 1, 2 * t, W_BR), lambda b, j: (b, j, 0, 0))],
        out_shape=[jax.ShapeDtypeStruct((b_, L, W_BR), F32), jax.ShapeDtypeStruct((b_, L, W_BR), F32),
                   jax.ShapeDtypeStruct((b_, nb, 2 * t, W_BR), F32)],
        compiler_params=_params(("parallel", "parallel")),
    )(hy, hy, hy, conv_w, conv_b[None], _split_const(fwd[:, :t]))
    ncb = W_BR // LANE
    return pl.pallas_call(
        functools.partial(_hymain_kernel, nb, t),
        name="hyena_longconv",
        grid=(ncb, b_, nb),
        in_specs=[pl.BlockSpec((2 * nb - 1, 2 * t, LANE), lambda c, b, i: (0, 0, c)),
                  pl.BlockSpec((1, nb, 2 * t, LANE), lambda c, b, i: (b, 0, 0, c)),
                  pl.BlockSpec((1, t, LANE), lambda c, b, i: (b, i, c)),
                  pl.BlockSpec((1, t, LANE), lambda c, b, i: (b, i, c)),
                  pl.BlockSpec((1, LANE), lambda c, b, i: (0, c)),
                  pl.BlockSpec((2, t, 2 * t), lambda c, b, i: (0, 0, 0))],
        out_specs=pl.BlockSpec((1, t, LANE), lambda c, b, i: (b, i, c)),
        out_shape=jax.ShapeDtypeStruct((b_, L, W_BR), F32),
        compiler_params=_params(("parallel", "parallel", "parallel")),
    )(ghat, uh, x0, u, h_bias[None], _split_const(inv))


def _s5_kernel(nc, u_ref, h0_ref, bb_ref, ap_ref, cb_ref, dsk_ref, wg_ref, y_ref, hf_ref, yacc, state):
    s = pl.program_id(1)
    t = u_ref.shape[1]
    n = S5_LANES

    @pl.when(s == 0)
    def _():
        hf_ref[...] = jnp.zeros_like(hf_ref)

    def phase(d):
        chunk = s if d == 0 else 2 * nc - 1 - s
        first = (s == 0) if d == 0 else (s == nc)
        last = (s == nc - 1) if d == 0 else (s == 2 * nc - 1)

        @pl.when(first)
        def _():
            state[0:1, :] = h0_ref[0, d:d + 1, :]

        u = u_ref[0]
        x = _dot(u.astype(BF16), bb_ref[d])
        row = _iota((t, n), 0)
        st = state[0:1, :]
        pw = (lambda p: p - 1) if d == 0 else (lambda p: SUBLANE - p)
        a_re, a_im = ap_ref[d, pw(1):pw(1) + 1, 0:n], ap_ref[d, pw(1):pw(1) + 1, n:2 * n]
        c_re = a_re * st[:, 0:n] - a_im * st[:, n:2 * n]
        c_im = a_re * st[:, n:2 * n] + a_im * st[:, 0:n]
        entry = 0 if d == 0 else t - 1
        x_re = x[:, 0:n] + jnp.where(row == entry, c_re, 0.0)
        x_im = x[:, n:2 * n] + jnp.where(row == entry, c_im, 0.0)
        n_grp = t // SUBLANE
        x_re, x_im = x_re.reshape(n_grp, SUBLANE, n), x_im.reshape(n_grp, SUBLANE, n)
        sub = _iota((SUBLANE, n), 0)
        for sh in (1, 2, 4):
            valid = (sub >= sh) if d == 0 else (sub < SUBLANE - sh)
            a_re = jnp.where(valid, ap_ref[d, pw(sh):pw(sh) + 1, 0:n], 0.0)[None]
            a_im = jnp.where(valid, ap_ref[d, pw(sh):pw(sh) + 1, n:2 * n], 0.0)[None]
            rot = sh if d == 0 else SUBLANE - sh
            s_re, s_im = pltpu.roll(x_re, rot, 1), pltpu.roll(x_im, rot, 1)
            x_re, x_im = x_re + a_re * s_re - a_im * s_im, x_im + a_re * s_im + a_im * s_re
        p_re, p_im = ap_ref[d, :, 0:n], ap_ref[d, :, n:2 * n]
        out_re, out_im = [None] * n_grp, [None] * n_grp
        c_re = c_im = None
        for g in (range(n_grp) if d == 0 else range(n_grp - 1, -1, -1)):
            g_re, g_im = x_re[g], x_im[g]
            if c_re is not None:
                g_re, g_im = g_re + p_re * c_re - p_im * c_im, g_im + p_re * c_im + p_im * c_re
            out_re[g], out_im[g] = g_re, g_im
            edge = SUBLANE - 1 if d == 0 else 0
            c_re, c_im = g_re[edge:edge + 1], g_im[edge:edge + 1]
        x_re, x_im = jnp.concatenate(out_re, axis=0), jnp.concatenate(out_im, axis=0)
        ex = t - 1 if d == 0 else 0
        state[0:1, 0:n] = x_re[ex:ex + 1, :]
        state[0:1, n:2 * n] = x_im[ex:ex + 1, :]
        yd = _dot(x_re.astype(BF16), cb_ref[d, 0:n, :]) + _dot(x_im.astype(BF16), cb_ref[d, n:2 * n, :])
        rows = pl.ds(pl.multiple_of(chunk * t, t), t)
        if d == 0:
            yacc[rows, :] = dsk_ref[...] * u + yd
        else:
            y = yacc[rows, :] + yd
            y = 0.5 * y * (1.0 + jnp.tanh(math.sqrt(2.0 / math.pi) * (y + 0.044715 * y * y * y)))
            y_ref[0] = y * _sigmoid(_bdot(y, wg_ref[...]))

        @pl.when(last)
        def _():
            hf_ref[0, d:d + 1, :] = state[0:1, :]

    @pl.when(s < nc)
    def _():
        phase(0)

    @pl.when(s >= nc)
    def _():
        phase(1)


def _two_pass_chunk(nc):
    return lambda s: jnp.where(s < nc, s, 2 * nc - 1 - s)


def _two_pass_out(nc):
    return lambda s: jnp.where(s < nc, nc - 1, 2 * nc - 1 - s)


def s5_prepare(lam_re, lam_im, log_step, b_re, b_im, c_re, c_im):
    step = jnp.exp(log_step)[:, :, None]
    mag = jnp.exp(lam_re * step)
    lb_re, lb_im = mag * jnp.cos(lam_im * step), mag * jnp.sin(lam_im * step)
    den = lam_re * lam_re + lam_im * lam_im
    q_re = ((lb_re - 1.0) * lam_re + lb_im * lam_im) / den
    q_im = (lb_im * lam_re - (lb_re - 1.0) * lam_im) / den
    bb_re = q_re[..., None] * b_re - q_im[..., None] * b_im
    bb_im = q_re[..., None] * b_im + q_im[..., None] * b_re
    eye = jnp.eye(S5_GROUPS, dtype=F32)

    def blockdiag_in(m):
        return jnp.einsum('dgni,gh->dgihn', m, eye).reshape(2, W_BR, S5_LANES)

    def blockdiag_out(m):
        return jnp.einsum('dgin,gh->dgnhi', m, eye).reshape(2, S5_LANES, W_BR)

    bblk = jnp.concatenate([blockdiag_in(bb_re), blockdiag_in(bb_im)], axis=-1)
    cblk = jnp.concatenate([blockdiag_out(c_re), -blockdiag_out(c_im)], axis=1)
    j = np.arange(SUBLANE, dtype=np.float32)
    pw = jnp.asarray(np.stack([j + 1.0, SUBLANE - j]))[:, :, None]
    arg_re = (lam_re * step).reshape(2, 1, S5_LANES) * pw
    arg_im = (lam_im * step).reshape(2, 1, S5_LANES) * pw
    apow = jnp.concatenate([jnp.exp(arg_re) * jnp.cos(arg_im), jnp.exp(arg_re) * jnp.sin(arg_im)], axis=-1)
    return bblk.astype(BF16), apow, cblk.astype(BF16)


def s5_mixer(u, h0, bblk, apow, cblk, d_skip, w_glu):
    b_, L, _ = u.shape
    t = min(S5_CHUNK, L)
    nc = L // t
    chunk_of, out_of = _two_pass_chunk(nc), _two_pass_out(nc)
    const = lambda a: pl.BlockSpec(a.shape, lambda b, s: (0,) * a.ndim)
    dsk = d_skip[None]
    wg = w_glu.astype(BF16)
    return pl.pallas_call(
        functools.partial(_s5_kernel, nc),
        name="s5_mixer",
        grid=(b_, 2 * nc),
        in_specs=[pl.BlockSpec((1, t, W_BR), lambda b, s: (b, chunk_of(s), 0)),
                  pl.BlockSpec((1, SUBLANE, 2 * S5_LANES), lambda b, s: (b, 0, 0)),
                  const(bblk), const(apow), const(cblk), const(dsk), const(wg)],
        out_specs=[pl.BlockSpec((1, t, W_BR), lambda b, s: (b, out_of(s), 0)),
                   pl.BlockSpec((1, SUBLANE, 2 * S5_LANES), lambda b, s: (b, 0, 0))],
        out_shape=[jax.ShapeDtypeStruct((b_, L, W_BR), F32),
                   jax.ShapeDtypeStruct((b_, SUBLANE, 2 * S5_LANES), F32)],
        scratch_shapes=[pltpu.VMEM((L, W_BR), F32), pltpu.VMEM((SUBLANE, 2 * S5_LANES), F32)],
        compiler_params=_params(("parallel", "arbitrary")),
    )(u, h0, bblk, apow, cblk, dsk, wg)


def _head_mask(h, shape, axis):
    lane = _iota(shape, axis)
    return (lane >= h * RW_HEAD) & (lane < (h + 1) * RW_HEAD)


def _split3(x):
    hi = x.astype(BF16)
    r1 = x - hi.astype(F32)
    mid = r1.astype(BF16)
    return hi, mid, (r1 - mid.astype(F32)).astype(BF16)


def _stack_heads(x):
    xb = x.astype(BF16)
    zero = jnp.zeros_like(xb)
    return jnp.concatenate([jnp.where(_head_mask(h, xb.shape, 1), xb, zero) for h in range(RW_HEADS)], axis=0)


def _split2(x):
    hi = x.astype(BF16)
    return hi, (x - hi.astype(F32)).astype(BF16)


def _dot_exact_rhs(a, b):
    hi, lo = _split2(a)
    return _dot(hi, b) + _dot(lo, b)


def _dot_split(a, b_hi, b_lo):
    hi, lo = _split2(a)
    return _dot(hi, b_hi) + _dot(hi, b_lo) + _dot(lo, b_hi)


RW_INV_BASE = 8


def _cat_to_blockdiag(x):
    t, n = x.shape
    xb = x.astype(BF16)
    tiled = jnp.concatenate([xb] * (n // t), axis=0)
    same = (_iota((n, n), 0) // t) == (_iota((n, n), 1) // t)
    return jnp.where(same, tiled, jnp.zeros_like(tiled))


def _tri_inverse(nmat):
    t = nmat[0].shape[0]
    row, col = _iota(nmat[0].shape, 0), _iota(nmat[0].shape, 1) % t
    same = lambda s: (row // s) == (col // s)
    mul = lambda x, y: _dot(x.astype(BF16), _cat_to_blockdiag(y))
    base = [jnp.where(same(RW_INV_BASE), x, 0.0) for x in nmat]
    eye = jnp.where(row == col, 1.0, 0.0)
    m = [eye - x for x in base]
    p = base
    for _ in range(RW_INV_BASE.bit_length() - 2):
        p = [mul(x, x) for x in p]
        m = [x + mul(x, y) for x, y in zip(m, p)]
    s = 2 * RW_INV_BASE
    while s <= t:
        off = same(s) & jnp.logical_not(same(s // 2))
        mc = [mul(x, jnp.where(off, y, 0.0)) for x, y in zip(m, nmat)]
        m = [x - mul(y, x) for x, y in zip(m, mc)]
        s *= 2
    return m


def _wkv_local(d, r, logw, kd, v, kk, a):
    t = r[0].shape[0]
    bf = lambda x: x.astype(BF16)
    each = lambda f, *ls: [f(*xs) for xs in zip(*ls)]
    ri, ci = _iota((t, t), 0), _iota((t, t), 1)
    tri = ((ci <= ri) if d == 0 else (ci >= ri)).astype(BF16)
    row, col = _iota((t, RW_HEADS * t), 0), _iota((t, RW_HEADS * t), 1) % t
    incl = (col <= row) if d == 0 else (col >= row)
    strict = (col < row) if d == 0 else (col > row)
    dn = W_BR
    diag = _iota((dn, dn), 0) == _iota((dn, dn), 1)

    cs = each(lambda x: sum(_dot(tri, part) for part in _split3(x)), logw)
    g_tot = each(lambda x: jnp.exp(jnp.sum(x, axis=0, keepdims=True)), logw)
    g_inv = each(lambda c: jnp.exp(-c), cs)
    bt = each(lambda k_, a_, g: k_ * a_ * g, kk, a, g_inv)
    kt = each(lambda k_, g: k_ * g, kd, g_inv)
    qa = each(lambda k_, c, w: k_ * jnp.exp(c - w), kk, cs, logw)
    rh = each(lambda r_, c: r_ * jnp.exp(c), r, cs)
    bts, kts, vs = each(_stack_heads, bt), each(_stack_heads, kt), each(_stack_heads, v)
    bhs = each(lambda x, g: _stack_heads(x * g), bt, g_tot)
    khs = each(lambda x, g: _stack_heads(x * g), kt, g_tot)
    a_ab = each(lambda q, b: jnp.where(strict, _dot_nt(bf(q), b), 0.0), qa, bts)
    a_ak = each(lambda q, k_: bf(jnp.where(strict, _dot_nt(bf(q), k_), 0.0)), qa, kts)
    a_rb = each(lambda q, b: bf(jnp.where(incl, _dot_nt(bf(q), b), 0.0)), rh, bts)
    a_rk = each(lambda q, k_: bf(jnp.where(incl, _dot_nt(bf(q), k_), 0.0)), rh, kts)
    m = each(bf, _tri_inverse(a_ab))
    x1 = each(lambda a_, v_: _stack_heads(_dot(a_, v_)), a_ak, vs)
    ul = each(lambda m_, x: _stack_heads(-_dot(m_, x)), m, x1)
    w0 = each(lambda m_, q: _stack_heads(_dot(m_, _stack_heads(q))), m, qa)
    yloc = each(lambda ak, v_, ab, u: _dot(ak, v_) + _dot(ab, u), a_rk, vs, a_rb, ul)
    yh = each(lambda r_, ab, w: r_ - _dot(ab, w), rh, a_rb, w0)
    p_bd = each(lambda g, b, w: jnp.where(diag, g, 0.0) - _dot_tn(b, w), g_tot, bhs, w0)
    q_bd = each(lambda b, u, k_, v_: _dot_tn(b, u) + _dot_tn(k_, v_), bhs, ul, khs, vs)
    return list(zip(yloc, yh, p_bd, q_bd))


def _rwkv_kernel(ng, cur_ref, prev_ref, next_ref, h0_ref, mu_ref, w0_ref, a0_ref, lrh_ref, lrl_ref,
                 kk_ref, ka_ref, rk_ref, lnw_ref, lnb_ref, y_ref, hf_ref, yacc, hst):
    s = pl.program_id(1)
    t = cur_ref.shape[1]
    n = W_BR
    ri, ci = _iota((n, n), 0), _iota((n, n), 1)
    head_sum = ((ri // RW_HEAD) == (ci // RW_HEAD)).astype(BF16)

    def phase(d):
        chunk = s if d == 0 else 2 * ng - 1 - s
        first = (s == 0) if d == 0 else (s == ng)
        last = (s == ng - 1) if d == 0 else (s == 2 * ng - 1)

        @pl.when(first)
        def _():
            hst[...] = h0_ref[0, d]

        cur = cur_ref[0]
        prev_row, next_row = _halo_rows(prev_ref, next_ref, chunk, ng)
        xm, xp = _shift_rows(cur, prev_row, next_row)
        p = cur + (0.5 * (xm + xp) - cur) * mu_ref[...]
        r, k, v, lr = p[:, 0:n], p[:, n:2 * n], p[:, 2 * n:3 * n], p[:, 3 * n:4 * n]
        kk = k * kk_ref[...]
        kk = kk * lax.rsqrt(jnp.maximum(_dot_exact_rhs(kk * kk, head_sum), 1e-24))
        lane = _iota(lr.shape, 1)
        feats = jnp.where(lane < 2 * RW_W_RANK, jnp.tanh(lr),
                          jnp.where(lane < 2 * (RW_W_RANK + RW_A_RANK), lr, _sigmoid(lr)))
        cols = (2 if d == 0 else 3) * n
        low = _dot_split(feats, lrh_ref[d, :, 0:cols], lrl_ref[d, :, 0:cols])
        w = w0_ref[d:d + 1, :] + low[:, 0:n]
        logw = -math.exp(-0.5) * _sigmoid(w)
        a = _sigmoid(a0_ref[d:d + 1, :] + low[:, n:2 * n])
        kd = k * (1.0 + (a - 1.0) * ka_ref[...])
        nsub = t // RW_SUB
        subs = lambda x: [x[q * RW_SUB:(q + 1) * RW_SUB] for q in range(nsub)]
        local = _wkv_local(d, subs(r), subs(logw), subs(kd), subs(v), subs(kk), subs(a))
        h = hst[...]
        ys = [None] * nsub
        for q in (range(nsub) if d == 0 else range(nsub - 1, -1, -1)):
            yloc, yh, p_bd, q_bd = local[q]
            hb = h.astype(BF16)
            ys[q] = yloc + _dot(yh.astype(BF16), hb)
            h = _dot(p_bd.astype(BF16), hb) + q_bd
        hst[...] = h
        yd = jnp.concatenate(ys, axis=0)
        rows = pl.ds(pl.multiple_of(chunk * t, t), t)
        if d == 0:
            yacc[rows, :] = yd
        else:
            y = yacc[rows, :] + yd
            mean = _dot_exact_rhs(y, head_sum) * (1.0 / RW_HEAD)
            yc = y - mean
            var = _dot_exact_rhs(yc * yc, head_sum) * (1.0 / RW_HEAD)
            y = yc * lax.rsqrt(var + RW_GN_EPS) * lnw_ref[...] + lnb_ref[...]
            bonus = _dot_exact_rhs(r * k * rk_ref[...], head_sum) * v
            y_ref[0] = (y + bonus) * low[:, 2 * n:3 * n]

        @pl.when(last)
        def _():
            hf_ref[0, d] = hst[...]

    @pl.when(s < ng)
    def _():
        phase(0)

    @pl.when(s >= ng)
    def _():
        phase(1)


def rwkv_prepare(mu, w0, w2, a0, a2, g2, k_k, k_a, r_k, ln_w, ln_b):
    n = W_BR
    mu_p = jnp.zeros((1, RW_PAD), F32).at[0, :RW_COLS].set(mu)
    w2_p = jnp.zeros((2, n, n), F32)
    a2_p = jnp.zeros((2, n, n), F32)
    for d in range(2):
        w2_p = w2_p.at[d, d * RW_W_RANK:(d + 1) * RW_W_RANK].set(w2[d])
        o = 2 * RW_W_RANK + d * RW_A_RANK
        a2_p = a2_p.at[d, o:o + RW_A_RANK].set(a2[d])
    o = 2 * RW_W_RANK + 2 * RW_A_RANK
    g2_p = jnp.zeros((n, n), F32).at[o:o + RW_G_RANK].set(g2)
    low = jnp.concatenate([w2_p, a2_p, jnp.broadcast_to(g2_p, (2, n, n))], axis=-1)
    low_hi = low.astype(BF16)
    low_lo = (low - low_hi.astype(F32)).astype(BF16)
    return (mu_p, w0, a0, low_hi, low_lo, k_k[None], k_a[None], r_k.reshape(1, n), ln_w[None], ln_b[None])


def rwkv_mixer(rw, h0, prm):
    b_, L, _ = rw.shape
    t = min(RW_STEP, L)
    ng = L // t
    chunk_of, out_of = _two_pass_chunk(ng), _two_pass_out(ng)
    prev_spec, next_spec = _halo_specs(t, ng, RW_PAD, 0, chunk_of)
    const = lambda a: pl.BlockSpec(a.shape, lambda b, s: (0,) * a.ndim)
    st_spec = pl.BlockSpec((1, 2, W_BR, W_BR), lambda b, s: (b, 0, 0, 0))
    return pl.pallas_call(
        functools.partial(_rwkv_kernel, ng),
        name="rwkv_mixer",
        grid=(b_, 2 * ng),
        in_specs=[pl.BlockSpec((1, t, RW_PAD), lambda b, s: (b, chunk_of(s), 0)), prev_spec, next_spec, st_spec]
                 + [const(a) for a in prm],
        out_specs=[pl.BlockSpec((1, t, W_BR), lambda b, s: (b, out_of(s), 0)), st_spec],
        out_shape=[jax.ShapeDtypeStruct((b_, L, W_BR), F32), jax.ShapeDtypeStruct((b_, 2, W_BR, W_BR), F32)],
        scratch_shapes=[pltpu.VMEM((L, W_BR), F32), pltpu.VMEM((W_BR, W_BR), F32)],
        compiler_params=_params(("parallel", "arbitrary")),
    )(rw, rw, rw, h0, *prm)


def _m2_kernel(nc, z_ref, cur_ref, prev_ref, next_ref, dt_ref, h0_ref, cw_ref, cb_ref, ex_ref, a_ref, dtb_ref,
               dsk_ref, nw_ref, y_ref, hf_ref, yacc, xbc_sc, hst):
    s = pl.program_id(1)
    t = cur_ref.shape[1]
    n = W_BR
    ns = M2_STATE
    ri, ci = _iota((t, t), 0), _iota((t, t), 1)

    def phase(d):
        chunk = s if d == 0 else 2 * nc - 1 - s
        first = (s == 0) if d == 0 else (s == nc)
        last = (s == nc - 1) if d == 0 else (s == 2 * nc - 1)

        @pl.when(first)
        def _():
            hst[...] = h0_ref[0, d]

        rows = pl.ds(pl.multiple_of(chunk * t, t), t)
        if d == 0:
            cur = cur_ref[0]
            prev_row, next_row = _halo_rows(prev_ref, next_ref, chunk, nc)
            xm, xp = _shift_rows(cur, prev_row, next_row)
            xbc = _silu(cb_ref[...] + xm * cw_ref[0:1, :] + cur * cw_ref[1:2, :] + xp * cw_ref[2:3, :])
            xbc_sc[rows, :] = xbc
        else:
            xbc = xbc_sc[rows, :]
        xs, bm, cm = xbc[:, 0:n], xbc[:, n:2 * n], xbc[:, 2 * n:3 * n]
        raw = sum(_dot(part, ex_ref[d]) for part in _split3(dt_ref[0])) + dtb_ref[d:d + 1, :]
        dtd = jnp.maximum(raw, 0.0) + jnp.log(1.0 + jnp.exp(-jnp.abs(raw)))
        da = dtd * a_ref[d:d + 1, :]
        incl = (ci <= ri) if d == 0 else (ci >= ri)
        tri = incl.astype(BF16)
        da_parts = _split3(da)
        acs = sum(_dot(tri, part) for part in da_parts)
        tot = jnp.sum(da, axis=0, keepdims=True)
        xdt = xs * dtd
        xdt_b = xdt.astype(BF16)
        bm_b, cm_b = bm.astype(BF16), cm.astype(BF16)
        sel = jnp.where(_iota((SUBLANE, n), 1) // M2_HEADDIM == _iota((SUBLANE, n), 0), 1.0 / M2_HEADDIM,
                        0.0).astype(BF16)
        acs_t = sum(_dot_nt(sel, part) for part in _split3(acs))
        ydiag = jnp.zeros((t, n), F32)
        for g in range(M2_GROUPS):
            cb = _dot_nt(cm_b[:, g * ns:(g + 1) * ns], bm_b[:, g * ns:(g + 1) * ns])
            for h in range(g * (M2_HEADS // M2_GROUPS), (g + 1) * (M2_HEADS // M2_GROUPS)):
                seg = acs[:, h * M2_HEADDIM:h * M2_HEADDIM + 1] - acs_t[h:h + 1, :]
                scores = cb * jnp.exp(jnp.where(incl, seg, -jnp.inf))
                ydiag = jnp.where(_head_mask(h, (t, n), 1), _dot(scores.astype(BF16), xdt_b), ydiag)
        h_in = hst[...]
        h_b = h_in.astype(BF16)
        lane = _iota((t, n), 1)
        yoff = jnp.where(lane < n // M2_GROUPS, _dot(cm_b[:, 0:ns], h_b), _dot(cm_b[:, ns:2 * ns], h_b))
        yd = ydiag + jnp.exp(acs) * yoff
        xdec = (xdt * jnp.exp(tot - acs)).astype(BF16)
        lane_s = _iota((ns, n), 1)
        new = jnp.where(lane_s < n // M2_GROUPS, _dot_tn(bm_b[:, 0:ns], xdec), _dot_tn(bm_b[:, ns:2 * ns], xdec))
        hst[...] = h_in * jnp.exp(tot) + new
        if d == 0:
            yacc[rows, :] = yd + dsk_ref[...] * xs
        else:
            y = (yacc[rows, :] + yd) * _silu(z_ref[0])
            y_ref[0] = _rms(y) * nw_ref[...]

        @pl.when(last)
        def _():
            hf_ref[0, d] = hst[...]

    @pl.when(s < nc)
    def _():
        phase(0)

    @pl.when(s >= nc)
    def _():
        phase(1)


def mamba_prepare(conv_w, conv_b, a_log, dt_bias, d_skip, norm_w):
    n = W_BR
    rep = lambda v: jnp.repeat(v, M2_HEADDIM, axis=-1)
    expand = jnp.zeros((2, LANE, n), F32)
    for d in range(2):
        for h in range(M2_HEADS):
            expand = expand.at[d, d * M2_HEADS + h, h * M2_HEADDIM:(h + 1) * M2_HEADDIM].set(1.0)
    return (conv_w, conv_b[None], expand.astype(BF16), rep(-jnp.exp(a_log)), rep(dt_bias), rep(d_skip)[None],
            norm_w[None])


def mamba_mixer(z, xbc, dt, h0, prm):
    b_, L, _ = z.shape
    t = min(M2_CHUNK, L)
    nc = L // t
    chunk_of, out_of = _two_pass_chunk(nc), _two_pass_out(nc)
    prev_spec, next_spec = _halo_specs(t, nc, M2_XBC, 0, chunk_of)
    const = lambda a: pl.BlockSpec(a.shape, lambda b, s: (0,) * a.ndim)
    st_spec = pl.BlockSpec((1, 2, M2_STATE, W_BR), lambda b, s: (b, 0, 0, 0))
    seq = lambda w: pl.BlockSpec((1, t, w), lambda b, s: (b, chunk_of(s), 0))
    return pl.pallas_call(
        functools.partial(_m2_kernel, nc),
        name="mamba_mixer",
        grid=(b_, 2 * nc),
        in_specs=[seq(W_BR), seq(M2_XBC), prev_spec, next_spec, seq(LANE), st_spec] + [const(a) for a in prm],
        out_specs=[pl.BlockSpec((1, t, W_BR), lambda b, s: (b, out_of(s), 0)), st_spec],
        out_shape=[jax.ShapeDtypeStruct((b_, L, W_BR), F32),
                   jax.ShapeDtypeStruct((b_, 2, M2_STATE, W_BR), F32)],
        scratch_shapes=[pltpu.VMEM((L, W_BR), F32), pltpu.VMEM((L, M2_XBC), F32),
                        pltpu.VMEM((M2_STATE, W_BR), F32)],
        compiler_params=_params(("parallel", "arbitrary")),
    )(z, xbc, xbc, xbc, dt, h0, *prm)


def _merge_kernel(x_ref, g_ref, sh_ref, sc_ref, gt_ref, y0_ref, y1_ref, y2_ref, y3_ref, wg_ref, wb_ref, wo_ref,
                  o_ref):
    x = x_ref[...]
    h = _modulate(x, g_ref[...], sh_ref[0], sc_ref[0]).astype(BF16)
    acc = jnp.zeros(x.shape, F32)
    for i, y_ref in enumerate((y0_ref, y1_ref, y2_ref, y3_ref)):
        gate = _sigmoid(_dot(h, wg_ref[i]))
        acc += gate * _dot(y_ref[...].astype(BF16), wb_ref[i])
    o_ref[...] = x + gt_ref[0] * _dot(acc.astype(BF16), wo_ref[...])


def merge_branches(xt, g, shift, scale, gate, ys, w_gate, w_branch, w_out, rows_per_mod, tm):
    n = xt.shape[0]
    tiles_per_mod = rows_per_mod // tm
    mod_spec = pl.BlockSpec((1, 1, D_MODEL), lambda i: (i // tiles_per_mod, 0, 0))
    const = lambda a: pl.BlockSpec(a.shape, lambda i: (0,) * a.ndim)
    tok = lambda w: pl.BlockSpec((tm, w), lambda i: (i, 0))
    return pl.pallas_call(
        _merge_kernel,
        name="merge_branches",
        grid=(n // tm,),
        in_specs=[tok(D_MODEL), const(g), mod_spec, mod_spec, mod_spec] + [tok(W_BR)] * N_BRANCH
                 + [const(w_gate), const(w_branch), const(w_out)],
        out_specs=tok(D_MODEL),
        out_shape=jax.ShapeDtypeStruct((n, D_MODEL), F32),
        compiler_params=_params(("parallel",)),
    )(xt, g, shift, scale, gate, *ys, w_gate, w_branch, w_out)


def _router_kernel(x_ref, g_ref, sh_ref, sc_ref, wr_ref, comb_ref):
    u = _modulate(x_ref[...], g_ref[...], sh_ref[0], sc_ref[0])
    logits = _dot_split(u, wr_ref[0], wr_ref[1])
    lane = _iota(logits.shape, 1).astype(F32)
    neg = -jnp.inf
    is_grp = (lane >= MOE_EXPERTS) & (lane < MOE_EXPERTS + MOE_GROUPS)
    gl = jnp.where(is_grp, logits, neg)
    gmax = jnp.max(gl, axis=-1, keepdims=True)
    grp_p = 1.0 / jnp.sum(jnp.exp(gl - gmax), axis=-1, keepdims=True)
    grp_idx = jnp.min(jnp.where(gl == gmax, lane, 4.0 * LANE), axis=-1, keepdims=True) - MOE_EXPERTS
    in_grp = (lane >= grp_idx * MOE_PER_GROUP) & (lane < (grp_idx + 1) * MOE_PER_GROUP)
    el = jnp.where(in_grp, logits, neg)
    v1 = jnp.max(el, axis=-1, keepdims=True)
    i1 = jnp.min(jnp.where(el == v1, lane, 4.0 * LANE), axis=-1, keepdims=True)
    el2 = jnp.where(lane == i1, neg, el)
    v2 = jnp.max(el2, axis=-1, keepdims=True)
    i2 = jnp.min(jnp.where(el2 == v2, lane, 4.0 * LANE), axis=-1, keepdims=True)
    e2 = jnp.exp(v2 - v1)
    w1 = grp_p / (1.0 + e2)
    comb = jnp.where(lane == i1, w1, jnp.where(lane == i2, w1 * e2, 0.0))
    comb_ref[...] = jnp.where(lane == MOE_GROUP_LANE, grp_idx, comb)


def moe_router(xt, g, shift, scale, w_route, rows_per_mod, tm):
    n = xt.shape[0]
    tiles_per_mod = rows_per_mod // tm
    mod_spec = pl.BlockSpec((1, 1, D_MODEL), lambda i: (i // tiles_per_mod, 0, 0))
    return pl.pallas_call(
        _router_kernel,
        name="moe_router",
        grid=(n // tm,),
        in_specs=[pl.BlockSpec((tm, D_MODEL), lambda i: (i, 0)), pl.BlockSpec((1, D_MODEL), lambda i: (0, 0)),
                  mod_spec, mod_spec, pl.BlockSpec((2, D_MODEL, LANE), lambda i: (0, 0, 0))],
        out_specs=pl.BlockSpec((tm, LANE), lambda i: (i, 0)),
        out_shape=jax.ShapeDtypeStruct((n, LANE), F32),
        compiler_params=_params(("parallel",)),
    )(xt, g, shift, scale, w_route)


def _moe_kernel(final_norm, meta_ref, x_ref, g_ref, sh_ref, sc_ref, gt_ref, comb_ref, wg_ref, wu_ref, wd_ref, nf_ref,
                o_ref, pt_sc, xs_sc, cs_sc, acc):
    i = pl.program_id(0)
    e = pl.program_id(1)
    tm = x_ref.shape[0]

    @pl.when(e == 0)
    def _():
        u = _modulate(x_ref[...], g_ref[...], sh_ref[0], sc_ref[0]).astype(BF16)
        comb = comb_ref[...]
        lane = _iota((tm, LANE), 1)
        member = comb[:, MOE_GROUP_LANE:MOE_GROUP_LANE + 1] == lane.astype(F32)
        tri = jnp.where(_iota((tm, tm), 1) <= _iota((tm, tm), 0), 1.0, 0.0).astype(BF16)
        rank = _dot(tri, jnp.where(member, 1.0, 0.0).astype(BF16))
        offs = jnp.zeros((tm, LANE), jnp.int32)
        for grp in range(MOE_GROUPS):
            offs = jnp.where(lane == grp, meta_ref[i, grp], offs)
        posmat = jnp.where(member, offs.astype(F32) + rank - 1.0, 0.0)
        pos_col = jnp.sum(posmat, axis=-1, keepdims=True)
        ones = jnp.ones((SUBLANE, LANE), BF16)
        pos_row = sum(_dot_nt(ones, part) for part in _split3(posmat))[0:1, :]
        pt_sc[...] = jnp.where(pos_col == _iota((tm, tm), 1).astype(F32), 1.0, 0.0).astype(BF16)
        p = jnp.where(pos_row == _iota((tm, tm), 0).astype(F32), 1.0, 0.0).astype(BF16)
        xs_sc[...] = _dot(p, u).astype(BF16)
        cs_sc[...] = sum(_dot(p, part) for part in _split3(comb))
        acc[...] = jnp.zeros_like(acc)

    grp = e // MOE_PER_GROUP
    start = meta_ref[i, grp]
    count = meta_ref[i, MOE_GROUPS + grp]
    lo = start // MOE_SUB
    hi = jnp.where(count > 0, (start + count + MOE_SUB - 1) // MOE_SUB, lo)

    def body(j, carry):
        rows = pl.ds(pl.multiple_of(j * MOE_SUB, MOE_SUB), MOE_SUB)
        xs = xs_sc[rows, :]
        h = _silu(_dot(xs, wg_ref[0, 0])) * _dot(xs, wu_ref[0, 0])
        cs = cs_sc[rows, :]
        w = jnp.sum(jnp.where(_iota(cs.shape, 1) == e, cs, 0.0), axis=-1, keepdims=True)
        acc[rows, :] += w * _dot(h.astype(BF16), wd_ref[0, 0])
        return carry

    lax.fori_loop(lo, hi, body, 0)

    @pl.when(e == MOE_EXPERTS - 1)
    def _():
        y = x_ref[...] + gt_ref[0] * _dot(pt_sc[...], acc[...].astype(BF16))
        o_ref[...] = _rms(y) * nf_ref[...] if final_norm else y


def moe_experts(xt, g, shift, scale, gate, comb, layer, w_gate, w_up, w_down, norm_final, final_norm, rows_per_mod,
                tm):
    n = xt.shape[0]
    tiles_per_mod = rows_per_mod // tm
    gid = comb[:, MOE_GROUP_LANE].astype(jnp.int32).reshape(n // tm, tm)
    counts = jnp.sum(gid[:, :, None] == jnp.arange(MOE_GROUPS, dtype=jnp.int32), axis=1, dtype=jnp.int32)
    meta = jnp.concatenate([jnp.cumsum(counts, axis=1) - counts, counts], axis=1)
    mod_spec = pl.BlockSpec((1, 1, D_MODEL), lambda i, e, m: (i // tiles_per_mod, 0, 0))
    return pl.pallas_call(
        functools.partial(_moe_kernel, final_norm),
        name="moe_experts",
        grid_spec=pltpu.PrefetchScalarGridSpec(
            num_scalar_prefetch=1,
            grid=(n // tm, MOE_EXPERTS),
            in_specs=[pl.BlockSpec((tm, D_MODEL), lambda i, e, m: (i, 0)),
                      pl.BlockSpec((1, D_MODEL), lambda i, e, m: (0, 0)), mod_spec, mod_spec, mod_spec,
                      pl.BlockSpec((tm, LANE), lambda i, e, m: (i, 0)),
                      pl.BlockSpec((1, 1, D_MODEL, MOE_FF), lambda i, e, m: (layer, e, 0, 0)),
                      pl.BlockSpec((1, 1, D_MODEL, MOE_FF), lambda i, e, m: (layer, e, 0, 0)),
                      pl.BlockSpec((1, 1, MOE_FF, D_MODEL), lambda i, e, m: (layer, e, 0, 0)),
                      pl.BlockSpec((1, D_MODEL), lambda i, e, m: (0, 0))],
            out_specs=pl.BlockSpec((tm, D_MODEL), lambda i, e, m: (i, 0)),
            scratch_shapes=[pltpu.VMEM((tm, tm), BF16), pltpu.VMEM((tm, D_MODEL), BF16),
                            pltpu.VMEM((tm, LANE), F32), pltpu.VMEM((tm, D_MODEL), F32)]),
        out_shape=jax.ShapeDtypeStruct((n, D_MODEL), F32),
        compiler_params=_params(("parallel", "arbitrary")),
    )(meta, xt, g, shift, scale, gate, comb, w_gate, w_up, w_down, norm_final)


def _mix_weights(w_in):
    o = 0
    parts = []
    for cols, padded in ((HY_COLS, HY_COLS), (S5_COLS, S5_COLS), (RW_COLS, RW_PAD), (M2_COLS, M2_PAD)):
        parts.append(jnp.pad(w_in[:, o:o + cols], ((0, 0), (0, padded - cols))))
        o += cols
    w_gate = w_in[:, o:].reshape(D_MODEL, N_BRANCH, D_MODEL).transpose(1, 0, 2)
    return jnp.concatenate(parts, axis=1).astype(BF16), w_gate.astype(BF16)


def kernel(x, c, ctx, c_ctx, mod_w, mod_b, norm_mix, norm_ffn, w_in, hy_conv_w, hy_conv_b, hy_f_w1, hy_f_b1, hy_f_w2, hy_f_b2, hy_f_w3, hy_f_freq, hy_bias, s5_lam_re, s5_lam_im, s5_log_step, s5_b_re, s5_b_im, s5_c_re, s5_c_im, s5_d, s5_w_glu, rw_mu, rw_w0, rw_w2, rw_a0, rw_a2, rw_g2, rw_k_k, rw_k_a, rw_r_k, rw_ln_w, rw_ln_b, m2_conv_w, m2_conv_b, m2_a_log, m2_dt_bias, m2_d, m2_norm_w, w_branch, w_out, moe_w_group, moe_w_expert, moe_w_gate, moe_w_up, moe_w_down, norm_final):
    b_, L, _ = x.shape
    lc = ctx.shape[1]
    depth = mod_w.shape[0]
    tm = 512
    tmc = min(tm, lc)

    cvec = jnp.zeros((SUBLANE, D_MODEL), F32).at[:b_].set(c).at[b_].set(c_ctx)
    mod = adaln_mod(cvec, mod_w, mod_b)

    xt = x.reshape(b_ * L, D_MODEL)
    ct = ctx.reshape(b_ * lc, D_MODEL)
    moe_wg, moe_wu, moe_wd = moe_w_gate.astype(BF16), moe_w_up.astype(BF16), moe_w_down.astype(BF16)
    for l in range(depth):
        ctx_out = l < depth - 1
        mx = mod[l, :b_].reshape(b_, 1, N_MOD, D_MODEL)
        mc = mod[l, b_:b_ + 1].reshape(1, 1, N_MOD, D_MODEL)
        sh1, sc1, g1, sh2, sc2, g2 = (mx[:, :, i] for i in range(N_MOD))
        csh1, csc1, cg1, csh2, csc2, cg2 = (mc[:, :, i] for i in range(N_MOD))
        w_mix, w_gate = _mix_weights(w_in[l])
        nm, nf = norm_mix[l][None], norm_ffn[l][None]

        s5_prm = s5_prepare(s5_lam_re[l], s5_lam_im[l], s5_log_step[l], s5_b_re[l], s5_b_im[l], s5_c_re[l],
                            s5_c_im[l])
        rw_prm = rwkv_prepare(rw_mu[l], rw_w0[l], rw_w2[l], rw_a0[l], rw_a2[l], rw_g2[l], rw_k_k[l], rw_k_a[l],
                              rw_r_k[l], rw_ln_w[l], rw_ln_b[l])
        m2_prm = mamba_prepare(m2_conv_w[l], m2_conv_b[l], m2_a_log[l], m2_dt_bias[l], m2_d[l], m2_norm_w[l])
        hy_f = (hy_f_w1[l], hy_f_b1[l], hy_f_w2[l], hy_f_b2[l], hy_f_w3[l], hy_f_freq[l])

        def mixers(tokens, n_tok, shift, scale, rows_per_mod, tile, states, want_hyena):
            hy, s5, rw, m2z, m2x, m2dt = in_projection(tokens, nm, shift, scale, w_mix, rows_per_mod, tile)
            seq = lambda a: a.reshape(b_, n_tok, a.shape[-1])
            y_hy = None
            if want_hyena:
                t = min(HY_BLOCK, n_tok)
                ghat = hyena_filter_spectra(hyena_filter(n_tok, *hy_f), n_tok, t)
                y_hy = hyena_mixer(seq(hy), ghat, hy_conv_w[l], hy_conv_b[l], hy_bias[l])
            y_s5, s5_h = s5_mixer(seq(s5), states[0], *s5_prm, s5_d[l], s5_w_glu[l])
            y_rw, rw_h = rwkv_mixer(seq(rw), states[1], rw_prm)
            y_m2, m2_h = mamba_mixer(seq(m2z), seq(m2x), seq(m2dt), states[2], m2_prm)
            flat = lambda a: None if a is None else a.reshape(b_ * n_tok, W_BR)
            return [flat(y_hy), flat(y_s5), flat(y_rw), flat(y_m2)], (s5_h, rw_h, m2_h)

        zero_states = (jnp.zeros((b_, SUBLANE, 2 * S5_LANES), F32), jnp.zeros((b_, 2, W_BR, W_BR), F32),
                       jnp.zeros((b_, 2, M2_STATE, W_BR), F32))
        ys_c, ctx_states = mixers(ct, lc, csh1, csc1, b_ * lc, tmc, zero_states, ctx_out)
        ys_x, _ = mixers(xt, L, sh1, sc1, L, tm, ctx_states, True)

        w_route = jnp.zeros((D_MODEL, LANE), F32)
        w_route = w_route.at[:, :MOE_EXPERTS].set(moe_w_expert[l].transpose(1, 0, 2).reshape(D_MODEL, MOE_EXPERTS))
        w_route = w_route.at[:, MOE_EXPERTS:MOE_EXPERTS + MOE_GROUPS].set(moe_w_group[l])
        w_route = jnp.stack(_split2(w_route))
        wb, wo = w_branch[l].astype(BF16), w_out[l].astype(BF16)
        nfin = norm_final[None]

        def channel_mix(tokens, ys, mods, rows_per_mod, tile, final):
            s1, c1, gt1, s2, c2, gt2 = mods
            t1 = merge_branches(tokens, nm, s1, c1, gt1, ys, w_gate, wb, wo, rows_per_mod, tile)
            comb = moe_router(t1, nf, s2, c2, w_route, rows_per_mod, tile)
            moe_tile = min(MOE_TILE, rows_per_mod)
            return moe_experts(t1, nf, s2, c2, gt2, comb, l, moe_wg, moe_wu, moe_wd, nfin, final, rows_per_mod,
                               moe_tile)

        xt = channel_mix(xt, ys_x, (sh1, sc1, g1, sh2, sc2, g2), L, tm, l == depth - 1)
        if ctx_out:
            ct = channel_mix(ct, ys_c, (csh1, csc1, cg1, csh2, csc2, cg2), b_ * lc, tmc, False)
    return xt.reshape(b_, L, D_MODEL)
```

```python
import functools
import math

import numpy as np
import jax
import jax.numpy as jnp
from jax import lax
from jax.experimental import pallas as pl
from jax.experimental.pallas import tpu as pltpu

F32 = jnp.float32
BF16 = jnp.bfloat16
HI = lax.Precision.HIGHEST

D_MODEL = 1024
W_BR = 256
N_BRANCH = 4
N_MOD = 6
NORM_EPS = 1e-6
GRID_W = 64

HY_BANDS = 8
HY_HID = 64
HY_TARGET = 1e-2
HY_FAST_PCT = 0.3
HY_SLOW_PCT = 1.5
HY_BLOCK = 512

S5_GROUP = 16
S5_GROUPS = 16
S5_STATE = 64
S5_LANES = S5_GROUPS * S5_STATE
S5_CHUNK = 1024

RW_HEAD = 64
RW_HEADS = 4
RW_W_RANK = 32
RW_A_RANK = 32
RW_G_RANK = 64
RW_GN_EPS = 64e-5
RW_SUB = 64
RW_STEP = 512

M2_HEADDIM = 64
M2_HEADS = 4
M2_GROUPS = 2
M2_STATE = 128
M2_CHUNK = 256

MOE_GROUPS = 4
MOE_PER_GROUP = 4
MOE_EXPERTS = 16
MOE_FF = 512
MOE_GROUP_LANE = MOE_EXPERTS
MOE_TILE = 1024
MOE_SUB = 128

HY_COLS = 3 * W_BR
S5_COLS = W_BR
RW_COLS = 3 * W_BR + 2 * RW_W_RANK + 2 * RW_A_RANK + RW_G_RANK
M2_XBC = W_BR + 2 * M2_GROUPS * M2_STATE
M2_COLS = W_BR + M2_XBC + 2 * M2_HEADS
RW_PAD = 1024
M2_PAD = 1152
LANE = 128
SUBLANE = 8

VMEM_LIMIT = 56 * 1024 * 1024


def _dot(a, b, prec=None):
    return jnp.dot(a, b, preferred_element_type=F32, precision=prec)


def _dot_nt(a, b, prec=None):
    return lax.dot_general(a, b, (((1,), (1,)), ((), ())), preferred_element_type=F32, precision=prec)


def _dot_tn(a, b, prec=None):
    return lax.dot_general(a, b, (((0,), (0,)), ((), ())), preferred_element_type=F32, precision=prec)


def _bdot(a, b):
    return jnp.dot(a.astype(BF16), b.astype(BF16), preferred_element_type=F32)


def _sigmoid(x):
    return 1.0 / (1.0 + jnp.exp(-x))


def _silu(x):
    return x * _sigmoid(x)


def _params(sem):
    return pltpu.CompilerParams(dimension_semantics=sem, vmem_limit_bytes=VMEM_LIMIT)


def _rms(x):
    return x * lax.rsqrt(jnp.mean(x * x, axis=-1, keepdims=True) + NORM_EPS)


def _modulate(x, g, shift, scale):
    return _rms(x) * g * (1.0 + scale) + shift


def _iota(shape, axis):
    return lax.broadcasted_iota(jnp.int32, shape, axis)


def _shift_rows(cur, prev_row, next_row):
    t = cur.shape[0]
    row = _iota(cur.shape, 0)
    xm = jnp.where(row == 0, prev_row, pltpu.roll(cur, 1, 0))
    xp = jnp.where(row == t - 1, next_row, pltpu.roll(cur, t - 1, 0))
    return xm, xp


def _halo_specs(t, n_blocks, width, col_block, chunk_of):
    per = t // SUBLANE
    last = n_blocks * per - 1

    def prev_map(b, s):
        return (b, jnp.maximum(chunk_of(s) * per - 1, 0), col_block)

    def next_map(b, s):
        return (b, jnp.minimum((chunk_of(s) + 1) * per, last), col_block)

    return (pl.BlockSpec((1, SUBLANE, width), prev_map), pl.BlockSpec((1, SUBLANE, width), next_map))


def _halo_rows(prev_ref, next_ref, chunk, n_chunks):
    prev_row = jnp.where(chunk == 0, 0.0, prev_ref[0, SUBLANE - 1:SUBLANE, :])
    next_row = jnp.where(chunk == n_chunks - 1, 0.0, next_ref[0, 0:1, :])
    return prev_row, next_row


def _mod_kernel(c_ref, w_ref, b_ref, o_ref):
    w_hi, w_lo = _split2(w_ref[0])
    o_ref[0] = _dot_split(_silu(c_ref[...]), w_hi, w_lo) + b_ref[0]


def adaln_mod(cvec, mod_w, mod_b):
    depth = mod_w.shape[0]
    return pl.pallas_call(
        _mod_kernel,
        name="adaln_mod",
        grid=(depth, N_MOD),
        in_specs=[pl.BlockSpec((SUBLANE, D_MODEL), lambda l, j: (0, 0)),
                  pl.BlockSpec((1, D_MODEL, D_MODEL), lambda l, j: (l, 0, j)),
                  pl.BlockSpec((1, 1, D_MODEL), lambda l, j: (l, 0, j))],
        out_specs=pl.BlockSpec((1, SUBLANE, D_MODEL), lambda l, j: (l, 0, j)),
        out_shape=jax.ShapeDtypeStruct((depth, SUBLANE, N_MOD * D_MODEL), F32),
        compiler_params=_params(("parallel", "parallel")),
    )(cvec, mod_w, mod_b.reshape(depth, 1, N_MOD * D_MODEL))


def _inproj_kernel(x_ref, g_ref, sh_ref, sc_ref, w_ref, *out_refs):
    h = _modulate(x_ref[...], g_ref[...], sh_ref[0], sc_ref[0]).astype(BF16)
    o = 0
    for ref in out_refs:
        n = ref.shape[-1]
        ref[...] = _dot(h, w_ref[:, o:o + n])
        o += n


def in_projection(xt, g, shift, scale, w_mix, rows_per_mod, tm):
    n = xt.shape[0]
    tiles_per_mod = rows_per_mod // tm
    widths = (HY_COLS, S5_COLS, RW_PAD, W_BR, M2_XBC, M2_PAD - W_BR - M2_XBC)
    mod_spec = pl.BlockSpec((1, 1, D_MODEL), lambda i: (i // tiles_per_mod, 0, 0))
    return pl.pallas_call(
        _inproj_kernel,
        name="in_projection",
        grid=(n // tm,),
        in_specs=[pl.BlockSpec((tm, D_MODEL), lambda i: (i, 0)),
                  pl.BlockSpec((1, D_MODEL), lambda i: (0, 0)),
                  mod_spec, mod_spec,
                  pl.BlockSpec(w_mix.shape, lambda i: (0, 0))],
        out_specs=[pl.BlockSpec((tm, w), lambda i: (i, 0)) for w in widths],
        out_shape=[jax.ShapeDtypeStruct((n, w), F32) for w in widths],
        compiler_params=_params(("parallel",)),
    )(xt, g, shift, scale, w_mix)


def _dft_mats(t):
    k = np.arange(t, dtype=np.float64)[:, None]
    n = np.arange(2 * t, dtype=np.float64)[None, :]
    ang = np.pi * (2.0 * k + 1.0) * n / (2.0 * t)
    fwd = np.concatenate([np.cos(ang), -np.sin(ang)], axis=0)
    inv = fwd[:, :t].T / t
    return fwd, inv


def _split_const(m):
    hi = jnp.asarray(m, F32).astype(BF16)
    lo = (jnp.asarray(m, F32) - hi.astype(F32)).astype(BF16)
    return jnp.stack([hi, lo])


def _dot_split_lhs(c_ref, b):
    b_hi, b_lo = _split2(b)
    return _dot(c_ref[0], b_hi) + _dot(c_ref[0], b_lo) + _dot(c_ref[1], b_hi)


def _hyfilt_kernel(w1_ref, b1_ref, w2_ref, b2_ref, w3_ref, fr_ref, o_ref):
    L = o_ref.shape[0]
    half_rows = L // 2
    shape = (half_rows, LANE)
    lane = _iota(shape, 1)
    sub = lane % HY_HID
    pos = (_iota(shape, 0) + (lane // HY_HID) * half_rows).astype(F32)
    t = pos / (L - 1)
    w = (2.0 * math.pi / L) * pos
    band = jnp.where(sub >= 1 + HY_BANDS, sub - 1 - HY_BANDS, sub - 1).astype(F32)
    f = 1e-4 + band * ((HY_BANDS - 1 - 1e-4) / (HY_BANDS - 1))
    arg = f * w
    feats = jnp.where(sub == 0, t,
                      jnp.where(sub <= HY_BANDS, jnp.cos(arg),
                                jnp.where(sub <= 2 * HY_BANDS, -jnp.sin(arg), 0.0)))
    fr = fr_ref[...]
    h = jnp.sin(fr * (_dot(feats, w1_ref[...], HI) + b1_ref[...]))
    h = jnp.sin(fr * (_dot(h, w2_ref[...], HI) + b2_ref[...]))
    h = _dot(h, w3_ref[...], HI)
    ch = _iota((half_rows, W_BR), 1).astype(F32)
    lo = math.log(HY_TARGET) / HY_SLOW_PCT
    hi = math.log(HY_TARGET) / HY_FAST_PCT
    deltas = jnp.abs(lo + ch * ((hi - lo) / (W_BR - 1)))
    for part in range(2):
        decay = jnp.exp(-t[:, part * HY_HID:part * HY_HID + 1] * deltas)
        for side in range(2):
            col = (2 * part + side) * W_BR
            o_ref[part * half_rows:(part + 1) * half_rows, side * W_BR:(side + 1) * W_BR] = (
                h[:, col:col + W_BR] * decay)


def hyena_filter(L, f_w1, f_b1, f_w2, f_b2, f_w3, f_freq):
    def twice(a):
        r, c = a.shape
        out = jnp.zeros((2 * HY_HID, 2 * c), F32)
        return out.at[:r, :c].set(a).at[HY_HID:HY_HID + r, c:].set(a)

    row2 = lambda v: jnp.concatenate([v, v])[None]
    args = (twice(f_w1), row2(f_b1), twice(f_w2), row2(f_b2), twice(f_w3), row2(f_freq))
    return pl.pallas_call(
        _hyfilt_kernel,
        name="hyena_filter",
        out_shape=jax.ShapeDtypeStruct((L, 2 * W_BR), F32),
        compiler_params=_params(None),
    )(*args)


def _ghat_kernel(f_ref, g_ref, o_ref):
    o_ref[0] = _dot_split_lhs(f_ref, g_ref[0])


def hyena_filter_spectra(filt, L, t):
    nb = L // t
    table = jnp.concatenate([filt[:, :W_BR], filt[:, W_BR:], jnp.zeros((1, W_BR), F32)], axis=0)
    m = np.arange(2 * t)[None, :]
    d = np.arange(-(nb - 1), nb)[:, None]
    off = d * t + np.where(m < t, m, m - 2 * t)
    idx = np.where(m == t, 2 * L, np.where(off >= 0, off, L - off)).astype(np.int32)
    sign = np.where(m == t, 0.0, np.where(m < t, 1.0, -1.0)).astype(np.float32) * np.ones_like(off, np.float32)
    g = table[idx] * sign[:, :, None]
    fwd, _ = _dft_mats(t)
    return pl.pallas_call(
        _ghat_kernel,
        name="hyena_filter_spectra",
        grid=(2 * nb - 1,),
        in_specs=[pl.BlockSpec((2, 2 * t, 2 * t), lambda d: (0, 0, 0)),
                  pl.BlockSpec((1, 2 * t, W_BR), lambda d: (d, 0, 0))],
        out_specs=pl.BlockSpec((1, 2 * t, W_BR), lambda d: (d, 0, 0)),
        out_shape=jax.ShapeDtypeStruct((2 * nb - 1, 2 * t, W_BR), F32),
        compiler_params=_params(("parallel",)),
    )(_split_const(fwd), g)


def _hypre_kernel(nb, cur_ref, prev_ref, next_ref, cw_ref, cb_ref, f_ref, x0_ref, u_ref, uh_ref):
    j = pl.program_id(1)
    cur = cur_ref[0]
    prev_row, next_row = _halo_rows(prev_ref, next_ref, j, nb)
    xm, xp = _shift_rows(cur, prev_row, next_row)
    pc = cb_ref[...] + xm * cw_ref[0:1, :] + cur * cw_ref[1:2, :] + xp * cw_ref[2:3, :]
    u = pc[:, W_BR:2 * W_BR] * pc[:, 2 * W_BR:3 * W_BR]
    x0_ref[0] = pc[:, 0:W_BR]
    u_ref[0] = u
    uh_ref[0, 0] = _dot_split_lhs(f_ref, u)


def _hymain_kernel(nb, t, g_ref, uh_ref, x0_ref, u_ref, bias_ref, inv_ref, o_ref):
    i = pl.program_id(2)
    acc_re = jnp.zeros((t, LANE), F32)
    acc_im = jnp.zeros((t, LANE), F32)
    for j in range(nb):
        g = g_ref[i - j + nb - 1]
        gr, gi = g[0:t], g[t:2 * t]
        ur, ui = uh_ref[0, j, 0:t, :], uh_ref[0, j, t:2 * t, :]
        acc_re += gr * ur - gi * ui
        acc_im += gr * ui + gi * ur
    y = _dot_split_lhs(inv_ref, jnp.concatenate([acc_re, acc_im], axis=0))
    o_ref[0] = x0_ref[0] * (y + bias_ref[...] * u_ref[0])


def hyena_mixer(hy, ghat, conv_w, conv_b, h_bias):
    b_, L, _ = hy.shape
    t = min(HY_BLOCK, L)
    nb = L // t
    fwd, inv = _dft_mats(t)
    prev_spec, next_spec = _halo_specs(t, nb, HY_COLS, 0, lambda s: s)
    x0, u, uh = pl.pallas_call(
        functools.partial(_hypre_kernel, nb),
        name="hyena_conv_dft",
        grid=(b_, nb),
        in_specs=[pl.BlockSpec((1, t, HY_COLS), lambda b, j: (b, j, 0)), prev_spec, next_spec,
                  pl.BlockSpec((3, HY_COLS), lambda b, j: (0, 0)),
                  pl.BlockSpec((1, HY_COLS), lambda b, j: (0, 0)),
                  pl.BlockSpec((2, 2 * t, t), lambda b, j: (0, 0, 0))],
        out_specs=[pl.BlockSpec((1, t, W_BR), lambda b, j: (b, j, 0)),
                   pl.BlockSpec((1, t, W_BR), lambda b, j: (b, j, 0)),
                   pl.BlockSpec((1, 1, 2 * t, W_BR), lambda b, j: (b, j, 0, 0))],
        out_shape=[jax.ShapeDtypeStruct((b_, L, W_BR), F32), jax.ShapeDtypeStruct((b_, L, W_BR), F32),
                   jax.ShapeDtypeStruct((b_, nb, 2 * t, W_BR), F32)],
        compiler_params=_params(("parallel", "parallel")),
    )(hy, hy, hy, conv_w, conv_b[None], _split_const(fwd[:, :t]))
    ncb = W_BR // LANE
    return pl.pallas_call(
        functools.partial(_hymain_kernel, nb, t),
        name="hyena_longconv",
        grid=(ncb, b_, nb),
        in_specs=[pl.BlockSpec((2 * nb - 1, 2 * t, LANE), lambda c, b, i: (0, 0, c)),
                  pl.BlockSpec((1, nb, 2 * t, LANE), lambda c, b, i: (b, 0, 0, c)),
                  pl.BlockSpec((1, t, LANE), lambda c, b, i: (b, i, c)),
                  pl.BlockSpec((1, t, LANE), lambda c, b, i: (b, i, c)),
                  pl.BlockSpec((1, LANE), lambda c, b, i: (0, c)),
                  pl.BlockSpec((2, t, 2 * t), lambda c, b, i: (0, 0, 0))],
        out_specs=pl.BlockSpec((1, t, LANE), lambda c, b, i: (b, i, c)),
        out_shape=jax.ShapeDtypeStruct((b_, L, W_BR), F32),
        compiler_params=_params(("parallel", "parallel", "parallel")),
    )(ghat, uh, x0, u, h_bias[None], _split_const(inv))


def _s5_kernel(nc, u_ref, h0_ref, bb_ref, ap_ref, cb_ref, dsk_ref, wg_ref, y_ref, hf_ref, yacc, state):
    s = pl.program_id(1)
    t = u_ref.shape[1]
    n = S5_LANES

    @pl.when(s == 0)
    def _():
        hf_ref[...] = jnp.zeros_like(hf_ref)

    def phase(d):
        chunk = s if d == 0 else 2 * nc - 1 - s
        first = (s == 0) if d == 0 else (s == nc)
        last = (s == nc - 1) if d == 0 else (s == 2 * nc - 1)

        @pl.when(first)
        def _():
            state[0:1, :] = h0_ref[0, d:d + 1, :]

        u = u_ref[0]
        x = _dot(u.astype(BF16), bb_ref[d])
        row = _iota((t, n), 0)
        st = state[0:1, :]
        pw = (lambda p: p - 1) if d == 0 else (lambda p: SUBLANE - p)
        a_re, a_im = ap_ref[d, pw(1):pw(1) + 1, 0:n], ap_ref[d, pw(1):pw(1) + 1, n:2 * n]
        c_re = a_re * st[:, 0:n] - a_im * st[:, n:2 * n]
        c_im = a_re * st[:, n:2 * n] + a_im * st[:, 0:n]
        entry = 0 if d == 0 else t - 1
        x_re = x[:, 0:n] + jnp.where(row == entry, c_re, 0.0)
        x_im = x[:, n:2 * n] + jnp.where(row == entry, c_im, 0.0)
        n_grp = t // SUBLANE
        x_re, x_im = x_re.reshape(n_grp, SUBLANE, n), x_im.reshape(n_grp, SUBLANE, n)
        sub = _iota((SUBLANE, n), 0)
        for sh in (1, 2, 4):
            valid = (sub >= sh) if d == 0 else (sub < SUBLANE - sh)
            a_re = jnp.where(valid, ap_ref[d, pw(sh):pw(sh) + 1, 0:n], 0.0)[None]
            a_im = jnp.where(valid, ap_ref[d, pw(sh):pw(sh) + 1, n:2 * n], 0.0)[None]
            rot = sh if d == 0 else SUBLANE - sh
            s_re, s_im = pltpu.roll(x_re, rot, 1), pltpu.roll(x_im, rot, 1)
            x_re, x_im = x_re + a_re * s_re - a_im * s_im, x_im + a_re * s_im + a_im * s_re
        p_re, p_im = ap_ref[d, :, 0:n], ap_ref[d, :, n:2 * n]
        out_re, out_im = [None] * n_grp, [None] * n_grp
        c_re = c_im = None
        for g in (range(n_grp) if d == 0 else range(n_grp - 1, -1, -1)):
            g_re, g_im = x_re[g], x_im[g]
            if c_re is not None:
                g_re, g_im = g_re + p_re * c_re - p_im * c_im, g_im + p_re * c_im + p_im * c_re
            out_re[g], out_im[g] = g_re, g_im
            edge = SUBLANE - 1 if d == 0 else 0
            c_re, c_im = g_re[edge:edge + 1], g_im[edge:edge + 1]
        x_re, x_im = jnp.concatenate(out_re, axis=0), jnp.concatenate(out_im, axis=0)
        ex = t - 1 if d == 0 else 0
        state[0:1, 0:n] = x_re[ex:ex + 1, :]
        state[0:1, n:2 * n] = x_im[ex:ex + 1, :]
        yd = _dot(x_re.astype(BF16), cb_ref[d, 0:n, :]) + _dot(x_im.astype(BF16), cb_ref[d, n:2 * n, :])
        rows = pl.ds(pl.multiple_of(chunk * t, t), t)
        if d == 0:
            yacc[rows, :] = dsk_ref[...] * u + yd
        else:
            y = yacc[rows, :] + yd
            y = 0.5 * y * (1.0 + jnp.tanh(math.sqrt(2.0 / math.pi) * (y + 0.044715 * y * y * y)))
            y_ref[0] = y * _sigmoid(_bdot(y, wg_ref[...]))

        @pl.when(last)
        def _():
            hf_ref[0, d:d + 1, :] = state[0:1, :]

    @pl.when(s < nc)
    def _():
        phase(0)

    @pl.when(s >= nc)
    def _():
        phase(1)


def _two_pass_chunk(nc):
    return lambda s: jnp.where(s < nc, s, 2 * nc - 1 - s)


def _two_pass_out(nc):
    return lambda s: jnp.where(s < nc, nc - 1, 2 * nc - 1 - s)


def s5_prepare(lam_re, lam_im, log_step, b_re, b_im, c_re, c_im):
    step = jnp.exp(log_step)[:, :, None]
    mag = jnp.exp(lam_re * step)
    lb_re, lb_im = mag * jnp.cos(lam_im * step), mag * jnp.sin(lam_im * step)
    den = lam_re * lam_re + lam_im * lam_im
    q_re = ((lb_re - 1.0) * lam_re + lb_im * lam_im) / den
    q_im = (lb_im * lam_re - (lb_re - 1.0) * lam_im) / den
    bb_re = q_re[..., None] * b_re - q_im[..., None] * b_im
    bb_im = q_re[..., None] * b_im + q_im[..., None] * b_re
    eye = jnp.eye(S5_GROUPS, dtype=F32)

    def blockdiag_in(m):
        return jnp.einsum('dgni,gh->dgihn', m, eye).reshape(2, W_BR, S5_LANES)

    def blockdiag_out(m):
        return jnp.einsum('dgin,gh->dgnhi', m, eye).reshape(2, S5_LANES, W_BR)

    bblk = jnp.concatenate([blockdiag_in(bb_re), blockdiag_in(bb_im)], axis=-1)
    cblk = jnp.concatenate([blockdiag_out(c_re), -blockdiag_out(c_im)], axis=1)
    j = np.arange(SUBLANE, dtype=np.float32)
    pw = jnp.asarray(np.stack([j + 1.0, SUBLANE - j]))[:, :, None]
    arg_re = (lam_re * step).reshape(2, 1, S5_LANES) * pw
    arg_im = (lam_im * step).reshape(2, 1, S5_LANES) * pw
    apow = jnp.concatenate([jnp.exp(arg_re) * jnp.cos(arg_im), jnp.exp(arg_re) * jnp.sin(arg_im)], axis=-1)
    return bblk.astype(BF16), apow, cblk.astype(BF16)


def s5_mixer(u, h0, bblk, apow, cblk, d_skip, w_glu):
    b_, L, _ = u.shape
    t = min(S5_CHUNK, L)
    nc = L // t
    chunk_of, out_of = _two_pass_chunk(nc), _two_pass_out(nc)
    const = lambda a: pl.BlockSpec(a.shape, lambda b, s: (0,) * a.ndim)
    dsk = d_skip[None]
    wg = w_glu.astype(BF16)
    return pl.pallas_call(
        functools.partial(_s5_kernel, nc),
        name="s5_mixer",
        grid=(b_, 2 * nc),
        in_specs=[pl.BlockSpec((1, t, W_BR), lambda b, s: (b, chunk_of(s), 0)),
                  pl.BlockSpec((1, SUBLANE, 2 * S5_LANES), lambda b, s: (b, 0, 0)),
                  const(bblk), const(apow), const(cblk), const(dsk), const(wg)],
        out_specs=[pl.BlockSpec((1, t, W_BR), lambda b, s: (b, out_of(s), 0)),
                   pl.BlockSpec((1, SUBLANE, 2 * S5_LANES), lambda b, s: (b, 0, 0))],
        out_shape=[jax.ShapeDtypeStruct((b_, L, W_BR), F32),
                   jax.ShapeDtypeStruct((b_, SUBLANE, 2 * S5_LANES), F32)],
        scratch_shapes=[pltpu.VMEM((L, W_BR), F32), pltpu.VMEM((SUBLANE, 2 * S5_LANES), F32)],
        compiler_params=_params(("parallel", "arbitrary")),
    )(u, h0, bblk, apow, cblk, dsk, wg)


def _head_mask(h, shape, axis):
    lane = _iota(shape, axis)
    return (lane >= h * RW_HEAD) & (lane < (h + 1) * RW_HEAD)


def _split3(x):
    hi = x.astype(BF16)
    r1 = x - hi.astype(F32)
    mid = r1.astype(BF16)
    return hi, mid, (r1 - mid.astype(F32)).astype(BF16)


def _stack_heads(x):
    xb = x.astype(BF16)
    zero = jnp.zeros_like(xb)
    return jnp.concatenate([jnp.where(_head_mask(h, xb.shape, 1), xb, zero) for h in range(RW_HEADS)], axis=0)


def _split2(x):
    hi = x.astype(BF16)
    return hi, (x - hi.astype(F32)).astype(BF16)


def _dot_exact_rhs(a, b):
    hi, lo = _split2(a)
    return _dot(hi, b) + _dot(lo, b)


def _dot_split(a, b_hi, b_lo):
    hi, lo = _split2(a)
    return _dot(hi, b_hi) + _dot(hi, b_lo) + _dot(lo, b_hi)


RW_INV_BASE = 8


def _cat_to_blockdiag(x):
    t, n = x.shape
    xb = x.astype(BF16)
    tiled = jnp.concatenate([xb] * (n // t), axis=0)
    same = (_iota((n, n), 0) // t) == (_iota((n, n), 1) // t)
    return jnp.where(same, tiled, jnp.zeros_like(tiled))


def _tri_inverse(nmat):
    t = nmat[0].shape[0]
    row, col = _iota(nmat[0].shape, 0), _iota(nmat[0].shape, 1) % t
    same = lambda s: (row // s) == (col // s)
    mul = lambda x, y: _dot(x.astype(BF16), _cat_to_blockdiag(y))
    base = [jnp.where(same(RW_INV_BASE), x, 0.0) for x in nmat]
    eye = jnp.where(row == col, 1.0, 0.0)
    m = [eye - x for x in base]
    p = base
    for _ in range(RW_INV_BASE.bit_length() - 2):
        p = [mul(x, x) for x in p]
        m = [x + mul(x, y) for x, y in zip(m, p)]
    s = 2 * RW_INV_BASE
    while s <= t:
        off = same(s) & jnp.logical_not(same(s // 2))
        mc = [mul(x, jnp.where(off, y, 0.0)) for x, y in zip(m, nmat)]
        m = [x - mul(y, x) for x, y in zip(m, mc)]
        s *= 2
    return m


def _wkv_local(d, r, logw, kd, v, kk, a):
    t = r[0].shape[0]
    bf = lambda x: x.astype(BF16)
    each = lambda f, *ls: [f(*xs) for xs in zip(*ls)]
    ri, ci = _iota((t, t), 0), _iota((t, t), 1)
    tri = ((ci <= ri) if d == 0 else (ci >= ri)).astype(BF16)
    row, col = _iota((t, RW_HEADS * t), 0), _iota((t, RW_HEADS * t), 1) % t
    incl = (col <= row) if d == 0 else (col >= row)
    strict = (col < row) if d == 0 else (col > row)
    dn = W_BR
    diag = _iota((dn, dn), 0) == _iota((dn, dn), 1)

    cs = each(lambda x: sum(_dot(tri, part) for part in _split3(x)), logw)
    g_tot = each(lambda x: jnp.exp(jnp.sum(x, axis=0, keepdims=True)), logw)
    g_inv = each(lambda c: jnp.exp(-c), cs)
    bt = each(lambda k_, a_, g: k_ * a_ * g, kk, a, g_inv)
    kt = each(lambda k_, g: k_ * g, kd, g_inv)
    qa = each(lambda k_, c, w: k_ * jnp.exp(c - w), kk, cs, logw)
    rh = each(lambda r_, c: r_ * jnp.exp(c), r, cs)
    bts, kts, vs = each(_stack_heads, bt), each(_stack_heads, kt), each(_stack_heads, v)
    bhs = each(lambda x, g: _stack_heads(x * g), bt, g_tot)
    khs = each(lambda x, g: _stack_heads(x * g), kt, g_tot)
    a_ab = each(lambda q, b: jnp.where(strict, _dot_nt(bf(q), b), 0.0), qa, bts)
    a_ak = each(lambda q, k_: bf(jnp.where(strict, _dot_nt(bf(q), k_), 0.0)), qa, kts)
    a_rb = each(lambda q, b: bf(jnp.where(incl, _dot_nt(bf(q), b), 0.0)), rh, bts)
    a_rk = each(lambda q, k_: bf(jnp.where(incl, _dot_nt(bf(q), k_), 0.0)), rh, kts)
    m = each(bf, _tri_inverse(a_ab))
    x1 = each(lambda a_, v_: _stack_heads(_dot(a_, v_)), a_ak, vs)
    ul = each(lambda m_, x: _stack_heads(-_dot(m_, x)), m, x1)
    w0 = each(lambda m_, q: _stack_heads(_dot(m_, _stack_heads(q))), m, qa)
    yloc = each(lambda ak, v_, ab, u: _dot(ak, v_) + _dot(ab, u), a_rk, vs, a_rb, ul)
    yh = each(lambda r_, ab, w: r_ - _dot(ab, w), rh, a_rb, w0)
    p_bd = each(lambda g, b, w: jnp.where(diag, g, 0.0) - _dot_tn(b, w), g_tot, bhs, w0)
    q_bd = each(lambda b, u, k_, v_: _dot_tn(b, u) + _dot_tn(k_, v_), bhs, ul, khs, vs)
    return list(zip(yloc, yh, p_bd, q_bd))


def _rwkv_kernel(ng, cur_ref, prev_ref, next_ref, h0_ref, mu_ref, w0_ref, a0_ref, lrh_ref, lrl_ref,
                 kk_ref, ka_ref, rk_ref, lnw_ref, lnb_ref, y_ref, hf_ref, yacc, hst):
    s = pl.program_id(1)
    t = cur_ref.shape[1]
    n = W_BR
    ri, ci = _iota((n, n), 0), _iota((n, n), 1)
    head_sum = ((ri // RW_HEAD) == (ci // RW_HEAD)).astype(BF16)

    def phase(d):
        chunk = s if d == 0 else 2 * ng - 1 - s
        first = (s == 0) if d == 0 else (s == ng)
        last = (s == ng - 1) if d == 0 else (s == 2 * ng - 1)

        @pl.when(first)
        def _():
            hst[...] = h0_ref[0, d]

        cur = cur_ref[0]
        prev_row, next_row = _halo_rows(prev_ref, next_ref, chunk, ng)
        xm, xp = _shift_rows(cur, prev_row, next_row)
        p = cur + (0.5 * (xm + xp) - cur) * mu_ref[...]
        r, k, v, lr = p[:, 0:n], p[:, n:2 * n], p[:, 2 * n:3 * n], p[:, 3 * n:4 * n]
        kk = k * kk_ref[...]
        kk = kk * lax.rsqrt(jnp.maximum(_dot_exact_rhs(kk * kk, head_sum), 1e-24))
        lane = _iota(lr.shape, 1)
        feats = jnp.where(lane < 2 * RW_W_RANK, jnp.tanh(lr),
                          jnp.where(lane < 2 * (RW_W_RANK + RW_A_RANK), lr, _sigmoid(lr)))
        cols = (2 if d == 0 else 3) * n
        low = _dot_split(feats, lrh_ref[d, :, 0:cols], lrl_ref[d, :, 0:cols])
        w = w0_ref[d:d + 1, :] + low[:, 0:n]
        logw = -math.exp(-0.5) * _sigmoid(w)
        a = _sigmoid(a0_ref[d:d + 1, :] + low[:, n:2 * n])
        kd = k * (1.0 + (a - 1.0) * ka_ref[...])
        nsub = t // RW_SUB
        subs = lambda x: [x[q * RW_SUB:(q + 1) * RW_SUB] for q in range(nsub)]
        local = _wkv_local(d, subs(r), subs(logw), subs(kd), subs(v), subs(kk), subs(a))
        h = hst[...]
        ys = [None] * nsub
        for q in (range(nsub) if d == 0 else range(nsub - 1, -1, -1)):
            yloc, yh, p_bd, q_bd = local[q]
            hb = h.astype(BF16)
            ys[q] = yloc + _dot(yh.astype(BF16), hb)
            h = _dot(p_bd.astype(BF16), hb) + q_bd
        hst[...] = h
        yd = jnp.concatenate(ys, axis=0)
        rows = pl.ds(pl.multiple_of(chunk * t, t), t)
        if d == 0:
            yacc[rows, :] = yd
        else:
            y = yacc[rows, :] + yd
            mean = _dot_exact_rhs(y, head_sum) * (1.0 / RW_HEAD)
            yc = y - mean
            var = _dot_exact_rhs(yc * yc, head_sum) * (1.0 / RW_HEAD)
            y = yc * lax.rsqrt(var + RW_GN_EPS) * lnw_ref[...] + lnb_ref[...]
            bonus = _dot_exact_rhs(r * k * rk_ref[...], head_sum) * v
            y_ref[0] = (y + bonus) * low[:, 2 * n:3 * n]

        @pl.when(last)
        def _():
            hf_ref[0, d] = hst[...]

    @pl.when(s < ng)
    def _():
        phase(0)

    @pl.when(s >= ng)
    def _():
        phase(1)


def rwkv_prepare(mu, w0, w2, a0, a2, g2, k_k, k_a, r_k, ln_w, ln_b):
    n = W_BR
    mu_p = jnp.zeros((1, RW_PAD), F32).at[0, :RW_COLS].set(mu)
    w2_p = jnp.zeros((2, n, n), F32)
    a2_p = jnp.zeros((2, n, n), F32)
    for d in range(2):
        w2_p = w2_p.at[d, d * RW_W_RANK:(d + 1) * RW_W_RANK].set(w2[d])
        o = 2 * RW_W_RANK + d * RW_A_RANK
        a2_p = a2_p.at[d, o:o + RW_A_RANK].set(a2[d])
    o = 2 * RW_W_RANK + 2 * RW_A_RANK
    g2_p = jnp.zeros((n, n), F32).at[o:o + RW_G_RANK].set(g2)
    low = jnp.concatenate([w2_p, a2_p, jnp.broadcast_to(g2_p, (2, n, n))], axis=-1)
    low_hi = low.astype(BF16)
    low_lo = (low - low_hi.astype(F32)).astype(BF16)
    return (mu_p, w0, a0, low_hi, low_lo, k_k[None], k_a[None], r_k.reshape(1, n), ln_w[None], ln_b[None])


def rwkv_mixer(rw, h0, prm):
    b_, L, _ = rw.shape
    t = min(RW_STEP, L)
    ng = L // t
    chunk_of, out_of = _two_pass_chunk(ng), _two_pass_out(ng)
    prev_spec, next_spec = _halo_specs(t, ng, RW_PAD, 0, chunk_of)
    const = lambda a: pl.BlockSpec(a.shape, lambda b, s: (0,) * a.ndim)
    st_spec = pl.BlockSpec((1, 2, W_BR, W_BR), lambda b, s: (b, 0, 0, 0))
    return pl.pallas_call(
        functools.partial(_rwkv_kernel, ng),
        name="rwkv_mixer",
        grid=(b_, 2 * ng),
        in_specs=[pl.BlockSpec((1, t, RW_PAD), lambda b, s: (b, chunk_of(s), 0)), prev_spec, next_spec, st_spec]
                 + [const(a) for a in prm],
        out_specs=[pl.BlockSpec((1, t, W_BR), lambda b, s: (b, out_of(s), 0)), st_spec],
        out_shape=[jax.ShapeDtypeStruct((b_, L, W_BR), F32), jax.ShapeDtypeStruct((b_, 2, W_BR, W_BR), F32)],
        scratch_shapes=[pltpu.VMEM((L, W_BR), F32), pltpu.VMEM((W_BR, W_BR), F32)],
        compiler_params=_params(("parallel", "arbitrary")),
    )(rw, rw, rw, h0, *prm)


def _m2_kernel(nc, z_ref, cur_ref, prev_ref, next_ref, dt_ref, h0_ref, cw_ref, cb_ref, ex_ref, a_ref, dtb_ref,
               dsk_ref, nw_ref, y_ref, hf_ref, yacc, xbc_sc, hst):
    s = pl.program_id(1)
    t = cur_ref.shape[1]
    n = W_BR
    ns = M2_STATE
    ri, ci = _iota((t, t), 0), _iota((t, t), 1)

    def phase(d):
        chunk = s if d == 0 else 2 * nc - 1 - s
        first = (s == 0) if d == 0 else (s == nc)
        last = (s == nc - 1) if d == 0 else (s == 2 * nc - 1)

        @pl.when(first)
        def _():
            hst[...] = h0_ref[0, d]

        rows = pl.ds(pl.multiple_of(chunk * t, t), t)
        if d == 0:
            cur = cur_ref[0]
            prev_row, next_row = _halo_rows(prev_ref, next_ref, chunk, nc)
            xm, xp = _shift_rows(cur, prev_row, next_row)
            xbc = _silu(cb_ref[...] + xm * cw_ref[0:1, :] + cur * cw_ref[1:2, :] + xp * cw_ref[2:3, :])
            xbc_sc[rows, :] = xbc
        else:
            xbc = xbc_sc[rows, :]
        xs, bm, cm = xbc[:, 0:n], xbc[:, n:2 * n], xbc[:, 2 * n:3 * n]
        raw = sum(_dot(part, ex_ref[d]) for part in _split3(dt_ref[0])) + dtb_ref[d:d + 1, :]
        dtd = jnp.maximum(raw, 0.0) + jnp.log(1.0 + jnp.exp(-jnp.abs(raw)))
        da = dtd * a_ref[d:d + 1, :]
        incl = (ci <= ri) if d == 0 else (ci >= ri)
        tri = incl.astype(BF16)
        da_parts = _split3(da)
        acs = sum(_dot(tri, part) for part in da_parts)
        tot = jnp.sum(da, axis=0, keepdims=True)
        xdt = xs * dtd
        xdt_b = xdt.astype(BF16)
        bm_b, cm_b = bm.astype(BF16), cm.astype(BF16)
        sel = jnp.where(_iota((SUBLANE, n), 1) // M2_HEADDIM == _iota((SUBLANE, n), 0), 1.0 / M2_HEADDIM,
                        0.0).astype(BF16)
        acs_t = sum(_dot_nt(sel, part) for part in _split3(acs))
        ydiag = jnp.zeros((t, n), F32)
        for g in range(M2_GROUPS):
            cb = _dot_nt(cm_b[:, g * ns:(g + 1) * ns], bm_b[:, g * ns:(g + 1) * ns])
            for h in range(g * (M2_HEADS // M2_GROUPS), (g + 1) * (M2_HEADS // M2_GROUPS)):
                seg = acs[:, h * M2_HEADDIM:h * M2_HEADDIM + 1] - acs_t[h:h + 1, :]
                scores = cb * jnp.exp(jnp.where(incl, seg, -jnp.inf))
                ydiag = jnp.where(_head_mask(h, (t, n), 1), _dot(scores.astype(BF16), xdt_b), ydiag)
        h_in = hst[...]
        h_b = h_in.astype(BF16)
        lane = _iota((t, n), 1)
        yoff = jnp.where(lane < n // M2_GROUPS, _dot(cm_b[:, 0:ns], h_b), _dot(cm_b[:, ns:2 * ns], h_b))
        yd = ydiag + jnp.exp(acs) * yoff
        xdec = (xdt * jnp.exp(tot - acs)).astype(BF16)
        lane_s = _iota((ns, n), 1)
        new = jnp.where(lane_s < n // M2_GROUPS, _dot_tn(bm_b[:, 0:ns], xdec), _dot_tn(bm_b[:, ns:2 * ns], xdec))
        hst[...] = h_in * jnp.exp(tot) + new
        if d == 0:
            yacc[rows, :] = yd + dsk_ref[...] * xs
        else:
            y = (yacc[rows, :] + yd) * _silu(z_ref[0])
            y_ref[0] = _rms(y) * nw_ref[...]

        @pl.when(last)
        def _():
            hf_ref[0, d] = hst[...]

    @pl.when(s < nc)
    def _():
        phase(0)

    @pl.when(s >= nc)
    def _():
        phase(1)


def mamba_prepare(conv_w, conv_b, a_log, dt_bias, d_skip, norm_w):
    n = W_BR
    rep = lambda v: jnp.repeat(v, M2_HEADDIM, axis=-1)
    expand = jnp.zeros((2, LANE, n), F32)
    for d in range(2):
        for h in range(M2_HEADS):
            expand = expand.at[d, d * M2_HEADS + h, h * M2_HEADDIM:(h + 1) * M2_HEADDIM].set(1.0)
    return (conv_w, conv_b[None], expand.astype(BF16), rep(-jnp.exp(a_log)), rep(dt_bias), rep(d_skip)[None],
            norm_w[None])


def mamba_mixer(z, xbc, dt, h0, prm):
    b_, L, _ = z.shape
    t = min(M2_CHUNK, L)
    nc = L // t
    chunk_of, out_of = _two_pass_chunk(nc), _two_pass_out(nc)
    prev_spec, next_spec = _halo_specs(t, nc, M2_XBC, 0, chunk_of)
    const = lambda a: pl.BlockSpec(a.shape, lambda b, s: (0,) * a.ndim)
    st_spec = pl.BlockSpec((1, 2, M2_STATE, W_BR), lambda b, s: (b, 0, 0, 0))
    seq = lambda w: pl.BlockSpec((1, t, w), lambda b, s: (b, chunk_of(s), 0))
    return pl.pallas_call(
        functools.partial(_m2_kernel, nc),
        name="mamba_mixer",
        grid=(b_, 2 * nc),
        in_specs=[seq(W_BR), seq(M2_XBC), prev_spec, next_spec, seq(LANE), st_spec] + [const(a) for a in prm],
        out_specs=[pl.BlockSpec((1, t, W_BR), lambda b, s: (b, out_of(s), 0)), st_spec],
        out_shape=[jax.ShapeDtypeStruct((b_, L, W_BR), F32),
                   jax.ShapeDtypeStruct((b_, 2, M2_STATE, W_BR), F32)],
        scratch_shapes=[pltpu.VMEM((L, W_BR), F32), pltpu.VMEM((L, M2_XBC), F32),
                        pltpu.VMEM((M2_STATE, W_BR), F32)],
        compiler_params=_params(("parallel", "arbitrary")),
    )(z, xbc, xbc, xbc, dt, h0, *prm)


def _merge_kernel(x_ref, g_ref, sh_ref, sc_ref, gt_ref, y0_ref, y1_ref, y2_ref, y3_ref, wg_ref, wb_ref, wo_ref,
                  o_ref):
    x = x_ref[...]
    h = _modulate(x, g_ref[...], sh_ref[0], sc_ref[0]).astype(BF16)
    acc = jnp.zeros(x.shape, F32)
    for i, y_ref in enumerate((y0_ref, y1_ref, y2_ref, y3_ref)):
        gate = _sigmoid(_dot(h, wg_ref[i]))
        acc += gate * _dot(y_ref[...].astype(BF16), wb_ref[i])
    o_ref[...] = x + gt_ref[0] * _dot(acc.astype(BF16), wo_ref[...])


def merge_branches(xt, g, shift, scale, gate, ys, w_gate, w_branch, w_out, rows_per_mod, tm):
    n = xt.shape[0]
    tiles_per_mod = rows_per_mod // tm
    mod_spec = pl.BlockSpec((1, 1, D_MODEL), lambda i: (i // tiles_per_mod, 0, 0))
    const = lambda a: pl.BlockSpec(a.shape, lambda i: (0,) * a.ndim)
    tok = lambda w: pl.BlockSpec((tm, w), lambda i: (i, 0))
    return pl.pallas_call(
        _merge_kernel,
        name="merge_branches",
        grid=(n // tm,),
        in_specs=[tok(D_MODEL), const(g), mod_spec, mod_spec, mod_spec] + [tok(W_BR)] * N_BRANCH
                 + [const(w_gate), const(w_branch), const(w_out)],
        out_specs=tok(D_MODEL),
        out_shape=jax.ShapeDtypeStruct((n, D_MODEL), F32),
        compiler_params=_params(("parallel",)),
    )(xt, g, shift, scale, gate, *ys, w_gate, w_branch, w_out)


def _router_kernel(x_ref, g_ref, sh_ref, sc_ref, wr_ref, comb_ref):
    u = _modulate(x_ref[...], g_ref[...], sh_ref[0], sc_ref[0])
    logits = _dot_split(u, wr_ref[0], wr_ref[1])
    lane = _iota(logits.shape, 1).astype(F32)
    neg = -jnp.inf
    is_grp = (lane >= MOE_EXPERTS) & (lane < MOE_EXPERTS + MOE_GROUPS)
    gl = jnp.where(is_grp, logits, neg)
    gmax = jnp.max(gl, axis=-1, keepdims=True)
    grp_p = 1.0 / jnp.sum(jnp.exp(gl - gmax), axis=-1, keepdims=True)
    grp_idx = jnp.min(jnp.where(gl == gmax, lane, 4.0 * LANE), axis=-1, keepdims=True) - MOE_EXPERTS
    in_grp = (lane >= grp_idx * MOE_PER_GROUP) & (lane < (grp_idx + 1) * MOE_PER_GROUP)
    el = jnp.where(in_grp, logits, neg)
    v1 = jnp.max(el, axis=-1, keepdims=True)
    i1 = jnp.min(jnp.where(el == v1, lane, 4.0 * LANE), axis=-1, keepdims=True)
    el2 = jnp.where(lane == i1, neg, el)
    v2 = jnp.max(el2, axis=-1, keepdims=True)
    i2 = jnp.min(jnp.where(el2 == v2, lane, 4.0 * LANE), axis=-1, keepdims=True)
    e2 = jnp.exp(v2 - v1)
    w1 = grp_p / (1.0 + e2)
    comb = jnp.where(lane == i1, w1, jnp.where(lane == i2, w1 * e2, 0.0))
    comb_ref[...] = jnp.where(lane == MOE_GROUP_LANE, grp_idx, comb)


def moe_router(xt, g, shift, scale, w_route, rows_per_mod, tm):
    n = xt.shape[0]
    tiles_per_mod = rows_per_mod // tm
    mod_spec = pl.BlockSpec((1, 1, D_MODEL), lambda i: (i // tiles_per_mod, 0, 0))
    return pl.pallas_call(
        _router_kernel,
        name="moe_router",
        grid=(n // tm,),
        in_specs=[pl.BlockSpec((tm, D_MODEL), lambda i: (i, 0)), pl.BlockSpec((1, D_MODEL), lambda i: (0, 0)),
                  mod_spec, mod_spec, pl.BlockSpec((2, D_MODEL, LANE), lambda i: (0, 0, 0))],
        out_specs=pl.BlockSpec((tm, LANE), lambda i: (i, 0)),
        out_shape=jax.ShapeDtypeStruct((n, LANE), F32),
        compiler_params=_params(("parallel",)),
    )(xt, g, shift, scale, w_route)


def _moe_kernel(final_norm, meta_ref, x_ref, g_ref, sh_ref, sc_ref, gt_ref, comb_ref, wg_ref, wu_ref, wd_ref, nf_ref,
                o_ref, pt_sc, xs_sc, cs_sc, acc):
    i = pl.program_id(0)
    e = pl.program_id(1)
    tm = x_ref.shape[0]

    @pl.when(e == 0)
    def _():
        u = _modulate(x_ref[...], g_ref[...], sh_ref[0], sc_ref[0]).astype(BF16)
        comb = comb_ref[...]
        lane = _iota((tm, LANE), 1)
        member = comb[:, MOE_GROUP_LANE:MOE_GROUP_LANE + 1] == lane.astype(F32)
        tri = jnp.where(_iota((tm, tm), 1) <= _iota((tm, tm), 0), 1.0, 0.0).astype(BF16)
        rank = _dot(tri, jnp.where(member, 1.0, 0.0).astype(BF16))
        offs = jnp.zeros((tm, LANE), jnp.int32)
        for grp in range(MOE_GROUPS):
            offs = jnp.where(lane == grp, meta_ref[i, grp], offs)
        posmat = jnp.where(member, offs.astype(F32) + rank - 1.0, 0.0)
        pos_col = jnp.sum(posmat, axis=-1, keepdims=True)
        ones = jnp.ones((SUBLANE, LANE), BF16)
        pos_row = sum(_dot_nt(ones, part) for part in _split3(posmat))[0:1, :]
        pt_sc[...] = jnp.where(pos_col == _iota((tm, tm), 1).astype(F32), 1.0, 0.0).astype(BF16)
        p = jnp.where(pos_row == _iota((tm, tm), 0).astype(F32), 1.0, 0.0).astype(BF16)
        xs_sc[...] = _dot(p, u).astype(BF16)
        cs_sc[...] = sum(_dot(p, part) for part in _split3(comb))
        acc[...] = jnp.zeros_like(acc)

    grp = e // MOE_PER_GROUP
    start = meta_ref[i, grp]
    count = meta_ref[i, MOE_GROUPS + grp]
    lo = start // MOE_SUB
    hi = jnp.where(count > 0, (start + count + MOE_SUB - 1) // MOE_SUB, lo)

    def body(j, carry):
        rows = pl.ds(pl.multiple_of(j * MOE_SUB, MOE_SUB), MOE_SUB)
        xs = xs_sc[rows, :]
        h = _silu(_dot(xs, wg_ref[0, 0])) * _dot(xs, wu_ref[0, 0])
        cs = cs_sc[rows, :]
        w = jnp.sum(jnp.where(_iota(cs.shape, 1) == e, cs, 0.0), axis=-1, keepdims=True)
        acc[rows, :] += w * _dot(h.astype(BF16), wd_ref[0, 0])
        return carry

    lax.fori_loop(lo, hi, body, 0)

    @pl.when(e == MOE_EXPERTS - 1)
    def _():
        y = x_ref[...] + gt_ref[0] * _dot(pt_sc[...], acc[...].astype(BF16))
        o_ref[...] = _rms(y) * nf_ref[...] if final_norm else y


def moe_experts(xt, g, shift, scale, gate, comb, layer, w_gate, w_up, w_down, norm_final, final_norm, rows_per_mod,
                tm):
    n = xt.shape[0]
    tiles_per_mod = rows_per_mod // tm
    gid = comb[:, MOE_GROUP_LANE].astype(jnp.int32).reshape(n // tm, tm)
    counts = jnp.sum(gid[:, :, None] == jnp.arange(MOE_GROUPS, dtype=jnp.int32), axis=1, dtype=jnp.int32)
    meta = jnp.concatenate([jnp.cumsum(counts, axis=1) - counts, counts], axis=1)
    mod_spec = pl.BlockSpec((1, 1, D_MODEL), lambda i, e, m: (i // tiles_per_mod, 0, 0))
    return pl.pallas_call(
        functools.partial(_moe_kernel, final_norm),
        name="moe_experts",
        grid_spec=pltpu.PrefetchScalarGridSpec(
            num_scalar_prefetch=1,
            grid=(n // tm, MOE_EXPERTS),
            in_specs=[pl.BlockSpec((tm, D_MODEL), lambda i, e, m: (i, 0)),
                      pl.BlockSpec((1, D_MODEL), lambda i, e, m: (0, 0)), mod_spec, mod_spec, mod_spec,
                      pl.BlockSpec((tm, LANE), lambda i, e, m: (i, 0)),
                      pl.BlockSpec((1, 1, D_MODEL, MOE_FF), lambda i, e, m: (layer, e, 0, 0)),
                      pl.BlockSpec((1, 1, D_MODEL, MOE_FF), lambda i, e, m: (layer, e, 0, 0)),
                      pl.BlockSpec((1, 1, MOE_FF, D_MODEL), lambda i, e, m: (layer, e, 0, 0)),
                      pl.BlockSpec((1, D_MODEL), lambda i, e, m: (0, 0))],
            out_specs=pl.BlockSpec((tm, D_MODEL), lambda i, e, m: (i, 0)),
            scratch_shapes=[pltpu.VMEM((tm, tm), BF16), pltpu.VMEM((tm, D_MODEL), BF16),
                            pltpu.VMEM((tm, LANE), F32), pltpu.VMEM((tm, D_MODEL), F32)]),
        out_shape=jax.ShapeDtypeStruct((n, D_MODEL), F32),
        compiler_params=_params(("parallel", "arbitrary")),
    )(meta, xt, g, shift, scale, gate, comb, w_gate, w_up, w_down, norm_final)


def _mix_weights(w_in):
    o = 0
    parts = []
    for cols, padded in ((HY_COLS, HY_COLS), (S5_COLS, S5_COLS), (RW_COLS, RW_PAD), (M2_COLS, M2_PAD)):
        parts.append(jnp.pad(w_in[:, o:o + cols], ((0, 0), (0, padded - cols))))
        o += cols
    w_gate = w_in[:, o:].reshape(D_MODEL, N_BRANCH, D_MODEL).transpose(1, 0, 2)
    return jnp.concatenate(parts, axis=1).astype(BF16), w_gate.astype(BF16)


def kernel(x, c, ctx, c_ctx, mod_w, mod_b, norm_mix, norm_ffn, w_in, hy_conv_w, hy_conv_b, hy_f_w1, hy_f_b1, hy_f_w2, hy_f_b2, hy_f_w3, hy_f_freq, hy_bias, s5_lam_re, s5_lam_im, s5_log_step, s5_b_re, s5_b_im, s5_c_re, s5_c_im, s5_d, s5_w_glu, rw_mu, rw_w0, rw_w2, rw_a0, rw_a2, rw_g2, rw_k_k, rw_k_a, rw_r_k, rw_ln_w, rw_ln_b, m2_conv_w, m2_conv_b, m2_a_log, m2_dt_bias, m2_d, m2_norm_w, w_branch, w_out, moe_w_group, moe_w_expert, moe_w_gate, moe_w_up, moe_w_down, norm_final):
    b_, L, _ = x.shape
    lc = ctx.shape[1]
    depth = mod_w.shape[0]
    tm = 512
    tmc = min(tm, lc)

    cvec = jnp.zeros((SUBLANE, D_MODEL), F32).at[:b_].set(c).at[b_].set(c_ctx)
    mod = adaln_mod(cvec, mod_w, mod_b)

    xt = x.reshape(b_ * L, D_MODEL)
    ct = ctx.reshape(b_ * lc, D_MODEL)
    moe_wg, moe_wu, moe_wd = moe_w_gate.astype(BF16), moe_w_up.astype(BF16), moe_w_down.astype(BF16)
    for l in range(depth):
        ctx_out = l < depth - 1
        mx = mod[l, :b_].reshape(b_, 1, N_MOD, D_MODEL)
        mc = mod[l, b_:b_ + 1].reshape(1, 1, N_MOD, D_MODEL)
        sh1, sc1, g1, sh2, sc2, g2 = (mx[:, :, i] for i in range(N_MOD))
        csh1, csc1, cg1, csh2, csc2, cg2 = (mc[:, :, i] for i in range(N_MOD))
        w_mix, w_gate = _mix_weights(w_in[l])
        nm, nf = norm_mix[l][None], norm_ffn[l][None]

        s5_prm = s5_prepare(s5_lam_re[l], s5_lam_im[l], s5_log_step[l], s5_b_re[l], s5_b_im[l], s5_c_re[l],
                            s5_c_im[l])
        rw_prm = rwkv_prepare(rw_mu[l], rw_w0[l], rw_w2[l], rw_a0[l], rw_a2[l], rw_g2[l], rw_k_k[l], rw_k_a[l],
                              rw_r_k[l], rw_ln_w[l], rw_ln_b[l])
        m2_prm = mamba_prepare(m2_conv_w[l], m2_conv_b[l], m2_a_log[l], m2_dt_bias[l], m2_d[l], m2_norm_w[l])
        hy_f = (hy_f_w1[l], hy_f_b1[l], hy_f_w2[l], hy_f_b2[l], hy_f_w3[l], hy_f_freq[l])

        def mixers(tokens, n_tok, shift, scale, rows_per_mod, tile, states, want_hyena):
            hy, s5, rw, m2z, m2x, m2dt = in_projection(tokens, nm, shift, scale, w_mix, rows_per_mod, tile)
            seq = lambda a: a.reshape(b_, n_tok, a.shape[-1])
            y_hy = None
            if want_hyena:
                t = min(HY_BLOCK, n_tok)
                ghat = hyena_filter_spectra(hyena_filter(n_tok, *hy_f), n_tok, t)
                y_hy = hyena_mixer(seq(hy), ghat, hy_conv_w[l], hy_conv_b[l], hy_bias[l])
            y_s5, s5_h = s5_mixer(seq(s5), states[0], *s5_prm, s5_d[l], s5_w_glu[l])
            y_rw, rw_h = rwkv_mixer(seq(rw), states[1], rw_prm)
            y_m2, m2_h = mamba_mixer(seq(m2z), seq(m2x), seq(m2dt), states[2], m2_prm)
            flat = lambda a: None if a is None else a.reshape(b_ * n_tok, W_BR)
            return [flat(y_hy), flat(y_s5), flat(y_rw), flat(y_m2)], (s5_h, rw_h, m2_h)

        zero_states = (jnp.zeros((b_, SUBLANE, 2 * S5_LANES), F32), jnp.zeros((b_, 2, W_BR, W_BR), F32),
                       jnp.zeros((b_, 2, M2_STATE, W_BR), F32))
        ys_c, ctx_states = mixers(ct, lc, csh1, csc1, b_ * lc, tmc, zero_states, ctx_out)
        ys_x, _ = mixers(xt, L, sh1, sc1, L, tm, ctx_states, True)

        w_route = jnp.zeros((D_MODEL, LANE), F32)
        w_route = w_route.at[:, :MOE_EXPERTS].set(moe_w_expert[l].transpose(1, 0, 2).reshape(D_MODEL, MOE_EXPERTS))
        w_route = w_route.at[:, MOE_EXPERTS:MOE_EXPERTS + MOE_GROUPS].set(moe_w_group[l])
        w_route = jnp.stack(_split2(w_route))
        wb, wo = w_branch[l].astype(BF16), w_out[l].astype(BF16)
        nfin = norm_final[None]

        def channel_mix(tokens, ys, mods, rows_per_mod, tile, final):
            s1, c1, gt1, s2, c2, gt2 = mods
            t1 = merge_branches(tokens, nm, s1, c1, gt1, ys, w_gate, wb, wo, rows_per_mod, tile)
            comb = moe_router(t1, nf, s2, c2, w_route, rows_per_mod, tile)
            moe_tile = min(MOE_TILE, rows_per_mod)
            return moe_experts(t1, nf, s2, c2, gt2, comb, l, moe_wg, moe_wu, moe_wd, nfin, final, rows_per_mod,
                               moe_tile)

        xt = channel_mix(xt, ys_x, (sh1, sc1, g1, sh2, sc2, g2), L, tm, l == depth - 1)
        if ctx_out:
            ct = channel_mix(ct, ys_c, (csh1, csc1, cg1, csh2, csc2, cg2), b_ * lc, tmc, False)
    return xt.reshape(b_, L, D_MODEL)
```
